```python
import math
import jax, jax.numpy as jnp
from jax import lax
import numpy as np

D_MODEL = 2048
BATCH = 2
SEQ = 4096
DEPTH = 2

N_MIXERS = 2
BLOCK = 128
HEAD_DIM = 128
DIL_GROUPS = ((128, 1), (512, 4), (2048, 16))
N_GROUPS = len(DIL_GROUPS)
A_HEADS_PER_GROUP = D_MODEL // HEAD_DIM
A_HEADS = N_GROUPS * A_HEADS_PER_GROUP
A_PROJ = N_GROUPS * 3 * A_HEADS_PER_GROUP * HEAD_DIM
B_HEADS = D_MODEL // HEAD_DIM
B_PROJ = 3 * B_HEADS * HEAD_DIM
MIX_WIDTH = D_MODEL
REL_BUCKETS = 32
REL_MAX_DIST = 2048
PEER_HEADS = 8
N_SUBKEYS = 128
N_EXPERTS = N_SUBKEYS * N_SUBKEYS
PEER_QDIM = 256
PEER_TOPK = 16
PEER_CHUNK = 128
N_A_LAYERS = (DEPTH + 1) // 2
N_B_LAYERS = DEPTH // 2
EPS = 1e-6

kernel_name = "hybrid_dilated_stickbreak_peer_adaln"


def rms_norm(x, g):
    xf = x.astype(jnp.float32)
    y = xf * lax.rsqrt(jnp.mean(xf * xf, axis=-1, keepdims=True) + EPS)
    return (y * g).astype(x.dtype)


def modulate(xn, shift, scale):
    return xn * (1 + scale[:, None, :]) + shift[:, None, :]


def rel_bucket(dist):
    exact = REL_BUCKETS // 2
    d_f = jnp.maximum(dist, exact).astype(jnp.float32)
    log_b = exact + (jnp.log(d_f / exact) / math.log(REL_MAX_DIST / exact)
                     * (REL_BUCKETS - exact)).astype(jnp.int32)
    return jnp.where(dist < exact, dist, jnp.minimum(log_b, REL_BUCKETS - 1))


def dilated_band(q, k, v, table_g, window, dil):
    b_, s_, h_, e_ = q.shape
    w_steps = window // dil
    sub_len = s_ // dil
    nb = -(-sub_len // BLOCK)
    lp = nb * BLOCK

    def to_sub(t):
        return t.reshape(b_, sub_len, dil, h_, e_).transpose(0, 2, 1, 3, 4)

    qs = jnp.pad(to_sub(q), ((0, 0), (0, 0), (0, lp - sub_len), (0, 0), (0, 0)))
    qs = qs.reshape(b_, dil, nb, BLOCK, h_, e_)

    def band(t):
        tp = jnp.pad(to_sub(t), ((0, 0), (0, 0), (BLOCK, lp - sub_len), (0, 0), (0, 0)))
        tp = tp.reshape(b_, dil, nb + 1, BLOCK, h_, e_)
        return jnp.concatenate([tp[:, :, :-1], tp[:, :, 1:]], axis=3)

    ks, vs = band(k), band(v)
    qi = jnp.arange(BLOCK)[:, None]
    ci = jnp.arange(2 * BLOCK)[None, :]
    delta = qi + BLOCK - ci
    in_band = (delta >= 0) & (delta <= w_steps)
    key_pos = jnp.arange(nb)[:, None] * BLOCK + ci - BLOCK
    mask = in_band[None] & (key_pos >= 0)[:, None, :]
    bucket = rel_bucket(jnp.clip(delta, 0, w_steps) * dil)
    bias = jnp.moveaxis(table_g[bucket], -1, 0).astype(jnp.float32)

    logits = jnp.einsum('bdnqhe,bdnkhe->bdnhqk', qs, ks).astype(jnp.float32) / math.sqrt(e_)
    logits = jnp.where(mask[:, None], logits + bias, -jnp.inf)
    lse = jax.nn.logsumexp(logits, axis=-1)
    p = jnp.exp(logits - lse[..., None])
    o = jnp.einsum('bdnhqk,bdnkhe->bdnqhe', p.astype(v.dtype), vs)
    o = o.reshape(b_, dil, lp, h_, e_)[:, :, :sub_len].transpose(0, 2, 1, 3, 4).reshape(b_, s_, h_, e_)
    lse = lse.transpose(0, 1, 2, 4, 3).reshape(b_, dil, lp, h_)[:, :, :sub_len]
    lse = lse.transpose(0, 2, 1, 3).reshape(b_, s_, h_)
    return o, lse


def dilated_attention(xm, w_in, q_gain, k_gain, rel_bias):
    b_, s_, _ = xm.shape
    proj = (xm @ w_in).reshape(b_, s_, N_GROUPS, 3, A_HEADS_PER_GROUP, HEAD_DIM)
    outs, lses = [], []
    for g, (window, dil) in enumerate(DIL_GROUPS):
        q = rms_norm(proj[:, :, g, 0], q_gain[g])
        k = rms_norm(proj[:, :, g, 1], k_gain[g])
        v = proj[:, :, g, 2]
        table_g = rel_bias[:, g * A_HEADS_PER_GROUP:(g + 1) * A_HEADS_PER_GROUP]
        o, lse = dilated_band(q, k, v, table_g, window, dil)
        outs.append(o)
        lses.append(lse)
    alpha = jax.nn.softmax(jnp.stack(lses), axis=0)
    out = jnp.einsum('gbsh,gbshe->bshe', alpha, jnp.stack(outs).astype(jnp.float32))
    return out.astype(xm.dtype).reshape(b_, s_, MIX_WIDTH)


def stick_breaking(xm, w_in):
    b_, s_, _ = xm.shape
    proj = (xm @ w_in).reshape(b_, s_, 3, B_HEADS, HEAD_DIM)
    q, k, v = proj[:, :, 0], proj[:, :, 1], proj[:, :, 2]
    scale = 1.0 / math.sqrt(HEAD_DIM)
    outs = []
    for blk in range(s_ // BLOCK):
        t0, t1 = blk * BLOCK, (blk + 1) * BLOCK
        z = jnp.einsum('bqhe,bkhe->bhqk', q[:, t0:t1], k[:, :t1]).astype(jnp.float32) * scale
        before = jnp.arange(t1)[None, :] < (t0 + jnp.arange(BLOCK))[:, None]
        sp = jnp.where(before, jax.nn.softplus(z), 0.0)
        tail = lax.cumsum(sp, axis=3, reverse=True) - sp
        a = jnp.where(before, jnp.exp(jax.nn.log_sigmoid(z) - tail), 0.0)
        outs.append(jnp.einsum('bhqk,bkhe->bqhe', a.astype(v.dtype), v[:, :t1]))
    return jnp.concatenate(outs, axis=1).reshape(b_, s_, MIX_WIDTH)


def peer(xn, w_query, sub_keys, expert_u, expert_v):
    b_, s_, d_ = xn.shape
    half = PEER_QDIM // 2

    def chunk(xc):
        c_ = xc.shape[0]
        q = (xc @ w_query).reshape(c_, PEER_HEADS, 2, half)
        s = jnp.einsum('chpe,hpne->chpn', q, sub_keys)
        sv, si = lax.top_k(s, PEER_TOPK)
        cand = (sv[:, :, 0, :, None] + sv[:, :, 1, None, :]).reshape(c_, PEER_HEADS, PEER_TOPK * PEER_TOPK)
        best, bi = lax.top_k(cand, PEER_TOPK)
        i1 = jnp.take_along_axis(si[:, :, 0], bi // PEER_TOPK, axis=-1)
        i2 = jnp.take_along_axis(si[:, :, 1], bi % PEER_TOPK, axis=-1)
        e = i1 * N_SUBKEYS + i2
        gate = jax.nn.softmax(best.astype(jnp.float32), axis=-1)
        hid = jax.nn.gelu(jnp.einsum('chkd,cd->chk', expert_u[e], xc).astype(jnp.float32), approximate=False)
        return jnp.einsum('chk,chkd->cd', (hid * gate).astype(xc.dtype), expert_v[e])

    out = lax.map(chunk, xn.reshape(-1, PEER_CHUNK, d_))
    return out.reshape(b_, s_, d_)


def setup_inputs(seed: int = 0) -> dict:
    key = jax.random.key(seed)
    ks = jax.random.split(key, 20)
    f32 = jnp.float32
    nrm = lambda k, shape, s: jax.random.normal(k, shape, f32) * s
    D = D_MODEL
    return {
        "x": nrm(ks[0], (BATCH, SEQ, D), 1.0),
        "c": nrm(ks[1], (BATCH, D), 1.0),
        "rel_bias": nrm(ks[2], (REL_BUCKETS, A_HEADS), 0.5),
        "norm1_g": 1.0 + nrm(ks[3], (DEPTH, D), 0.02),
        "norm2_g": 1.0 + nrm(ks[4], (DEPTH, D), 0.02),
        "w_ada": nrm(ks[5], (DEPTH, D, 6 * D), 0.5 * D ** -0.5),
        "b_ada": nrm(ks[6], (DEPTH, 6 * D), 0.01),
        "a_w_in": nrm(ks[7], (N_A_LAYERS, D, A_PROJ), D ** -0.5),
        "a_q_gain": 1.0 + nrm(ks[8], (N_A_LAYERS, N_GROUPS, HEAD_DIM), 0.02),
        "a_k_gain": 1.0 + nrm(ks[9], (N_A_LAYERS, N_GROUPS, HEAD_DIM), 0.02),
        "b_w_in": nrm(ks[10], (N_B_LAYERS, D, B_PROJ), D ** -0.5),
        "w_out": nrm(ks[11], (DEPTH, MIX_WIDTH, D), MIX_WIDTH ** -0.5),
        "peer_w_query": nrm(ks[12], (DEPTH, D, PEER_HEADS * PEER_QDIM), D ** -0.5),
        "peer_sub_keys": nrm(ks[13], (DEPTH, PEER_HEADS, 2, N_SUBKEYS, PEER_QDIM // 2), (PEER_QDIM // 2) ** -0.5),
        "peer_u": nrm(ks[14], (DEPTH, N_EXPERTS, D), D ** -0.5),
        "peer_v": nrm(ks[15], (DEPTH, N_EXPERTS, D), PEER_HEADS ** -0.5),
    }


def reference(x, c, rel_bias, norm1_g, norm2_g, w_ada, b_ada, a_w_in, a_q_gain, a_k_gain,
              b_w_in, w_out, peer_w_query, peer_sub_keys, peer_u, peer_v):
    cs = jax.nn.silu(c)
    h = x
    for i in range(DEPTH):
        mod = cs @ w_ada[i] + b_ada[i]
        sh1, sc1, g1, sh2, sc2, g2 = jnp.split(mod, 6, axis=-1)
        xm = modulate(rms_norm(h, norm1_g[i]), sh1, sc1)
        j = i // N_MIXERS
        if i % N_MIXERS == 0:
            y = dilated_attention(xm, a_w_in[j], a_q_gain[j], a_k_gain[j], rel_bias)
        else:
            y = stick_breaking(xm, b_w_in[j])
        h = h + g1[:, None, :] * (y @ w_out[i])
        xf = modulate(rms_norm(h, norm2_g[i]), sh2, sc2)
        h = h + g2[:, None, :] * peer(xf, peer_w_query[i], peer_sub_keys[i], peer_u[i], peer_v[i])
    return h
```

```python
import functools
import math

import jax
import jax.numpy as jnp
from jax import lax
from jax.experimental import pallas as pl
from jax.experimental.pallas import tpu as pltpu

F32 = jnp.float32
BF16 = jnp.bfloat16

HEAD_DIM = 128
BLOCK = 128
DIL_GROUPS = ((128, 1), (512, 4), (2048, 16))
REL_BUCKETS = 32
REL_MAX_DIST = 2048
PEER_HEADS = 8
N_SUBKEYS = 128
PEER_TOPK = 16
EPS = 1e-6
NEG_BIG = -1e30
SB_DEAD_DECAY = 104.0
INV_SQRT2 = 0.7071067811865476

VMEM_LIMIT = 56 * 1024 * 1024

NT_DIMS = (((1,), (1,)), ((), ()))
TN_DIMS = (((0,), (0,)), ((), ()))


def _params(*sem):
    return pltpu.CompilerParams(dimension_semantics=sem, vmem_limit_bytes=VMEM_LIMIT)


def _adaln_kernel(c_ref, w_ref, b_ref, o_ref):
    c = c_ref[...]
    cs = c / (1.0 + jnp.exp(-c))
    o_ref[0] = jnp.dot(cs, w_ref[0], preferred_element_type=F32,
                       precision=lax.Precision.HIGHEST) + b_ref[0]


def adaln(c, w_ada, b_ada, *, tn=1024):
    depth, d, n = w_ada.shape
    b = c.shape[0]
    rows = 8
    c_pad = jnp.pad(c, ((0, rows - b), (0, 0)))
    out = pl.pallas_call(
        _adaln_kernel,
        grid=(depth, n // tn),
        in_specs=[
            pl.BlockSpec((rows, d), lambda i, j: (0, 0)),
            pl.BlockSpec((1, d, tn), lambda i, j: (i, 0, j)),
            pl.BlockSpec((1, 1, tn), lambda i, j: (i, 0, j)),
        ],
        out_specs=pl.BlockSpec((1, rows, tn), lambda i, j: (i, 0, j)),
        out_shape=jax.ShapeDtypeStruct((depth, rows, n), F32),
        compiler_params=_params("arbitrary", "arbitrary"),
        name="adaln",
    )(c_pad, w_ada, b_ada.reshape(depth, 1, n))
    return out[:, :b]


def _norm_modulate(x, g, shift, scale):
    ms = jnp.mean(x * x, axis=-1, keepdims=True)
    y = x * lax.rsqrt(ms + EPS) * g
    return y * (1.0 + scale) + shift


def _in_proj_kernel(h_ref, g_ref, sh_ref, sc_ref, w_ref, gain_ref, o_ref, xm_ref, *, head_norm):
    j = pl.program_id(1)

    @pl.when(j == 0)
    def _():
        xm_ref[...] = _norm_modulate(h_ref[...], g_ref[...], sh_ref[0], sc_ref[0]).astype(BF16)

    acc = jnp.dot(xm_ref[...], w_ref[...], preferred_element_type=F32)
    tn = acc.shape[1]

    if not head_norm:
        o_ref[...] = acc.astype(o_ref.dtype)
        return

    @pl.when(j % 3 == 2)
    def _():
        o_ref[...] = acc.astype(o_ref.dtype)

    @pl.when(j % 3 != 2)
    def _():
        for hh in range(tn // HEAD_DIM):
            sl = slice(hh * HEAD_DIM, (hh + 1) * HEAD_DIM)
            a = acc[:, sl]
            ms = jnp.mean(a * a, axis=-1, keepdims=True)
            o_ref[:, sl] = (a * lax.rsqrt(ms + EPS) * gain_ref[0, :, sl]).astype(o_ref.dtype)


def in_proj(h, g, shift, scale, w, gain, *, seq, head_norm, tm=512, tn=2048):
    t, d = h.shape
    n = w.shape[1]
    per_batch = seq // tm
    return pl.pallas_call(
        functools.partial(_in_proj_kernel, head_norm=head_norm),
        grid=(t // tm, n // tn),
        in_specs=[
            pl.BlockSpec((tm, d), lambda i, j: (i, 0)),
            pl.BlockSpec((1, d), lambda i, j: (0, 0)),
            pl.BlockSpec((1, 1, d), lambda i, j: (i // per_batch, 0, 0)),
            pl.BlockSpec((1, 1, d), lambda i, j: (i // per_batch, 0, 0)),
            pl.BlockSpec((d, tn), lambda i, j: (0, j)),
            pl.BlockSpec((1, 1, tn), lambda i, j: (j, 0, 0)),
        ],
        out_specs=pl.BlockSpec((tm, tn), lambda i, j: (i, j)),
        out_shape=jax.ShapeDtypeStruct((t, n), BF16),
        scratch_shapes=[pltpu.VMEM((tm, d), BF16)],
        compiler_params=_params("arbitrary", "arbitrary"),
        name="in_proj_qknorm" if head_norm else "in_proj",
    )(h, g, shift, scale, w, gain)


def _dil_attn_kernel(q_ref, kp_ref, kc_ref, vp_ref, vc_ref, bias_ref, o_ref, lse_ref, *, n_heads):
    n = pl.program_id(2)
    lane = lax.broadcasted_iota(jnp.int32, (1, 2 * BLOCK), 1)
    pen = jnp.where(jnp.logical_and(lane < BLOCK, n == 0), NEG_BIG, 0.0).astype(F32)
    ones = jnp.ones((2 * BLOCK, HEAD_DIM), BF16)
    for hh in range(n_heads):
        sl = slice(hh * HEAD_DIM, (hh + 1) * HEAD_DIM)
        q = q_ref[0, :, sl]
        k = jnp.concatenate([kp_ref[0, :, sl], kc_ref[0, :, sl]], axis=0)
        v = jnp.concatenate([vp_ref[0, :, sl], vc_ref[0, :, sl]], axis=0)
        logits = lax.dot_general(q, k, NT_DIMS, preferred_element_type=F32)
        logits = logits + bias_ref[hh] + pen
        m = jnp.max(logits, axis=-1, keepdims=True)
        p = jnp.exp(logits - m).astype(BF16)
        v_ext = jnp.concatenate([v, ones], axis=1)
        pv = jnp.dot(p, v_ext, preferred_element_type=F32)
        denom = pv[:, HEAD_DIM:]
        o_ref[0, :, sl] = (pv[:, :HEAD_DIM] / denom).astype(o_ref.dtype)
        lse_ref[0, :, sl] = m + jnp.log(denom)


def dilated_group_attention(qkv, bias, *, group, dil, n_heads):
    b, s, c = qkv.shape
    width = n_heads * HEAD_DIM
    sub_len = s // dil
    nb = sub_len // BLOCK
    cols = c // width
    x = qkv.reshape(b, sub_len, dil * c)
    base = group * 3

    def spec(which, prev):
        def imap(bi, r, n):
            blk = jnp.maximum(n - 1, 0) if prev else n
            return (bi, blk, r * cols + base + which)
        return pl.BlockSpec((1, BLOCK, width), imap)

    out_spec = pl.BlockSpec((1, BLOCK, width), lambda bi, r, n: (bi, n, r))
    o, lse = pl.pallas_call(
        functools.partial(_dil_attn_kernel, n_heads=n_heads),
        grid=(b, dil, nb),
        in_specs=[spec(0, False), spec(1, True), spec(1, False), spec(2, True), spec(2, False),
                  pl.BlockSpec((n_heads, BLOCK, 2 * BLOCK), lambda bi, r, n: (0, 0, 0))],
        out_specs=[out_spec, out_spec],
        out_shape=[jax.ShapeDtypeStruct((b, sub_len, dil * width), BF16),
                   jax.ShapeDtypeStruct((b, sub_len, dil * width), F32)],
        compiler_params=_params("arbitrary", "arbitrary", "arbitrary"),
        name=f"dilated_attn_g{group}",
    )(x, x, x, x, x, bias)
    return o.reshape(b, s, width), lse.reshape(b, s, width)


def _rel_bucket(dist):
    exact = REL_BUCKETS // 2
    d_f = jnp.maximum(dist, exact).astype(F32)
    log_b = exact + (jnp.log(d_f / exact) / math.log(REL_MAX_DIST / exact)
                     * (REL_BUCKETS - exact)).astype(jnp.int32)
    return jnp.where(dist < exact, dist, jnp.minimum(log_b, REL_BUCKETS - 1))


def _band_bias(table_g, window, dil):
    w_steps = window // dil
    qi = jnp.arange(BLOCK)[:, None]
    ci = jnp.arange(2 * BLOCK)[None, :]
    delta = qi + BLOCK - ci
    in_band = (delta >= 0) & (delta <= w_steps)
    bucket = _rel_bucket(jnp.clip(delta, 0, w_steps) * dil)
    bias = jnp.moveaxis(table_g[bucket], -1, 0).astype(F32)
    return jnp.where(in_band[None], bias, NEG_BIG)


def _out_proj_merge_kernel(o0_ref, o1_ref, o2_ref, l0_ref, l1_ref, l2_ref, w_ref, h_ref, g_ref, out_ref):
    l0, l1, l2 = l0_ref[...], l1_ref[...], l2_ref[...]
    m = jnp.maximum(jnp.maximum(l0, l1), l2)
    e0, e1, e2 = jnp.exp(l0 - m), jnp.exp(l1 - m), jnp.exp(l2 - m)
    y = (e0 * o0_ref[...].astype(F32) + e1 * o1_ref[...].astype(F32)
         + e2 * o2_ref[...].astype(F32)) / (e0 + e1 + e2)
    proj = jnp.dot(y.astype(BF16), w_ref[...], preferred_element_type=F32)
    out_ref[...] = h_ref[...] + g_ref[0] * proj


def _out_proj_kernel(y_ref, w_ref, h_ref, g_ref, out_ref):
    proj = jnp.dot(y_ref[...], w_ref[...], preferred_element_type=F32)
    out_ref[...] = h_ref[...] + g_ref[0] * proj


def out_proj(ys, lses, w, h, gate, *, seq, tm=None):
    t, d = h.shape
    width = w.shape[0]
    merge = len(ys) > 1
    tm = tm or (256 if merge else 512)
    per_batch = seq // tm
    row = lambda cols: pl.BlockSpec((tm, cols), lambda i: (i, 0))
    return pl.pallas_call(
        _out_proj_merge_kernel if merge else _out_proj_kernel,
        grid=(t // tm,),
        in_specs=[row(width)] * (len(ys) + len(lses)) + [
            pl.BlockSpec((width, d), lambda i: (0, 0)),
            row(d),
            pl.BlockSpec((1, 1, d), lambda i: (i // per_batch, 0, 0)),
        ],
        out_specs=row(d),
        out_shape=jax.ShapeDtypeStruct((t, d), F32),
        compiler_params=_params("arbitrary"),
        name="out_proj_merge" if merge else "out_proj",
    )(*ys, *lses, w, h, gate)


def _split2(x):
    hi = x.astype(BF16)
    lo = (x - hi.astype(F32)).astype(BF16)
    return hi, lo


def _sb_kernel(q_ref, k_ref, v_ref, o_ref, *, scale):
    nq = q_ref.shape[1] // BLOCK
    row = lax.broadcasted_iota(jnp.int32, (BLOCK, BLOCK), 0)
    col = lax.broadcasted_iota(jnp.int32, (BLOCK, BLOCK), 1)
    before = col < row
    tri = jnp.where(row > col, 1.0, 0.0).astype(BF16)
    tri_ext = jnp.concatenate([tri, jnp.ones((BLOCK, BLOCK), BF16)], axis=1)

    def tail_and_sum(sp):
        hi, lo = _split2(sp)
        r = (jnp.dot(hi, tri_ext, preferred_element_type=F32)
             + jnp.dot(lo, tri_ext, preferred_element_type=F32))
        return r[:, :BLOCK], r[:, BLOCK:]

    def softplus(z):
        return jnp.maximum(z, 0.0) + jnp.log1p(jnp.exp(-jnp.abs(z)))

    def qblock(i, carry):
        q0 = pl.multiple_of(i * BLOCK, BLOCK)
        q = q_ref[0, pl.ds(q0, BLOCK), :]
        z = lax.dot_general(q, k_ref[0, pl.ds(q0, BLOCK), :], NT_DIMS,
                            preferred_element_type=F32) * scale
        sp = softplus(z)
        tail, rsum = tail_and_sum(jnp.where(before, sp, 0.0))
        a = jnp.where(before, jnp.exp(z - sp - tail), 0.0)
        acc = jnp.dot(a.astype(BF16), v_ref[0, pl.ds(q0, BLOCK), :], preferred_element_type=F32)

        def cond(c):
            j, decay, _ = c
            return jnp.logical_and(j >= 0, jnp.min(decay) < SB_DEAD_DECAY)

        def body(c):
            j, decay, acc = c
            k0 = pl.multiple_of(j * BLOCK, BLOCK)
            z = lax.dot_general(q, k_ref[0, pl.ds(k0, BLOCK), :], NT_DIMS,
                                preferred_element_type=F32) * scale
            sp = softplus(z)
            tail, rsum = tail_and_sum(sp)
            a = jnp.exp(z - sp - (decay + tail))
            acc = acc + jnp.dot(a.astype(BF16), v_ref[0, pl.ds(k0, BLOCK), :],
                                preferred_element_type=F32)
            return j - 1, decay + rsum, acc

        _, _, acc = lax.while_loop(cond, body, (i - 1, rsum, acc))
        o_ref[0, pl.ds(q0, BLOCK), :] = acc.astype(o_ref.dtype)
        return carry

    lax.fori_loop(0, nq, qblock, 0)


def stick_breaking_attention(qkv, *, n_heads):
    b, s, _ = qkv.shape

    def spec(which):
        return pl.BlockSpec((1, s, HEAD_DIM), lambda bi, h: (bi, 0, which * n_heads + h))

    return pl.pallas_call(
        functools.partial(_sb_kernel, scale=1.0 / math.sqrt(HEAD_DIM)),
        grid=(b, n_heads),
        in_specs=[spec(0), spec(1), spec(2)],
        out_specs=pl.BlockSpec((1, s, HEAD_DIM), lambda bi, h: (bi, 0, h)),
        out_shape=jax.ShapeDtypeStruct((b, s, n_heads * HEAD_DIM), BF16),
        compiler_params=_params("arbitrary", "arbitrary"),
        name="stick_breaking",
    )(qkv, qkv, qkv)


def _extract_top(s, dst_ref, count):
    def body(r, s):
        m = jnp.max(s, axis=0, keepdims=True)
        dst_ref[pl.ds(r, 1), :] = m
        return jnp.where(s == m, -jnp.inf, s)
    lax.fori_loop(0, count, body, s)


def _peer_select_kernel(h_ref, g_ref, sh_ref, sc_ref, wq_ref, keys_ref,
                        xf_ref, s1_ref, s2_ref, a1_ref, a2_ref, thr_ref,
                        qt_ref, sv1_ref, sv2_ref, best_ref):
    xf = _norm_modulate(h_ref[...], g_ref[...], sh_ref[0], sc_ref[0]).astype(BF16)
    xf_ref[...] = xf
    qt_ref[...] = lax.dot_general(wq_ref[...], xf, NT_DIMS, preferred_element_type=F32).astype(BF16)
    k = PEER_TOPK
    half = N_SUBKEYS

    def head(h, carry):
        r0 = pl.multiple_of(h * 2 * half, 2 * half)
        s1 = jnp.dot(keys_ref[h, 0], qt_ref[pl.ds(r0, half), :], preferred_element_type=F32)
        s2 = jnp.dot(keys_ref[h, 1], qt_ref[pl.ds(r0 + half, half), :], preferred_element_type=F32)
        _extract_top(s1, sv1_ref, k)
        _extract_top(s2, sv2_ref, k)
        sv1, sv2 = sv1_ref[...], sv2_ref[...]
        rows = [sv1[0:1] + sv2]
        rows += [sv1[a:a + 1] + sv2[0:k // 2] for a in range(1, k // 2)]
        rows += [sv1[k // 2:] + sv2[0:1]]
        _extract_top(jnp.concatenate(rows, axis=0), best_ref, k)
        best = best_ref[...]
        top = best[0:1]
        z = jnp.sum(jnp.exp(best - top), axis=0, keepdims=True)
        s1_ref[h] = s1
        s2_ref[h] = s2
        a1_ref[h] = jnp.exp(s1 - sv1[0:1]) / z
        a2_ref[h] = jnp.exp(s2 - sv2[0:1])
        thr_ref[h] = best[k - 1:k]
        return carry

    lax.fori_loop(0, PEER_HEADS, head, 0)


def peer_select(h, g, shift, scale, wq_t, keys, *, seq, tm=256):
    t, d = h.shape
    nq = wq_t.shape[0]
    per_batch = seq // tm
    tab = jax.ShapeDtypeStruct((PEER_HEADS, N_SUBKEYS, t), F32)
    tab_spec = pl.BlockSpec((PEER_HEADS, N_SUBKEYS, tm), lambda i: (0, 0, i))
    return pl.pallas_call(
        _peer_select_kernel,
        grid=(t // tm,),
        in_specs=[
            pl.BlockSpec((tm, d), lambda i: (i, 0)),
            pl.BlockSpec((1, d), lambda i: (0, 0)),
            pl.BlockSpec((1, 1, d), lambda i: (i // per_batch, 0, 0)),
            pl.BlockSpec((1, 1, d), lambda i: (i // per_batch, 0, 0)),
            pl.BlockSpec((nq, d), lambda i: (0, 0)),
            pl.BlockSpec(keys.shape, lambda i: (0, 0, 0, 0)),
        ],
        out_specs=[pl.BlockSpec((tm, d), lambda i: (i, 0)),
                   tab_spec, tab_spec, tab_spec, tab_spec,
                   pl.BlockSpec((PEER_HEADS, 1, tm), lambda i: (0, 0, i))],
        out_shape=[jax.ShapeDtypeStruct((t, d), BF16), tab, tab, tab, tab,
                   jax.ShapeDtypeStruct((PEER_HEADS, 1, t), F32)],
        scratch_shapes=[pltpu.VMEM((nq, tm), BF16),
                        pltpu.VMEM((PEER_TOPK, tm), F32),
                        pltpu.VMEM((PEER_TOPK, tm), F32),
                        pltpu.VMEM((PEER_TOPK, tm), F32)],
        compiler_params=_params("arbitrary"),
        name="peer_select",
    )(h, g, shift, scale, wq_t, keys)


def _peer_mix_kernel(xf_ref, u_ref, v_ref, s1_ref, s2_ref, a1_ref, a2_ref, thr_ref, h_ref, g_ref,
                     o_ref, acc_ref, hid_ref, p_ref):
    e = pl.program_id(1)
    eb = u_ref.shape[0]
    rows_per_step = eb // N_SUBKEYS

    @pl.when(e == 0)
    def _():
        acc_ref[...] = jnp.zeros_like(acc_ref)

    hid_ref[...] = lax.dot_general(u_ref[...], xf_ref[...], NT_DIMS, preferred_element_type=F32)
    for kk in range(rows_per_step):
        i1 = e * rows_per_step + kk
        gate = None
        for h in range(PEER_HEADS):
            score = s2_ref[h] + s1_ref[h, pl.ds(i1, 1), :]
            w = jnp.where(score >= thr_ref[h], a2_ref[h] * a1_ref[h, pl.ds(i1, 1), :], 0.0)
            gate = w if gate is None else gate + w
        sl = slice(kk * N_SUBKEYS, (kk + 1) * N_SUBKEYS)
        hid = hid_ref[sl, :]
        act = 0.5 * hid * (1.0 + lax.erf(hid * INV_SQRT2))
        p_ref[sl, :] = (act * gate).astype(BF16)
    acc_ref[...] += lax.dot_general(p_ref[...], v_ref[...], TN_DIMS, preferred_element_type=F32)

    @pl.when(e == pl.num_programs(1) - 1)
    def _():
        o_ref[...] = h_ref[...] + g_ref[0] * acc_ref[...]


def peer_mix(xf, u, v, s1, s2, a1, a2, thr, h, gate, *, seq, tc=512, eb=512):
    t, d = h.shape
    n_exp = u.shape[0]
    per_batch = seq // tc
    tab_spec = pl.BlockSpec((PEER_HEADS, N_SUBKEYS, tc), lambda c, e: (0, 0, c))
    return pl.pallas_call(
        _peer_mix_kernel,
        grid=(t // tc, n_exp // eb),
        in_specs=[
            pl.BlockSpec((tc, d), lambda c, e: (c, 0)),
            pl.BlockSpec((eb, d), lambda c, e: (e, 0)),
            pl.BlockSpec((eb, d), lambda c, e: (e, 0)),
            tab_spec, tab_spec, tab_spec, tab_spec,
            pl.BlockSpec((PEER_HEADS, 1, tc), lambda c, e: (0, 0, c)),
            pl.BlockSpec((tc, d), lambda c, e: (c, 0)),
            pl.BlockSpec((1, 1, d), lambda c, e: (c // per_batch, 0, 0)),
        ],
        out_specs=pl.BlockSpec((tc, d), lambda c, e: (c, 0)),
        out_shape=jax.ShapeDtypeStruct((t, d), F32),
        scratch_shapes=[pltpu.VMEM((tc, d), F32),
                        pltpu.VMEM((eb, tc), F32),
                        pltpu.VMEM((eb, tc), BF16)],
        compiler_params=_params("arbitrary", "arbitrary"),
        name="peer_mix",
    )(xf, u, v, s1, s2, a1, a2, thr, h, gate)


def kernel(x, c, rel_bias, norm1_g, norm2_g, w_ada, b_ada, a_w_in, a_q_gain, a_k_gain,
           b_w_in, w_out, peer_w_query, peer_sub_keys, peer_u, peer_v):
    b, s, d = x.shape
    depth = w_ada.shape[0]
    n_groups = len(DIL_GROUPS)
    heads = d // HEAD_DIM

    mod = adaln(c, w_ada, b_ada).reshape(depth, b, 6, 1, d)
    h = x.reshape(b * s, d)
    for i in range(depth):
        sh1, sc1, g1, sh2, sc2, g2 = (mod[i, :, k] for k in range(6))
        j = i // 2
        if i % 2 == 0:
            ones = jnp.ones((HEAD_DIM,), F32)
            gain = jnp.stack([jnp.tile(row, heads) for g in range(n_groups)
                              for row in (a_q_gain[j, g] / math.sqrt(HEAD_DIM), a_k_gain[j, g], ones)])
            qkv = in_proj(h, norm1_g[i][None], sh1, sc1, a_w_in[j].astype(BF16), gain[:, None, :],
                          seq=s, head_norm=True).reshape(b, s, -1)
            outs, lses = [], []
            for g, (window, dil) in enumerate(DIL_GROUPS):
                bias = _band_bias(rel_bias[:, g * heads:(g + 1) * heads], window, dil)
                o, lse = dilated_group_attention(qkv, bias, group=g, dil=dil, n_heads=heads)
                outs.append(o.reshape(b * s, d))
                lses.append(lse.reshape(b * s, d))
            h = out_proj(outs, lses, w_out[i].astype(BF16), h, g1, seq=s)
        else:
            gain = jnp.ones((b_w_in.shape[2] // 2048, 1, 2048), F32)
            qkv = in_proj(h, norm1_g[i][None], sh1, sc1, b_w_in[j].astype(BF16), gain,
                          seq=s, head_norm=False).reshape(b, s, -1)
            y = stick_breaking_attention(qkv, n_heads=heads)
            h = out_proj([y.reshape(b * s, d)], [], w_out[i].astype(BF16), h, g1, seq=s)
        xf, s1, s2, a1, a2, thr = peer_select(
            h, norm2_g[i][None], sh2, sc2, peer_w_query[i].T.astype(BF16),
            peer_sub_keys[i].astype(BF16), seq=s)
        h = peer_mix(xf, peer_u[i].astype(BF16), peer_v[i].astype(BF16), s1, s2, a1, a2, thr,
                     h, g2, seq=s)
    return h.reshape(b, s, d)
```

```python
import functools
import math

import jax
import jax.numpy as jnp
from jax import lax
from jax.experimental import pallas as pl
from jax.experimental.pallas import tpu as pltpu

F32 = jnp.float32
BF16 = jnp.bfloat16

HEAD_DIM = 128
BLOCK = 128
DIL_GROUPS = ((128, 1), (512, 4), (2048, 16))
REL_BUCKETS = 32
REL_MAX_DIST = 2048
PEER_HEADS = 8
N_SUBKEYS = 128
PEER_TOPK = 16
EPS = 1e-6
NEG_BIG = -1e30
SB_DEAD_DECAY = 104.0
SB_HEADS_PER_STEP = 4
INV_SQRT2 = 0.7071067811865476

VMEM_LIMIT = 56 * 1024 * 1024

NT_DIMS = (((1,), (1,)), ((), ()))
TN_DIMS = (((0,), (0,)), ((), ()))


def _params(*sem):
    return pltpu.CompilerParams(dimension_semantics=sem, vmem_limit_bytes=VMEM_LIMIT)


def _adaln_kernel(c_ref, w_ref, b_ref, o_ref):
    c = c_ref[...]
    cs = c / (1.0 + jnp.exp(-c))
    o_ref[0] = jnp.dot(cs, w_ref[0], preferred_element_type=F32,
                       precision=lax.Precision.HIGHEST) + b_ref[0]


def adaln(c, w_ada, b_ada, *, tn=1024):
    depth, d, n = w_ada.shape
    b = c.shape[0]
    rows = 8
    c_pad = jnp.pad(c, ((0, rows - b), (0, 0)))
    out = pl.pallas_call(
        _adaln_kernel,
        grid=(depth, n // tn),
        in_specs=[
            pl.BlockSpec((rows, d), lambda i, j: (0, 0)),
            pl.BlockSpec((1, d, tn), lambda i, j: (i, 0, j)),
            pl.BlockSpec((1, 1, tn), lambda i, j: (i, 0, j)),
        ],
        out_specs=pl.BlockSpec((1, rows, tn), lambda i, j: (i, 0, j)),
        out_shape=jax.ShapeDtypeStruct((depth, rows, n), F32),
        compiler_params=_params("arbitrary", "arbitrary"),
        name="adaln",
    )(c_pad, w_ada, b_ada.reshape(depth, 1, n))
    return out[:, :b]


def _norm_modulate(x, g, shift, scale):
    ms = jnp.mean(x * x, axis=-1, keepdims=True)
    y = x * lax.rsqrt(ms + EPS) * g
    return y * (1.0 + scale) + shift


def _in_proj_kernel(h_ref, g_ref, sh_ref, sc_ref, w_ref, o_ref, xm_ref):
    @pl.when(pl.program_id(1) == 0)
    def _():
        xm_ref[...] = _norm_modulate(h_ref[...], g_ref[...], sh_ref[0], sc_ref[0]).astype(BF16)

    o_ref[...] = jnp.dot(xm_ref[...], w_ref[...], preferred_element_type=F32).astype(o_ref.dtype)


def in_proj(h, g, shift, scale, w, *, seq, tm=512, tn=2048):
    t, d = h.shape
    n = w.shape[1]
    per_batch = seq // tm
    return pl.pallas_call(
        _in_proj_kernel,
        grid=(t // tm, n // tn),
        in_specs=[
            pl.BlockSpec((tm, d), lambda i, j: (i, 0)),
            pl.BlockSpec((1, d), lambda i, j: (0, 0)),
            pl.BlockSpec((1, 1, d), lambda i, j: (i // per_batch, 0, 0)),
            pl.BlockSpec((1, 1, d), lambda i, j: (i // per_batch, 0, 0)),
            pl.BlockSpec((d, tn), lambda i, j: (0, j)),
        ],
        out_specs=pl.BlockSpec((tm, tn), lambda i, j: (i, j)),
        out_shape=jax.ShapeDtypeStruct((t, n), BF16),
        scratch_shapes=[pltpu.VMEM((tm, d), BF16)],
        compiler_params=_params("arbitrary", "arbitrary"),
        name="in_proj",
    )(h, g, shift, scale, w)


def _in_proj_dilated_kernel(h_ref, g_ref, sh_ref, sc_ref, w_ref, gain_ref, o0_ref, o1_ref, o2_ref,
                            xm_ref, res_ref):
    j = pl.program_id(1)
    n_heads, tm, _ = res_ref.shape

    @pl.when(j == 0)
    def _():
        xm_ref[...] = _norm_modulate(h_ref[...], g_ref[...], sh_ref[0], sc_ref[0]).astype(BF16)

    acc = jnp.dot(xm_ref[...], w_ref[...], preferred_element_type=F32)

    @pl.when(j % 3 == 2)
    def _():
        for hh in range(n_heads):
            res_ref[hh] = acc[:, hh * HEAD_DIM:(hh + 1) * HEAD_DIM]

    @pl.when(j % 3 != 2)
    def _():
        for hh in range(n_heads):
            sl = slice(hh * HEAD_DIM, (hh + 1) * HEAD_DIM)
            a = acc[:, sl]
            ms = jnp.mean(a * a, axis=-1, keepdims=True)
            res_ref[hh] = a * lax.rsqrt(ms + EPS) * gain_ref[0, :, sl]

    for grp, ((_, dil), o_ref) in enumerate(zip(DIL_GROUPS, (o0_ref, o1_ref, o2_ref))):
        @pl.when(j // 3 == grp)
        def _(dil=dil, o_ref=o_ref):
            for hh in range(n_heads):
                sl = slice(hh * HEAD_DIM, (hh + 1) * HEAD_DIM)
                for r in range(dil):
                    rows = pl.ds(r, tm // dil, stride=dil) if dil > 1 else slice(None)
                    o_ref[0, r, :, sl] = res_ref[hh, rows, :].astype(BF16)


def in_proj_dilated(h, g, shift, scale, w, gain, *, seq, tm=512, tn=2048):
    t, d = h.shape
    n = w.shape[1]
    b = t // seq
    per_batch = seq // tm
    n_groups = len(DIL_GROUPS)
    assert n == n_groups * 3 * tn

    def out_spec(grp, dil):
        def imap(i, j):
            return (i // per_batch, 0, i % per_batch, jnp.clip(j - 3 * grp, 0, 2))
        return pl.BlockSpec((1, dil, tm // dil, tn), imap)

    return pl.pallas_call(
        _in_proj_dilated_kernel,
        grid=(t // tm, n // tn),
        in_specs=[
            pl.BlockSpec((tm, d), lambda i, j: (i, 0)),
            pl.BlockSpec((1, d), lambda i, j: (0, 0)),
            pl.BlockSpec((1, 1, d), lambda i, j: (i // per_batch, 0, 0)),
            pl.BlockSpec((1, 1, d), lambda i, j: (i // per_batch, 0, 0)),
            pl.BlockSpec((d, tn), lambda i, j: (0, j)),
            pl.BlockSpec((1, 1, tn), lambda i, j: (j, 0, 0)),
        ],
        out_specs=[out_spec(grp, dil) for grp, (_, dil) in enumerate(DIL_GROUPS)],
        out_shape=[jax.ShapeDtypeStruct((b, dil, seq // dil, 3 * tn), BF16) for _, dil in DIL_GROUPS],
        scratch_shapes=[pltpu.VMEM((tm, d), BF16), pltpu.VMEM((tn // HEAD_DIM, tm, HEAD_DIM), F32)],
        compiler_params=_params("arbitrary", "arbitrary"),
        name="in_proj_dilated",
    )(h, g, shift, scale, w, gain)


def _dil_attn_kernel(q_ref, kp_ref, kc_ref, vp_ref, vc_ref, bias_ref, o_ref, lse_ref, *, n_heads):
    n = pl.program_id(2)
    lane = lax.broadcasted_iota(jnp.int32, (1, 2 * BLOCK), 1)
    pen = jnp.where(jnp.logical_and(lane < BLOCK, n == 0), NEG_BIG, 0.0).astype(F32)
    ones = jnp.ones((2 * BLOCK, HEAD_DIM), BF16)
    for hh in range(n_heads):
        sl = slice(hh * HEAD_DIM, (hh + 1) * HEAD_DIM)
        q = q_ref[0, 0, :, sl]
        k = jnp.concatenate([kp_ref[0, 0, :, sl], kc_ref[0, 0, :, sl]], axis=0)
        v = jnp.concatenate([vp_ref[0, 0, :, sl], vc_ref[0, 0, :, sl]], axis=0)
        logits = lax.dot_general(q, k, NT_DIMS, preferred_element_type=F32)
        logits = logits + bias_ref[hh] + pen
        m = jnp.max(logits, axis=-1, keepdims=True)
        p = jnp.exp(logits - m).astype(BF16)
        v_ext = jnp.concatenate([v, ones], axis=1)
        pv = jnp.dot(p, v_ext, preferred_element_type=F32)
        denom = pv[:, HEAD_DIM:]
        o_ref[0, 0, :, sl] = (pv[:, :HEAD_DIM] / denom).astype(o_ref.dtype)
        lse_ref[0, 0, :, sl] = m + jnp.log(denom)


def dilated_group_attention(qkv, bias, *, n_heads):
    b, dil, sub_len, c = qkv.shape
    width = n_heads * HEAD_DIM
    nb = sub_len // BLOCK

    def spec(which, prev):
        def imap(bi, r, n):
            return (bi, r, jnp.maximum(n - 1, 0) if prev else n, which)
        return pl.BlockSpec((1, 1, BLOCK, width), imap)

    out_spec = pl.BlockSpec((1, 1, BLOCK, width), lambda bi, r, n: (bi, r, n, 0))
    return pl.pallas_call(
        functools.partial(_dil_attn_kernel, n_heads=n_heads),
        grid=(b, dil, nb),
        in_specs=[spec(0, False), spec(1, True), spec(1, False), spec(2, True), spec(2, False),
                  pl.BlockSpec((n_heads, BLOCK, 2 * BLOCK), lambda bi, r, n: (0, 0, 0))],
        out_specs=[out_spec, out_spec],
        out_shape=[jax.ShapeDtypeStruct((b, dil, sub_len, width), BF16),
                   jax.ShapeDtypeStruct((b, dil, sub_len, width), F32)],
        compiler_params=_params("arbitrary", "arbitrary", "arbitrary"),
        name=f"dilated_attn_d{dil}",
    )(qkv, qkv, qkv, qkv, qkv, bias)


def _rel_bucket(dist):
    exact = REL_BUCKETS // 2
    d_f = jnp.maximum(dist, exact).astype(F32)
    log_b = exact + (jnp.log(d_f / exact) / math.log(REL_MAX_DIST / exact)
                     * (REL_BUCKETS - exact)).astype(jnp.int32)
    return jnp.where(dist < exact, dist, jnp.minimum(log_b, REL_BUCKETS - 1))


def _band_bias(table_g, window, dil):
    w_steps = window // dil
    period = 3 * BLOCK
    per_delta = table_g[_rel_bucket(jnp.arange(w_steps + 1) * dil)].astype(F32)
    row = jnp.full((period, table_g.shape[1]), NEG_BIG, F32)
    row = row.at[BLOCK - w_steps:BLOCK + 1].set(per_delta[::-1])
    flat = jnp.tile(row.T, (1, BLOCK))[:, :BLOCK * (period - 1)]
    return flat.reshape(-1, BLOCK, period - 1)[:, :, :2 * BLOCK]


def _out_proj_merge_kernel(o0_ref, o1_ref, o2_ref, l0_ref, l1_ref, l2_ref, w_ref, h_ref, g_ref, out_ref,
                           os1_ref, os2_ref, ls1_ref, ls2_ref, y_ref):
    tm = out_ref.shape[0]
    n_heads = y_ref.shape[1] // HEAD_DIM

    def to_token_order(src_ref, dst_ref, sl):
        dil = src_ref.shape[1]
        for r in range(dil):
            dst_ref[pl.ds(r, tm // dil, stride=dil), :] = src_ref[0, r, :, sl].astype(F32)

    for hh in range(n_heads):
        sl = slice(hh * HEAD_DIM, (hh + 1) * HEAD_DIM)
        to_token_order(o1_ref, os1_ref, sl)
        to_token_order(o2_ref, os2_ref, sl)
        to_token_order(l1_ref, ls1_ref, sl)
        to_token_order(l2_ref, ls2_ref, sl)
        l0, l1, l2 = l0_ref[0, 0, :, sl], ls1_ref[...], ls2_ref[...]
        m = jnp.maximum(jnp.maximum(l0, l1), l2)
        e0, e1, e2 = jnp.exp(l0 - m), jnp.exp(l1 - m), jnp.exp(l2 - m)
        y = (e0 * o0_ref[0, 0, :, sl].astype(F32) + e1 * os1_ref[...] + e2 * os2_ref[...]) / (e0 + e1 + e2)
        y_ref[:, sl] = y.astype(BF16)
    proj = jnp.dot(y_ref[...], w_ref[...], preferred_element_type=F32)
    out_ref[...] = h_ref[...] + g_ref[0] * proj


def out_proj_merge(outs, lses, w, h, gate, *, seq, tm=256):
    t, d = h.shape
    width = w.shape[0]
    per_batch = seq // tm
    row = pl.BlockSpec((tm, d), lambda i: (i, 0))

    def grp_spec(a):
        dil = a.shape[1]
        return pl.BlockSpec((1, dil, tm // dil, width), lambda i: (i // per_batch, 0, i % per_batch, 0))

    return pl.pallas_call(
        _out_proj_merge_kernel,
        grid=(t // tm,),
        in_specs=[grp_spec(a) for a in (*outs, *lses)] + [
            pl.BlockSpec((width, d), lambda i: (0, 0)),
            row,
            pl.BlockSpec((1, 1, d), lambda i: (i // per_batch, 0, 0)),
        ],
        out_specs=row,
        out_shape=jax.ShapeDtypeStruct((t, d), F32),
        scratch_shapes=[pltpu.VMEM((tm, HEAD_DIM), F32)] * 4 + [pltpu.VMEM((tm, width), BF16)],
        compiler_params=_params("arbitrary"),
        name="out_proj_merge",
    )(*outs, *lses, w, h, gate)


def _out_proj_kernel(y_ref, w_ref, h_ref, g_ref, out_ref):
    proj = jnp.dot(y_ref[...], w_ref[...], preferred_element_type=F32)
    out_ref[...] = h_ref[...] + g_ref[0] * proj


def out_proj(y, w, h, gate, *, seq, tm=512):
    t, d = h.shape
    width = w.shape[0]
    per_batch = seq // tm
    row = lambda cols: pl.BlockSpec((tm, cols), lambda i: (i, 0))
    return pl.pallas_call(
        _out_proj_kernel,
        grid=(t // tm,),
        in_specs=[row(width), pl.BlockSpec((width, d), lambda i: (0, 0)), row(d),
                  pl.BlockSpec((1, 1, d), lambda i: (i // per_batch, 0, 0))],
        out_specs=row(d),
        out_shape=jax.ShapeDtypeStruct((t, d), F32),
        compiler_params=_params("arbitrary"),
        name="out_proj",
    )(y, w, h, gate)


def _split2(x):
    hi = x.astype(BF16)
    lo = (x - hi.astype(F32)).astype(BF16)
    return hi, lo


def _sb_kernel(q_ref, k_ref, v_ref, o_ref, acc_ref, decay_ref, *, scale):
    nq = q_ref.shape[1] // BLOCK
    n_heads = acc_ref.shape[0]
    row = lax.broadcasted_iota(jnp.int32, (BLOCK, BLOCK), 0)
    col = lax.broadcasted_iota(jnp.int32, (BLOCK, BLOCK), 1)
    before = col < row
    tri = jnp.where(row > col, 1.0, 0.0).astype(BF16)
    tri_ext = jnp.concatenate([tri, jnp.ones((BLOCK, BLOCK), BF16)], axis=1)

    def tails_and_sums(sps):
        parts = [t for sp in sps for t in _split2(sp)]
        r = jnp.dot(jnp.concatenate(parts, axis=0), tri_ext, preferred_element_type=F32)
        out = []
        for g in range(len(sps)):
            hi = r[(2 * g) * BLOCK:(2 * g + 1) * BLOCK]
            lo = r[(2 * g + 1) * BLOCK:(2 * g + 2) * BLOCK]
            both = hi + lo
            out.append((both[:, :BLOCK], both[:, BLOCK:]))
        return out

    def softplus(z):
        return jnp.maximum(z, 0.0) + jnp.log1p(jnp.exp(-jnp.abs(z)))

    def logits(q0, k0, g):
        cs = slice(g * HEAD_DIM, (g + 1) * HEAD_DIM)
        return lax.dot_general(q_ref[0, pl.ds(q0, BLOCK), cs], k_ref[0, pl.ds(k0, BLOCK), cs],
                               NT_DIMS, preferred_element_type=F32) * scale

    def values(k0, g):
        return v_ref[0, pl.ds(k0, BLOCK), g * HEAD_DIM:(g + 1) * HEAD_DIM]

    def qblock(i, carry):
        q0 = pl.multiple_of(i * BLOCK, BLOCK)
        zs = [logits(q0, q0, g) for g in range(n_heads)]
        sps = [softplus(z) for z in zs]
        ts = tails_and_sums([jnp.where(before, sp, 0.0) for sp in sps])
        live = None
        for g in range(n_heads):
            tail, rsum = ts[g]
            a = jnp.where(before, jnp.exp(zs[g] - sps[g] - tail), 0.0)
            acc_ref[g] = jnp.dot(a.astype(BF16), values(q0, g), preferred_element_type=F32)
            decay_ref[g] = rsum
            lo = jnp.min(rsum)
            live = lo if live is None else jnp.minimum(live, lo)

        def cond(c):
            j, live = c
            return jnp.logical_and(j >= 0, live < SB_DEAD_DECAY)

        def body(c):
            j, _ = c
            k0 = pl.multiple_of(j * BLOCK, BLOCK)
            zs = [logits(q0, k0, g) for g in range(n_heads)]
            sps = [softplus(z) for z in zs]
            ts = tails_and_sums(sps)
            live = None
            for g in range(n_heads):
                tail, rsum = ts[g]
                decay = decay_ref[g]
                a = jnp.exp(zs[g] - sps[g] - (decay + tail))
                acc_ref[g] += jnp.dot(a.astype(BF16), values(k0, g), preferred_element_type=F32)
                decay = decay + rsum
                decay_ref[g] = decay
                lo = jnp.min(decay)
                live = lo if live is None else jnp.minimum(live, lo)
            return j - 1, live

        lax.while_loop(cond, body, (i - 1, live))
        for g in range(n_heads):
            o_ref[0, pl.ds(q0, BLOCK), g * HEAD_DIM:(g + 1) * HEAD_DIM] = acc_ref[g].astype(o_ref.dtype)
        return carry

    lax.fori_loop(0, nq, qblock, 0)


def stick_breaking_attention(qkv, *, n_heads):
    b, s, _ = qkv.shape
    hps = SB_HEADS_PER_STEP
    steps = n_heads // hps
    width = hps * HEAD_DIM

    def spec(which):
        return pl.BlockSpec((1, s, width), lambda bi, h: (bi, 0, which * steps + h))

    return pl.pallas_call(
        functools.partial(_sb_kernel, scale=1.0 / math.sqrt(HEAD_DIM)),
        grid=(b, steps),
        in_specs=[spec(0), spec(1), spec(2)],
        out_specs=pl.BlockSpec((1, s, width), lambda bi, h: (bi, 0, h)),
        out_shape=jax.ShapeDtypeStruct((b, s, n_heads * HEAD_DIM), BF16),
        scratch_shapes=[pltpu.VMEM((hps, BLOCK, HEAD_DIM), F32),
                        pltpu.VMEM((hps, BLOCK, BLOCK), F32)],
        compiler_params=_params("arbitrary", "arbitrary"),
        name="stick_breaking",
    )(qkv, qkv, qkv)


UNRANKED = 127.0


def _extract_ranked(s, dst_ref, count):
    n = s.shape[0]
    idx = lax.broadcasted_iota(jnp.int32, s.shape, 0).astype(F32)

    def body(r, carry):
        s, rank = carry
        m = jnp.max(s, axis=0, keepdims=True)
        first = jnp.min(jnp.where(s == m, idx, float(n)), axis=0, keepdims=True)
        pick = idx == first
        dst_ref[pl.ds(r, 1), :] = m
        return jnp.where(pick, -jnp.inf, s), jnp.where(pick, r.astype(F32), rank)

    _, rank = lax.fori_loop(0, count, body, (s, jnp.full(s.shape, UNRANKED, F32)))
    return rank


def _peer_select_kernel(h_ref, g_ref, sh_ref, sc_ref, wq_ref, keys_ref,
                        xf_ref, r2_ref, a2_ref, lim_ref, a1_ref,
                        qt_ref, sv1_ref, sv2_ref, best_ref):
    xf = _norm_modulate(h_ref[...], g_ref[...], sh_ref[0], sc_ref[0]).astype(BF16)
    xf_ref[...] = xf
    qt_ref[...] = lax.dot_general(wq_ref[...], xf, NT_DIMS, preferred_element_type=F32).astype(BF16)
    k = PEER_TOPK
    half = N_SUBKEYS

    def head(h, carry):
        r0 = pl.multiple_of(h * 2 * half, 2 * half)
        s1 = jnp.dot(keys_ref[h, 0], qt_ref[pl.ds(r0, half), :], preferred_element_type=F32)
        s2 = jnp.dot(keys_ref[h, 1], qt_ref[pl.ds(r0 + half, half), :], preferred_element_type=F32)
        rank1 = _extract_ranked(s1, sv1_ref, k)
        rank2 = _extract_ranked(s2, sv2_ref, k)
        sv1, sv2 = sv1_ref[...], sv2_ref[...]
        groups = [sv1[0:1] + sv2]
        groups += [sv1[a:a + 1] + sv2[0:k // 2] for a in range(1, k // 2)]
        groups += [sv1[k // 2:] + sv2[0:1]]
        taken = _extract_ranked(jnp.concatenate(groups, axis=0), best_ref, k) < float(k)
        taken = jnp.where(taken, 1.0, 0.0)
        counts = [jnp.sum(taken[0:k], axis=0, keepdims=True)]
        for a in range(1, k // 2):
            lo = k + (a - 1) * (k // 2)
            counts.append(jnp.sum(taken[lo:lo + k // 2], axis=0, keepdims=True))
        tail0 = k + (k // 2 - 1) * (k // 2)
        counts += [taken[tail0 + a:tail0 + a + 1] for a in range(k // 2)]
        lim = jnp.zeros_like(s1)
        for a in range(k):
            lim = jnp.where(rank1 == float(a), counts[a], lim)
        best = best_ref[...]
        z = jnp.sum(jnp.exp(best - best[0:1]), axis=0, keepdims=True)
        r2_ref[h] = rank2.astype(BF16)
        a2_ref[h] = jnp.exp(s2 - sv2[0:1]).astype(BF16)
        lim_ref[h] = lim
        a1_ref[h] = jnp.exp(s1 - sv1[0:1]) / z
        return carry

    lax.fori_loop(0, PEER_HEADS, head, 0)


def peer_select(h, g, shift, scale, wq_t, keys, *, seq, tm=256):
    t, d = h.shape
    nq = wq_t.shape[0]
    per_batch = seq // tm
    tab_spec = pl.BlockSpec((PEER_HEADS, N_SUBKEYS, tm), lambda i: (0, 0, i))
    tab = lambda dt: jax.ShapeDtypeStruct((PEER_HEADS, N_SUBKEYS, t), dt)
    return pl.pallas_call(
        _peer_select_kernel,
        grid=(t // tm,),
        in_specs=[
            pl.BlockSpec((tm, d), lambda i: (i, 0)),
            pl.BlockSpec((1, d), lambda i: (0, 0)),
            pl.BlockSpec((1, 1, d), lambda i: (i // per_batch, 0, 0)),
            pl.BlockSpec((1, 1, d), lambda i: (i // per_batch, 0, 0)),
            pl.BlockSpec((nq, d), lambda i: (0, 0)),
            pl.BlockSpec(keys.shape, lambda i: (0, 0, 0, 0)),
        ],
        out_specs=[pl.BlockSpec((tm, d), lambda i: (i, 0)), tab_spec, tab_spec, tab_spec, tab_spec],
        out_shape=[jax.ShapeDtypeStruct((t, d), BF16), tab(BF16), tab(BF16), tab(F32), tab(F32)],
        scratch_shapes=[pltpu.VMEM((nq, tm), BF16),
                        pltpu.VMEM((PEER_TOPK, tm), F32),
                        pltpu.VMEM((PEER_TOPK, tm), F32),
                        pltpu.VMEM((PEER_TOPK, tm), F32)],
        compiler_params=_params("arbitrary"),
        name="peer_select",
    )(h, g, shift, scale, wq_t, keys)


def _peer_mix_kernel(xf_ref, u_ref, vt_ref, r2_ref, a2_ref, lim_ref, a1_ref, h_ref, g_ref,
                     o_ref, acc_ref, hid_a_ref, hid_b_ref, p_ref, *, n_blocks):
    g = pl.program_id(0)
    eb, tc = hid_a_ref.shape
    rows_per_step = eb // N_SUBKEYS
    mix_block = jnp.maximum(g - 1, 0) % n_blocks

    @pl.when(mix_block == 0)
    def _():
        acc_ref[...] = jnp.zeros_like(acc_ref)

    @pl.when(g == 0)
    def _():
        hid_b_ref[...] = jnp.zeros_like(hid_b_ref)

    def step(cur_ref, prev_ref):
        cur_ref[...] = lax.dot_general(u_ref[...], xf_ref[...], NT_DIMS, preferred_element_type=F32)
        for kk in range(rows_per_step):
            gate = None
            for h in range(PEER_HEADS):
                lim = jnp.broadcast_to(lim_ref[h, kk:kk + 1, :], (16, tc)).astype(BF16)
                a1 = jnp.broadcast_to(a1_ref[h, kk:kk + 1, :], (16, tc)).astype(BF16)
                lim = pltpu.repeat(lim, N_SUBKEYS // 16, axis=0)
                a1 = pltpu.repeat(a1, N_SUBKEYS // 16, axis=0)
                w = jnp.where(r2_ref[h] < lim, a2_ref[h] * a1, jnp.zeros((), BF16))
                gate = w if gate is None else gate + w
            sl = slice(kk * N_SUBKEYS, (kk + 1) * N_SUBKEYS)
            hid = prev_ref[sl, :]
            act = 0.5 * hid * (1.0 + lax.erf(hid * INV_SQRT2))
            p_ref[sl, :] = act.astype(BF16) * gate
        acc_ref[...] += jnp.dot(vt_ref[...], p_ref[...], preferred_element_type=F32)

    @pl.when(g % 2 == 0)
    def _():
        step(hid_a_ref, hid_b_ref)

    @pl.when(g % 2 == 1)
    def _():
        step(hid_b_ref, hid_a_ref)

    @pl.when(jnp.logical_and(mix_block == n_blocks - 1, g > 0))
    def _():
        o_ref[...] = h_ref[...] + g_ref[0] * acc_ref[...].T


def peer_mix(xf, u, vt, r2, a2, lim, a1, h, gate, *, seq, tc=512, eb=1024):
    t, d = h.shape
    n_blocks = u.shape[0] // eb
    n_items = (t // tc) * n_blocks
    per_batch = seq // tc
    rows = eb // N_SUBKEYS

    def hid_item(g):
        return jnp.minimum(g, n_items - 1)

    def mix_item(g):
        return jnp.maximum(g - 1, 0)

    tab_spec = pl.BlockSpec((PEER_HEADS, N_SUBKEYS, tc), lambda g: (0, 0, mix_item(g) // n_blocks))
    row_spec = pl.BlockSpec((PEER_HEADS, rows, tc),
                            lambda g: (0, mix_item(g) % n_blocks, mix_item(g) // n_blocks))
    chunk_spec = pl.BlockSpec((tc, d), lambda g: (mix_item(g) // n_blocks, 0))
    return pl.pallas_call(
        functools.partial(_peer_mix_kernel, n_blocks=n_blocks),
        grid=(n_items + 1,),
        in_specs=[
            pl.BlockSpec((tc, d), lambda g: (hid_item(g) // n_blocks, 0)),
            pl.BlockSpec((eb, d), lambda g: (hid_item(g) % n_blocks, 0)),
            pl.BlockSpec((d, eb), lambda g: (0, mix_item(g) % n_blocks)),
            tab_spec, tab_spec, row_spec, row_spec,
            chunk_spec,
            pl.BlockSpec((1, 1, d), lambda g: (mix_item(g) // n_blocks // per_batch, 0, 0)),
        ],
        out_specs=chunk_spec,
        out_shape=jax.ShapeDtypeStruct((t, d), F32),
        scratch_shapes=[pltpu.VMEM((d, tc), F32),
                        pltpu.VMEM((eb, tc), F32),
                        pltpu.VMEM((eb, tc), F32),
                        pltpu.VMEM((eb, tc), BF16)],
        compiler_params=_params("arbitrary"),
        name="peer_mix",
    )(xf, u, vt, r2, a2, lim, a1, h, gate)


def kernel(x, c, rel_bias, norm1_g, norm2_g, w_ada, b_ada, a_w_in, a_q_gain, a_k_gain,
           b_w_in, w_out, peer_w_query, peer_sub_keys, peer_u, peer_v):
    b, s, d = x.shape
    depth = w_ada.shape[0]
    n_groups = len(DIL_GROUPS)
    heads = d // HEAD_DIM

    mod = adaln(c, w_ada, b_ada).reshape(depth, b, 6, 1, d)
    h = x.reshape(b * s, d)
    for i in range(depth):
        sh1, sc1, g1, sh2, sc2, g2 = (mod[i, :, k] for k in range(6))
        j = i // 2
        if i % 2 == 0:
            ones = jnp.ones((HEAD_DIM,), F32)
            gain = jnp.stack([jnp.tile(row, heads) for g in range(n_groups)
                              for row in (a_q_gain[j, g] / math.sqrt(HEAD_DIM), a_k_gain[j, g], ones)])
            qkvs = in_proj_dilated(h, norm1_g[i][None], sh1, sc1, a_w_in[j].astype(BF16),
                                   gain[:, None, :], seq=s)
            outs, lses = [], []
            for g, (window, dil) in enumerate(DIL_GROUPS):
                bias = _band_bias(rel_bias[:, g * heads:(g + 1) * heads], window, dil)
                o, lse = dilated_group_attention(qkvs[g], bias, n_heads=heads)
                outs.append(o)
                lses.append(lse)
            h = out_proj_merge(outs, lses, w_out[i].astype(BF16), h, g1, seq=s)
        else:
            qkv = in_proj(h, norm1_g[i][None], sh1, sc1, b_w_in[j].astype(BF16), seq=s)
            y = stick_breaking_attention(qkv.reshape(b, s, -1), n_heads=heads)
            h = out_proj(y.reshape(b * s, d), w_out[i].astype(BF16), h, g1, seq=s)
        xf, r2, a2, lim, a1 = peer_select(
            h, norm2_g[i][None], sh2, sc2, peer_w_query[i].T.astype(BF16),
            peer_sub_keys[i].astype(BF16), seq=s)
        h = peer_mix(xf, peer_u[i].astype(BF16), peer_v[i].T.astype(BF16), r2, a2, lim, a1,
                     h, g2, seq=s)
    return h.reshape(b, s, d)
```

```python
import functools
import math

import jax
import jax.numpy as jnp
from jax import lax
from jax.experimental import pallas as pl
from jax.experimental.pallas import tpu as pltpu

F32 = jnp.float32
BF16 = jnp.bfloat16

HEAD_DIM = 128
BLOCK = 128
DIL_GROUPS = ((128, 1), (512, 4), (2048, 16))
REL_BUCKETS = 32
REL_MAX_DIST = 2048
PEER_HEADS = 8
N_SUBKEYS = 128
PEER_TOPK = 16
EPS = 1e-6
NEG_BIG = -1e30
SB_DEAD_DECAY = 104.0
SB_HEADS_PER_STEP = 8
SB_Q_CHUNK = 1024
INV_SQRT2 = 0.7071067811865476

VMEM_LIMIT = 56 * 1024 * 1024

NT_DIMS = (((1,), (1,)), ((), ()))


def _params(*sem):
    return pltpu.CompilerParams(dimension_semantics=sem, vmem_limit_bytes=VMEM_LIMIT)


def _adaln_kernel(c_ref, w_ref, b_ref, o_ref):
    c = c_ref[...]
    cs = c / (1.0 + jnp.exp(-c))
    o_ref[0] = jnp.dot(cs, w_ref[0], preferred_element_type=F32,
                       precision=lax.Precision.HIGHEST) + b_ref[0]


def adaln(c, w_ada, b_ada, *, tn=1024):
    depth, d, n = w_ada.shape
    b = c.shape[0]
    rows = 8
    c_pad = jnp.pad(c, ((0, rows - b), (0, 0)))
    out = pl.pallas_call(
        _adaln_kernel,
        grid=(depth, n // tn),
        in_specs=[
            pl.BlockSpec((rows, d), lambda i, j: (0, 0)),
            pl.BlockSpec((1, d, tn), lambda i, j: (i, 0, j)),
            pl.BlockSpec((1, 1, tn), lambda i, j: (i, 0, j)),
        ],
        out_specs=pl.BlockSpec((1, rows, tn), lambda i, j: (i, 0, j)),
        out_shape=jax.ShapeDtypeStruct((depth, rows, n), F32),
        compiler_params=_params("arbitrary", "arbitrary"),
        name="adaln",
    )(c_pad, w_ada, b_ada.reshape(depth, 1, n))
    return out[:, :b]


def _norm_modulate(x, g, shift, scale):
    ms = jnp.mean(x * x, axis=-1, keepdims=True)
    y = x * lax.rsqrt(ms + EPS) * g
    return y * (1.0 + scale) + shift


def _in_proj_kernel(h_ref, g_ref, sh_ref, sc_ref, w_ref, o_ref, xm_ref):
    @pl.when(pl.program_id(1) == 0)
    def _():
        xm_ref[...] = _norm_modulate(h_ref[...], g_ref[...], sh_ref[0], sc_ref[0]).astype(BF16)

    o_ref[...] = jnp.dot(xm_ref[...], w_ref[...], preferred_element_type=F32).astype(o_ref.dtype)


def in_proj(h, g, shift, scale, w, *, layer, seq, tm=512, tn=2048):
    t, d = h.shape
    n = w.shape[2]
    per_batch = seq // tm
    return pl.pallas_call(
        _in_proj_kernel,
        grid=(t // tm, n // tn),
        in_specs=[
            pl.BlockSpec((tm, d), lambda i, j: (i, 0)),
            pl.BlockSpec((1, d), lambda i, j: (0, 0)),
            pl.BlockSpec((1, 1, d), lambda i, j: (i // per_batch, 0, 0)),
            pl.BlockSpec((1, 1, d), lambda i, j: (i // per_batch, 0, 0)),
            pl.BlockSpec((None, d, tn), lambda i, j: (layer, 0, j)),
        ],
        out_specs=pl.BlockSpec((tm, tn), lambda i, j: (i, j)),
        out_shape=jax.ShapeDtypeStruct((t, n), BF16),
        scratch_shapes=[pltpu.VMEM((tm, d), BF16)],
        compiler_params=_params("arbitrary", "arbitrary"),
        name="in_proj",
    )(h, g, shift, scale, w)


def _in_proj_dilated_kernel(h_ref, g_ref, sh_ref, sc_ref, w_ref, gain_ref, o0_ref, o1_ref, o2_ref,
                            xm0_ref, xm1_ref, xm2_ref, slab_ref):
    j = pl.program_id(1)
    tm, d = xm0_ref.shape
    xm_refs = (xm0_ref, xm1_ref, xm2_ref)

    @pl.when(j == 0)
    def _():
        xm = _norm_modulate(h_ref[...], g_ref[...], sh_ref[0], sc_ref[0])
        for sb in range(d // HEAD_DIM):
            slab_ref[sb] = xm[:, sb * HEAD_DIM:(sb + 1) * HEAD_DIM]
        for (_, dil), xm_ref in zip(DIL_GROUPS, xm_refs):
            if dil == 1:
                xm_ref[...] = xm.astype(BF16)
                continue
            rows = tm // dil
            for sb in range(d // HEAD_DIM):
                for r in range(dil):
                    xm_ref[r * rows:(r + 1) * rows, sb * HEAD_DIM:(sb + 1) * HEAD_DIM] = (
                        slab_ref[sb, pl.ds(r, rows, stride=dil), :].astype(BF16))

    def project(xm_ref, o_ref, dil):
        rows = tm // dil
        acc = jnp.dot(xm_ref[...], w_ref[...], preferred_element_type=F32)

        @pl.when(j % 3 == 2)
        def _():
            for r in range(dil):
                o_ref[0, r] = acc[r * rows:(r + 1) * rows].astype(BF16)

        @pl.when(j % 3 != 2)
        def _():
            for hh in range(acc.shape[1] // HEAD_DIM):
                sl = slice(hh * HEAD_DIM, (hh + 1) * HEAD_DIM)
                a = acc[:, sl]
                ms = jnp.mean(a * a, axis=-1, keepdims=True)
                y = (a * lax.rsqrt(ms + EPS) * gain_ref[0, :, sl]).astype(BF16)
                for r in range(dil):
                    o_ref[0, r, :, sl] = y[r * rows:(r + 1) * rows]

    for grp, ((_, dil), xm_ref, o_ref) in enumerate(zip(DIL_GROUPS, xm_refs, (o0_ref, o1_ref, o2_ref))):
        @pl.when(j // 3 == grp)
        def _(dil=dil, xm_ref=xm_ref, o_ref=o_ref):
            project(xm_ref, o_ref, dil)


def in_proj_dilated(h, g, shift, scale, w, gain, *, layer, seq, tm=512, tn=2048):
    t, d = h.shape
    n = w.shape[2]
    b = t // seq
    per_batch = seq // tm
    n_groups = len(DIL_GROUPS)
    assert n == n_groups * 3 * tn

    def out_spec(grp, dil):
        def imap(i, j):
            return (i // per_batch, 0, i % per_batch, jnp.clip(j - 3 * grp, 0, 2))
        return pl.BlockSpec((1, dil, tm // dil, tn), imap)

    return pl.pallas_call(
        _in_proj_dilated_kernel,
        grid=(t // tm, n // tn),
        in_specs=[
            pl.BlockSpec((tm, d), lambda i, j: (i, 0)),
            pl.BlockSpec((1, d), lambda i, j: (0, 0)),
            pl.BlockSpec((1, 1, d), lambda i, j: (i // per_batch, 0, 0)),
            pl.BlockSpec((1, 1, d), lambda i, j: (i // per_batch, 0, 0)),
            pl.BlockSpec((None, d, tn), lambda i, j: (layer, 0, j)),
            pl.BlockSpec((1, 1, tn), lambda i, j: (j, 0, 0)),
        ],
        out_specs=[out_spec(grp, dil) for grp, (_, dil) in enumerate(DIL_GROUPS)],
        out_shape=[jax.ShapeDtypeStruct((b, dil, seq // dil, 3 * tn), BF16) for _, dil in DIL_GROUPS],
        scratch_shapes=[pltpu.VMEM((tm, d), BF16)] * 3 + [pltpu.VMEM((d // HEAD_DIM, tm, HEAD_DIM), F32)],
        compiler_params=_params("arbitrary", "arbitrary"),
        name="in_proj_dilated",
    )(h, g, shift, scale, w, gain)


def _dil_attn_kernel(q_ref, kp_ref, kc_ref, vp_ref, vc_ref, bias_ref, o_ref, lse_ref, *, n_heads):
    n = pl.program_id(2)
    lane = lax.broadcasted_iota(jnp.int32, (1, 2 * BLOCK), 1)
    pen = jnp.where(jnp.logical_and(lane < BLOCK, n == 0), NEG_BIG, 0.0).astype(F32)
    ones = jnp.ones((2 * BLOCK, HEAD_DIM), BF16)
    for hh in range(n_heads):
        sl = slice(hh * HEAD_DIM, (hh + 1) * HEAD_DIM)
        q = q_ref[0, 0, :, sl]
        k = jnp.concatenate([kp_ref[0, 0, :, sl], kc_ref[0, 0, :, sl]], axis=0)
        v = jnp.concatenate([vp_ref[0, 0, :, sl], vc_ref[0, 0, :, sl]], axis=0)
        logits = lax.dot_general(q, k, NT_DIMS, preferred_element_type=F32)
        logits = logits + bias_ref[hh] + pen
        m = jnp.max(logits, axis=-1, keepdims=True)
        p = jnp.exp(logits - m).astype(BF16)
        v_ext = jnp.concatenate([v, ones], axis=1)
        pv = jnp.dot(p, v_ext, preferred_element_type=F32)
        denom = pv[:, HEAD_DIM:]
        o_ref[0, 0, :, sl] = (pv[:, :HEAD_DIM] / denom).astype(o_ref.dtype)
        lse_ref[0, 0, :, sl] = m + jnp.log(denom)


def dilated_group_attention(qkv, bias, *, n_heads):
    b, dil, sub_len, c = qkv.shape
    width = n_heads * HEAD_DIM
    nb = sub_len // BLOCK

    def spec(which, prev):
        def imap(bi, r, n):
            return (bi, r, jnp.maximum(n - 1, 0) if prev else n, which)
        return pl.BlockSpec((1, 1, BLOCK, width), imap)

    out_spec = pl.BlockSpec((1, 1, BLOCK, width), lambda bi, r, n: (bi, r, n, 0))
    return pl.pallas_call(
        functools.partial(_dil_attn_kernel, n_heads=n_heads),
        grid=(b, dil, nb),
        in_specs=[spec(0, False), spec(1, True), spec(1, False), spec(2, True), spec(2, False),
                  pl.BlockSpec((n_heads, BLOCK, 2 * BLOCK), lambda bi, r, n: (0, 0, 0))],
        out_specs=[out_spec, out_spec],
        out_shape=[jax.ShapeDtypeStruct((b, dil, sub_len, width), BF16),
                   jax.ShapeDtypeStruct((b, dil, sub_len, width), F32)],
        compiler_params=_params("arbitrary", "arbitrary", "arbitrary"),
        name=f"dilated_attn_d{dil}",
    )(qkv, qkv, qkv, qkv, qkv, bias)


def _rel_bucket(dist):
    exact = REL_BUCKETS // 2
    d_f = jnp.maximum(dist, exact).astype(F32)
    log_b = exact + (jnp.log(d_f / exact) / math.log(REL_MAX_DIST / exact)
                     * (REL_BUCKETS - exact)).astype(jnp.int32)
    return jnp.where(dist < exact, dist, jnp.minimum(log_b, REL_BUCKETS - 1))


def _band_bias(table_g, window, dil):
    w_steps = window // dil
    period = 3 * BLOCK
    per_delta = table_g[_rel_bucket(jnp.arange(w_steps + 1) * dil)].astype(F32)
    row = jnp.full((period, table_g.shape[1]), NEG_BIG, F32)
    row = row.at[BLOCK - w_steps:BLOCK + 1].set(per_delta[::-1])
    flat = jnp.tile(row.T, (1, BLOCK))[:, :BLOCK * (period - 1)]
    return flat.reshape(-1, BLOCK, period - 1)[:, :, :2 * BLOCK]


def _out_proj_merge_kernel(o0_ref, o1_ref, o2_ref, l0_ref, l1_ref, l2_ref, w_ref, h_ref, g_ref, out_ref,
                           os1_ref, os2_ref, ls1_ref, ls2_ref, y_ref):
    tm = out_ref.shape[0]
    n_heads = y_ref.shape[1] // HEAD_DIM

    def to_token_order(src_ref, dst_ref, sl):
        dil = src_ref.shape[1]
        for r in range(dil):
            dst_ref[pl.ds(r, tm // dil, stride=dil), :] = src_ref[0, r, :, sl].astype(F32)

    for hh in range(n_heads):
        sl = slice(hh * HEAD_DIM, (hh + 1) * HEAD_DIM)
        to_token_order(o1_ref, os1_ref, sl)
        to_token_order(o2_ref, os2_ref, sl)
        to_token_order(l1_ref, ls1_ref, sl)
        to_token_order(l2_ref, ls2_ref, sl)
        l0, l1, l2 = l0_ref[0, 0, :, sl], ls1_ref[...], ls2_ref[...]
        m = jnp.maximum(jnp.maximum(l0, l1), l2)
        e0, e1, e2 = jnp.exp(l0 - m), jnp.exp(l1 - m), jnp.exp(l2 - m)
        y = (e0 * o0_ref[0, 0, :, sl].astype(F32) + e1 * os1_ref[...] + e2 * os2_ref[...]) / (e0 + e1 + e2)
        y_ref[:, sl] = y.astype(BF16)
    proj = jnp.dot(y_ref[...], w_ref[...], preferred_element_type=F32)
    out_ref[...] = h_ref[...] + g_ref[0] * proj


def out_proj_merge(outs, lses, w, h, gate, *, layer, seq, tm=256):
    t, d = h.shape
    width = w.shape[1]
    per_batch = seq // tm
    row = pl.BlockSpec((tm, d), lambda i: (i, 0))

    def grp_spec(a):
        dil = a.shape[1]
        return pl.BlockSpec((1, dil, tm // dil, width), lambda i: (i // per_batch, 0, i % per_batch, 0))

    return pl.pallas_call(
        _out_proj_merge_kernel,
        grid=(t // tm,),
        in_specs=[grp_spec(a) for a in (*outs, *lses)] + [
            pl.BlockSpec((None, width, d), lambda i: (layer, 0, 0)),
            row,
            pl.BlockSpec((1, 1, d), lambda i: (i // per_batch, 0, 0)),
        ],
        out_specs=row,
        out_shape=jax.ShapeDtypeStruct((t, d), F32),
        scratch_shapes=[pltpu.VMEM((tm, HEAD_DIM), F32)] * 4 + [pltpu.VMEM((tm, width), BF16)],
        compiler_params=_params("arbitrary"),
        name="out_proj_merge",
    )(*outs, *lses, w, h, gate)


def _out_proj_kernel(y_ref, w_ref, h_ref, g_ref, out_ref):
    proj = jnp.dot(y_ref[...], w_ref[...], preferred_element_type=F32)
    out_ref[...] = h_ref[...] + g_ref[0] * proj


def out_proj(y, w, h, gate, *, layer, seq, tm=512):
    t, d = h.shape
    width = w.shape[1]
    per_batch = seq // tm
    row = lambda cols: pl.BlockSpec((tm, cols), lambda i: (i, 0))
    return pl.pallas_call(
        _out_proj_kernel,
        grid=(t // tm,),
        in_specs=[row(width), pl.BlockSpec((None, width, d), lambda i: (layer, 0, 0)), row(d),
                  pl.BlockSpec((1, 1, d), lambda i: (i // per_batch, 0, 0))],
        out_specs=row(d),
        out_shape=jax.ShapeDtypeStruct((t, d), F32),
        compiler_params=_params("arbitrary"),
        name="out_proj",
    )(y, w, h, gate)


def _split2(x):
    hi = x.astype(BF16)
    lo = (x - hi.astype(F32)).astype(BF16)
    return hi, lo


def _sb_kernel(q_ref, k_ref, v_ref, o_ref, acc_ref, decay_ref, *, scale):
    nq = q_ref.shape[1] // BLOCK
    q_base = pl.program_id(2) * q_ref.shape[1]
    n_heads = acc_ref.shape[0]
    row = lax.broadcasted_iota(jnp.int32, (BLOCK, BLOCK), 0)
    col = lax.broadcasted_iota(jnp.int32, (BLOCK, BLOCK), 1)
    before = col < row
    tri = jnp.where(row > col, 1.0, 0.0).astype(BF16)
    tri_ext = jnp.concatenate([tri, jnp.ones((BLOCK, BLOCK), BF16)], axis=1)

    def tails_and_sums(sps):
        parts = [t for sp in sps for t in _split2(sp)]
        r = jnp.dot(jnp.concatenate(parts, axis=0), tri_ext, preferred_element_type=F32)
        out = []
        for g in range(len(sps)):
            hi = r[(2 * g) * BLOCK:(2 * g + 1) * BLOCK]
            lo = r[(2 * g + 1) * BLOCK:(2 * g + 2) * BLOCK]
            both = hi + lo
            out.append((both[:, :BLOCK], both[:, BLOCK:]))
        return out

    def softplus(z):
        return jnp.maximum(z, 0.0) + jnp.log1p(jnp.exp(-jnp.abs(z)))

    def logits(q0, k0, g):
        cs = slice(g * HEAD_DIM, (g + 1) * HEAD_DIM)
        return lax.dot_general(q_ref[0, pl.ds(q0, BLOCK), cs], k_ref[0, pl.ds(k0, BLOCK), cs],
                               NT_DIMS, preferred_element_type=F32) * scale

    def values(k0, g):
        return v_ref[0, pl.ds(k0, BLOCK), g * HEAD_DIM:(g + 1) * HEAD_DIM]

    def qblock(i, carry):
        q0 = pl.multiple_of(i * BLOCK, BLOCK)
        d0 = pl.multiple_of(q_base + q0, BLOCK)
        zs = [logits(q0, d0, g) for g in range(n_heads)]
        sps = [softplus(z) for z in zs]
        ts = tails_and_sums([jnp.where(before, sp, 0.0) for sp in sps])
        live = None
        for g in range(n_heads):
            tail, rsum = ts[g]
            a = jnp.where(before, jnp.exp(zs[g] - sps[g] - tail), 0.0)
            acc_ref[g] = jnp.dot(a.astype(BF16), values(d0, g), preferred_element_type=F32)
            decay_ref[g] = rsum
            lo = jnp.min(rsum)
            live = lo if live is None else jnp.minimum(live, lo)

        def cond(c):
            j, live = c
            return jnp.logical_and(j >= 0, live < SB_DEAD_DECAY)

        def body(c):
            j, _ = c
            k0 = pl.multiple_of(j * BLOCK, BLOCK)
            zs = [logits(q0, k0, g) for g in range(n_heads)]
            sps = [softplus(z) for z in zs]
            ts = tails_and_sums(sps)
            live = None
            for g in range(n_heads):
                tail, rsum = ts[g]
                decay = decay_ref[g]
                a = jnp.exp(zs[g] - sps[g] - (decay + tail))
                acc_ref[g] += jnp.dot(a.astype(BF16), values(k0, g), preferred_element_type=F32)
                decay = decay + rsum
                decay_ref[g] = decay
                lo = jnp.min(decay)
                live = lo if live is None else jnp.minimum(live, lo)
            return j - 1, live

        lax.while_loop(cond, body, (d0 // BLOCK - 1, live))
        for g in range(n_heads):
            o_ref[0, pl.ds(q0, BLOCK), g * HEAD_DIM:(g + 1) * HEAD_DIM] = acc_ref[g].astype(o_ref.dtype)
        return carry

    lax.fori_loop(0, nq, qblock, 0)


def stick_breaking_attention(qkv, *, n_heads):
    b, s, _ = qkv.shape
    hps = min(SB_HEADS_PER_STEP, n_heads)
    steps = n_heads // hps
    width = hps * HEAD_DIM
    qc = min(SB_Q_CHUNK, s)

    def spec(which):
        return pl.BlockSpec((1, s, width), lambda bi, h, c: (bi, 0, which * steps + h))

    return pl.pallas_call(
        functools.partial(_sb_kernel, scale=1.0 / math.sqrt(HEAD_DIM)),
        grid=(b, steps, s // qc),
        in_specs=[pl.BlockSpec((1, qc, width), lambda bi, h, c: (bi, c, h)), spec(1), spec(2)],
        out_specs=pl.BlockSpec((1, qc, width), lambda bi, h, c: (bi, c, h)),
        out_shape=jax.ShapeDtypeStruct((b, s, n_heads * HEAD_DIM), BF16),
        scratch_shapes=[pltpu.VMEM((hps, BLOCK, HEAD_DIM), F32),
                        pltpu.VMEM((hps, BLOCK, BLOCK), F32)],
        compiler_params=_params("arbitrary", "arbitrary", "arbitrary"),
        name="stick_breaking",
    )(qkv, qkv, qkv)


UNRANKED = 127.0


def _as_f32(r):
    return jnp.asarray(r).astype(F32)


def _extract_ranked(s, dst_ref, count):
    n = s.shape[0]
    idx = lax.broadcasted_iota(jnp.int32, s.shape, 0).astype(F32)

    def body(r, carry):
        s, rank = carry
        m = jnp.max(s, axis=0, keepdims=True)
        first = jnp.min(jnp.where(s == m, idx, float(n)), axis=0, keepdims=True)
        pick = idx == first
        dst_ref[pl.ds(r, 1), :] = m
        return jnp.where(pick, -jnp.inf, s), jnp.where(pick, _as_f32(r), rank)

    _, rank = lax.fori_loop(0, count, body, (s, jnp.full(s.shape, UNRANKED, F32)))
    return rank


def _candidates(sv1, sv2):
    k = PEER_TOPK
    groups = [sv1[0:1] + sv2]
    groups += [sv1[a:a + 1] + sv2[0:k // 2] for a in range(1, k // 2)]
    groups += [sv1[k // 2:] + sv2[0:1]]
    spans = [(0, k)] + [(k + (a - 1) * (k // 2), k // 2) for a in range(1, k // 2)]
    tail0 = k + (k // 2 - 1) * (k // 2)
    spans += [(tail0 + a, 1) for a in range(k // 2)]
    return jnp.concatenate(groups, axis=0), spans


def _staircase_counts(taken, spans):
    return [jnp.sum(taken[lo:lo + n], axis=0, keepdims=True) for lo, n in spans]


def _peer_select_kernel(h_ref, g_ref, sh_ref, sc_ref, wq_ref, keys_ref,
                        xf_ref, r2_ref, a2_ref, lim_ref, a1_ref,
                        qt_ref, sv1_all, sv2_all, best_all):
    xf = _norm_modulate(h_ref[...], g_ref[...], sh_ref[0], sc_ref[0]).astype(BF16)
    xf_ref[...] = xf
    qt_ref[...] = lax.dot_general(wq_ref[...], xf, NT_DIMS, preferred_element_type=F32).astype(BF16)
    k = PEER_TOPK
    half = N_SUBKEYS
    kf = float(k)

    def unit(h, lt):
        ls = slice(lt * 128, (lt + 1) * 128)
        sv1_ref, sv2_ref, best_ref = sv1_all.at[lt], sv2_all.at[lt], best_all.at[lt]
        r0 = pl.multiple_of(h * 2 * half, 2 * half)
        s1 = jnp.dot(keys_ref[h, 0], qt_ref[pl.ds(r0, half), ls], preferred_element_type=F32)
        s2 = jnp.dot(keys_ref[h, 1], qt_ref[pl.ds(r0 + half, half), ls], preferred_element_type=F32)

        def emit(rank2, lim):
            sv1, sv2, best = sv1_ref[...], sv2_ref[...], best_ref[...]
            z = jnp.sum(jnp.exp(best - best[0:1]), axis=0, keepdims=True)
            r2_ref[h, :, ls] = rank2.astype(BF16)
            a2_ref[h, :, ls] = jnp.exp(s2 - sv2[0:1]).astype(BF16)
            lim_ref[h, :, ls] = lim
            a1_ref[h, :, ls] = jnp.exp(s1 - sv1[0:1]) / z

        def round12(r, c):
            w1, w2, rank2 = c
            m1 = jnp.max(w1, axis=0, keepdims=True)
            m2 = jnp.max(w2, axis=0, keepdims=True)
            sv1_ref[pl.ds(r, 1), :] = m1
            sv2_ref[pl.ds(r, 1), :] = m2
            p2 = w2 == m2
            return (jnp.where(w1 == m1, -jnp.inf, w1), jnp.where(p2, -jnp.inf, w2),
                    jnp.where(p2, _as_f32(r), rank2))

        _, _, rank2 = lax.fori_loop(0, k, round12, (s1, s2, jnp.full(s2.shape, UNRANKED, F32)))
        sv1, sv2 = sv1_ref[...], sv2_ref[...]
        cand, spans = _candidates(sv1, sv2)

        def round_c(r, w):
            m = jnp.max(w, axis=0, keepdims=True)
            best_ref[pl.ds(r, 1), :] = m
            return jnp.where(w == m, -jnp.inf, w)

        lax.fori_loop(0, k, round_c, cand)
        taken = jnp.where(cand >= best_ref[k - 1:k, :], 1.0, 0.0)
        counts = _staircase_counts(taken, spans)
        lim = jnp.zeros_like(s1)
        for a in range(k):
            lim = jnp.where(s1 == sv1[a:a + 1], counts[a], lim)
        emit(rank2, lim)

        n1 = jnp.sum(jnp.where(s1 >= sv1[k - 1:k], 1.0, 0.0), axis=0, keepdims=True)
        n2 = jnp.sum(jnp.where(rank2 < kf, 1.0, 0.0), axis=0, keepdims=True)
        nc = jnp.sum(taken, axis=0, keepdims=True)
        tied = jnp.max(jnp.maximum(jnp.maximum(n1, n2), nc)) > kf

        @pl.when(tied)
        def _():
            rank1 = _extract_ranked(s1, sv1_ref, k)
            rank2 = _extract_ranked(s2, sv2_ref, k)
            cand, spans = _candidates(sv1_ref[...], sv2_ref[...])
            taken = jnp.where(_extract_ranked(cand, best_ref, k) < kf, 1.0, 0.0)
            counts = _staircase_counts(taken, spans)
            lim = jnp.zeros_like(s1)
            for a in range(k):
                lim = jnp.where(rank1 == float(a), counts[a], lim)
            emit(rank2, lim)

    def head(h, carry):
        for lt in range(sv1_all.shape[0]):
            unit(h, lt)
        return carry

    lax.fori_loop(0, PEER_HEADS, head, 0)


def peer_select(h, g, shift, scale, wq_t, keys, *, layer, seq, tm=256):
    t, d = h.shape
    nq = wq_t.shape[1]
    per_batch = seq // tm
    tab_spec = pl.BlockSpec((PEER_HEADS, N_SUBKEYS, tm), lambda i: (0, 0, i))
    tab = lambda dt: jax.ShapeDtypeStruct((PEER_HEADS, N_SUBKEYS, t), dt)
    return pl.pallas_call(
        _peer_select_kernel,
        grid=(t // tm,),
        in_specs=[
            pl.BlockSpec((tm, d), lambda i: (i, 0)),
            pl.BlockSpec((1, d), lambda i: (0, 0)),
            pl.BlockSpec((1, 1, d), lambda i: (i // per_batch, 0, 0)),
            pl.BlockSpec((1, 1, d), lambda i: (i // per_batch, 0, 0)),
            pl.BlockSpec((None, nq, d), lambda i: (layer, 0, 0)),
            pl.BlockSpec((None,) + keys.shape[1:], lambda i: (layer, 0, 0, 0, 0)),
        ],
        out_specs=[pl.BlockSpec((tm, d), lambda i: (i, 0)), tab_spec, tab_spec, tab_spec, tab_spec],
        out_shape=[jax.ShapeDtypeStruct((t, d), BF16), tab(BF16), tab(BF16), tab(F32), tab(F32)],
        scratch_shapes=[pltpu.VMEM((nq, tm), BF16),
                        pltpu.VMEM((tm // 128, PEER_TOPK, 128), F32),
                        pltpu.VMEM((tm // 128, PEER_TOPK, 128), F32),
                        pltpu.VMEM((tm // 128, PEER_TOPK, 128), F32)],
        compiler_params=_params("arbitrary"),
        name="peer_select",
    )(h, g, shift, scale, wq_t, keys)


def _peer_mix_kernel(xf_ref, u_ref, vt_ref, r2_ref, a2_ref, lim_ref, a1_ref, h_ref, g_ref,
                     o_ref, acc_ref, hid_a_ref, hid_b_ref, p_ref, *, n_blocks):
    g = pl.program_id(0)
    eb, tc = hid_a_ref.shape
    rows_per_step = eb // N_SUBKEYS
    mix_block = jnp.maximum(g - 1, 0) % n_blocks

    @pl.when(mix_block == 0)
    def _():
        acc_ref[...] = jnp.zeros_like(acc_ref)

    @pl.when(g == 0)
    def _():
        hid_b_ref[...] = jnp.zeros_like(hid_b_ref)

    def step(cur_ref, prev_ref):
        cur_ref[...] = lax.dot_general(u_ref[...], xf_ref[...], NT_DIMS, preferred_element_type=F32)
        for kk in range(rows_per_step):
            gate = None
            for h in range(PEER_HEADS):
                lim = jnp.broadcast_to(lim_ref[h, kk:kk + 1, :], (16, tc)).astype(BF16)
                a1 = jnp.broadcast_to(a1_ref[h, kk:kk + 1, :], (16, tc)).astype(BF16)
                lim = pltpu.repeat(lim, N_SUBKEYS // 16, axis=0)
                a1 = pltpu.repeat(a1, N_SUBKEYS // 16, axis=0)
                w = jnp.where(r2_ref[h] < lim, a2_ref[h] * a1, jnp.zeros((), BF16))
                gate = w if gate is None else gate + w
            sl = slice(kk * N_SUBKEYS, (kk + 1) * N_SUBKEYS)
            hid = prev_ref[sl, :]
            act = 0.5 * hid * (1.0 + lax.erf(hid * INV_SQRT2))
            p_ref[sl, :] = act.astype(BF16) * gate
        acc_ref[...] += jnp.dot(vt_ref[...], p_ref[...], preferred_element_type=F32)

    @pl.when(g % 2 == 0)
    def _():
        step(hid_a_ref, hid_b_ref)

    @pl.when(g % 2 == 1)
    def _():
        step(hid_b_ref, hid_a_ref)

    @pl.when(jnp.logical_and(mix_block == n_blocks - 1, g > 0))
    def _():
        o_ref[...] = h_ref[...] + g_ref[0] * acc_ref[...].T


def peer_mix(xf, u, vt, r2, a2, lim, a1, h, gate, *, layer, seq, tc=512, eb=1024):
    t, d = h.shape
    n_blocks = u.shape[1] // eb
    n_items = (t // tc) * n_blocks
    per_batch = seq // tc
    rows = eb // N_SUBKEYS

    def hid_item(g):
        return jnp.minimum(g, n_items - 1)

    def mix_item(g):
        return jnp.maximum(g - 1, 0)

    tab_spec = pl.BlockSpec((PEER_HEADS, N_SUBKEYS, tc), lambda g: (0, 0, mix_item(g) // n_blocks))
    row_spec = pl.BlockSpec((PEER_HEADS, rows, tc),
                            lambda g: (0, mix_item(g) % n_blocks, mix_item(g) // n_blocks))
    chunk_spec = pl.BlockSpec((tc, d), lambda g: (mix_item(g) // n_blocks, 0))
    return pl.pallas_call(
        functools.partial(_peer_mix_kernel, n_blocks=n_blocks),
        grid=(n_items + 1,),
        in_specs=[
            pl.BlockSpec((tc, d), lambda g: (hid_item(g) // n_blocks, 0)),
            pl.BlockSpec((None, eb, d), lambda g: (layer, hid_item(g) % n_blocks, 0)),
            pl.BlockSpec((None, d, eb), lambda g: (layer, 0, mix_item(g) % n_blocks)),
            tab_spec, tab_spec, row_spec, row_spec,
            chunk_spec,
            pl.BlockSpec((1, 1, d), lambda g: (mix_item(g) // n_blocks // per_batch, 0, 0)),
        ],
        out_specs=chunk_spec,
        out_shape=jax.ShapeDtypeStruct((t, d), F32),
        scratch_shapes=[pltpu.VMEM((d, tc), F32),
                        pltpu.VMEM((eb, tc), F32),
                        pltpu.VMEM((eb, tc), F32),
                        pltpu.VMEM((eb, tc), BF16)],
        compiler_params=_params("arbitrary"),
        name="peer_mix",
    )(xf, u, vt, r2, a2, lim, a1, h, gate)


def kernel(x, c, rel_bias, norm1_g, norm2_g, w_ada, b_ada, a_w_in, a_q_gain, a_k_gain,
           b_w_in, w_out, peer_w_query, peer_sub_keys, peer_u, peer_v):
    b, s, d = x.shape
    depth = w_ada.shape[0]
    n_groups = len(DIL_GROUPS)
    heads = d // HEAD_DIM

    mod = adaln(c, w_ada, b_ada).reshape(depth, b, 6, 1, d)
    a_w, b_w, w_o = a_w_in.astype(BF16), b_w_in.astype(BF16), w_out.astype(BF16)
    wq_t = jnp.swapaxes(peer_w_query, 1, 2).astype(BF16)
    keys = peer_sub_keys.astype(BF16)
    u, vt = peer_u.astype(BF16), jnp.swapaxes(peer_v, 1, 2).astype(BF16)
    h = x.reshape(b * s, d)
    for i in range(depth):
        sh1, sc1, g1, sh2, sc2, g2 = (mod[i, :, k] for k in range(6))
        j = i // 2
        if i % 2 == 0:
            ones = jnp.ones((HEAD_DIM,), F32)
            gain = jnp.stack([jnp.tile(row, heads) for g in range(n_groups)
                              for row in (a_q_gain[j, g] / math.sqrt(HEAD_DIM), a_k_gain[j, g], ones)])
            qkvs = in_proj_dilated(h, norm1_g[i][None], sh1, sc1, a_w, gain[:, None, :], layer=j, seq=s)
            outs, lses = [], []
            for g, (window, dil) in enumerate(DIL_GROUPS):
                bias = _band_bias(rel_bias[:, g * heads:(g + 1) * heads], window, dil)
                o, lse = dilated_group_attention(qkvs[g], bias, n_heads=heads)
                outs.append(o)
                lses.append(lse)
            h = out_proj_merge(outs, lses, w_o, h, g1, layer=i, seq=s)
        else:
            qkv = in_proj(h, norm1_g[i][None], sh1, sc1, b_w, layer=j, seq=s)
            y = stick_breaking_attention(qkv.reshape(b, s, -1), n_heads=heads)
            h = out_proj(y.reshape(b * s, d), w_o, h, g1, layer=i, seq=s)
        xf, r2, a2, lim, a1 = peer_select(h, norm2_g[i][None], sh2, sc2, wq_t, keys, layer=i, seq=s)
        h = peer_mix(xf, u, vt, r2, a2, lim, a1, h, g2, layer=i, seq=s)
    return h.reshape(b, s, d)
```

```python
import functools
import math

import jax
import jax.numpy as jnp
from jax import lax
from jax.experimental import pallas as pl
from jax.experimental.pallas import tpu as pltpu

F32 = jnp.float32
BF16 = jnp.bfloat16

HEAD_DIM = 128
BLOCK = 128
DIL_GROUPS = ((128, 1), (512, 4), (2048, 16))
REL_BUCKETS = 32
REL_MAX_DIST = 2048
PEER_HEADS = 8
N_SUBKEYS = 128
PEER_TOPK = 16
EPS = 1e-6
NEG_BIG = -1e30
SB_DEAD_DECAY = 104.0
SB_HEADS_PER_STEP = 8
SB_Q_CHUNK = 1024
INV_SQRT2 = 0.7071067811865476

VMEM_LIMIT = 56 * 1024 * 1024

NT_DIMS = (((1,), (1,)), ((), ()))


def _params(*sem):
    return pltpu.CompilerParams(dimension_semantics=sem, vmem_limit_bytes=VMEM_LIMIT)


def _adaln_kernel(c_ref, w_lo_ref, w_hi_ref, b_ref, o_ref):
    c = c_ref[...]
    cs = c / (1.0 + jnp.exp(-c))
    half = w_lo_ref.shape[1]
    dot = functools.partial(jnp.dot, preferred_element_type=F32, precision=lax.Precision.HIGHEST)
    o_ref[0] = dot(cs[:, :half], w_lo_ref[0]) + dot(cs[:, half:], w_hi_ref[0]) + b_ref[0]


def adaln(c, w_ada, b_ada, *, tn=1024):
    depth, d, n = w_ada.shape
    b = c.shape[0]
    rows = 8
    c_pad = jnp.pad(c, ((0, rows - b), (0, 0)))
    out = pl.pallas_call(
        _adaln_kernel,
        grid=(depth, n // tn),
        in_specs=[
            pl.BlockSpec((rows, d), lambda i, j: (0, 0)),
            pl.BlockSpec((1, d // 2, tn), lambda i, j: (i, 0, j)),
            pl.BlockSpec((1, d // 2, tn), lambda i, j: (i, 1, j)),
            pl.BlockSpec((1, 1, tn), lambda i, j: (i, 0, j)),
        ],
        out_specs=pl.BlockSpec((1, rows, tn), lambda i, j: (i, 0, j)),
        out_shape=jax.ShapeDtypeStruct((depth, rows, n), F32),
        compiler_params=_params("arbitrary", "arbitrary"),
        name="adaln",
    )(c_pad, w_ada, w_ada, b_ada.reshape(depth, 1, n))
    return out[:, :b]


def _norm_modulate(x, g, shift, scale):
    ms = jnp.mean(x * x, axis=-1, keepdims=True)
    y = x * lax.rsqrt(ms + EPS) * g
    return y * (1.0 + scale) + shift


def _in_proj_kernel(h_ref, g_ref, sh_ref, sc_ref, w_ref, o_ref, xm_ref):
    @pl.when(pl.program_id(1) == 0)
    def _():
        xm_ref[...] = _norm_modulate(h_ref[...], g_ref[...], sh_ref[0], sc_ref[0]).astype(BF16)

    o_ref[...] = jnp.dot(xm_ref[...], w_ref[...], preferred_element_type=F32).astype(o_ref.dtype)


def in_proj(h, g, shift, scale, w, *, layer, seq, tm=512, tn=2048):
    t, d = h.shape
    n = w.shape[2]
    per_batch = seq // tm
    return pl.pallas_call(
        _in_proj_kernel,
        grid=(t // tm, n // tn),
        in_specs=[
            pl.BlockSpec((tm, d), lambda i, j: (i, 0)),
            pl.BlockSpec((1, d), lambda i, j: (0, 0)),
            pl.BlockSpec((1, 1, d), lambda i, j: (i // per_batch, 0, 0)),
            pl.BlockSpec((1, 1, d), lambda i, j: (i // per_batch, 0, 0)),
            pl.BlockSpec((None, d, tn), lambda i, j: (layer, 0, j)),
        ],
        out_specs=pl.BlockSpec((tm, tn), lambda i, j: (i, j)),
        out_shape=jax.ShapeDtypeStruct((t, n), BF16),
        scratch_shapes=[pltpu.VMEM((tm, d), BF16)],
        compiler_params=_params("arbitrary", "arbitrary"),
        name="in_proj",
    )(h, g, shift, scale, w)


def _in_proj_dilated_kernel(h_ref, g_ref, sh_ref, sc_ref, w_ref, gain_ref, o0_ref, o1_ref, o2_ref,
                            xm0_ref, xm1_ref, xm2_ref, slab_ref):
    j = pl.program_id(1)
    tm, d = xm0_ref.shape
    xm_refs = (xm0_ref, xm1_ref, xm2_ref)

    @pl.when(j == 0)
    def _():
        xm = _norm_modulate(h_ref[...], g_ref[...], sh_ref[0], sc_ref[0])
        for sb in range(d // HEAD_DIM):
            slab_ref[sb] = xm[:, sb * HEAD_DIM:(sb + 1) * HEAD_DIM]
        for (_, dil), xm_ref in zip(DIL_GROUPS, xm_refs):
            if dil == 1:
                xm_ref[...] = xm.astype(BF16)
                continue
            rows = tm // dil
            for sb in range(d // HEAD_DIM):
                for r in range(dil):
                    xm_ref[r * rows:(r + 1) * rows, sb * HEAD_DIM:(sb + 1) * HEAD_DIM] = (
                        slab_ref[sb, pl.ds(r, rows, stride=dil), :].astype(BF16))

    def project(xm_ref, o_ref, dil):
        rows = tm // dil
        acc = jnp.dot(xm_ref[...], w_ref[...], preferred_element_type=F32)

        @pl.when(j % 3 == 2)
        def _():
            for r in range(dil):
                o_ref[0, r] = acc[r * rows:(r + 1) * rows].astype(BF16)

        @pl.when(j % 3 != 2)
        def _():
            for hh in range(acc.shape[1] // HEAD_DIM):
                sl = slice(hh * HEAD_DIM, (hh + 1) * HEAD_DIM)
                a = acc[:, sl]
                ms = jnp.mean(a * a, axis=-1, keepdims=True)
                y = (a * lax.rsqrt(ms + EPS) * gain_ref[0, :, sl]).astype(BF16)
                for r in range(dil):
                    o_ref[0, r, :, sl] = y[r * rows:(r + 1) * rows]

    for grp, ((_, dil), xm_ref, o_ref) in enumerate(zip(DIL_GROUPS, xm_refs, (o0_ref, o1_ref, o2_ref))):
        @pl.when(j // 3 == grp)
        def _(dil=dil, xm_ref=xm_ref, o_ref=o_ref):
            project(xm_ref, o_ref, dil)


def in_proj_dilated(h, g, shift, scale, w, gain, *, layer, seq, tm=512, tn=2048):
    t, d = h.shape
    n = w.shape[2]
    b = t // seq
    per_batch = seq // tm
    n_groups = len(DIL_GROUPS)
    assert n == n_groups * 3 * tn

    def out_spec(grp, dil):
        def imap(i, j):
            return (i // per_batch, 0, i % per_batch, jnp.clip(j - 3 * grp, 0, 2))
        return pl.BlockSpec((1, dil, tm // dil, tn), imap)

    return pl.pallas_call(
        _in_proj_dilated_kernel,
        grid=(t // tm, n // tn),
        in_specs=[
            pl.BlockSpec((tm, d), lambda i, j: (i, 0)),
            pl.BlockSpec((1, d), lambda i, j: (0, 0)),
            pl.BlockSpec((1, 1, d), lambda i, j: (i // per_batch, 0, 0)),
            pl.BlockSpec((1, 1, d), lambda i, j: (i // per_batch, 0, 0)),
            pl.BlockSpec((None, d, tn), lambda i, j: (layer, 0, j)),
            pl.BlockSpec((1, 1, tn), lambda i, j: (j, 0, 0)),
        ],
        out_specs=[out_spec(grp, dil) for grp, (_, dil) in enumerate(DIL_GROUPS)],
        out_shape=[jax.ShapeDtypeStruct((b, dil, seq // dil, 3 * tn), BF16) for _, dil in DIL_GROUPS],
        scratch_shapes=[pltpu.VMEM((tm, d), BF16)] * 3 + [pltpu.VMEM((d // HEAD_DIM, tm, HEAD_DIM), F32)],
        compiler_params=_params("arbitrary", "arbitrary"),
        name="in_proj_dilated",
    )(h, g, shift, scale, w, gain)


def _dil_attn_kernel(q_ref, kp_ref, kc_ref, vp_ref, vc_ref, bias_ref, o_ref, lse_ref, *, n_heads):
    n = pl.program_id(2)
    n_sub = q_ref.shape[2] // BLOCK
    lane = lax.broadcasted_iota(jnp.int32, (1, 2 * BLOCK), 1)
    pen = jnp.where(jnp.logical_and(lane < BLOCK, n == 0), NEG_BIG, 0.0).astype(F32)
    ones = jnp.ones((2 * BLOCK, HEAD_DIM), BF16)
    lse_lanes = BLOCK // n_heads
    lane_head = lax.broadcasted_iota(jnp.int32, (1, BLOCK), 1) // lse_lanes
    for sb in range(n_sub):
        rows = slice(sb * BLOCK, (sb + 1) * BLOCK)
        prev = slice((sb - 1) * BLOCK, sb * BLOCK)
        lse_pack = jnp.zeros((BLOCK, BLOCK), F32)
        for hh in range(n_heads):
            sl = slice(hh * HEAD_DIM, (hh + 1) * HEAD_DIM)
            q = q_ref[0, 0, rows, sl]
            k_prev = kp_ref[0, 0, :, sl] if sb == 0 else kc_ref[0, 0, prev, sl]
            v_prev = vp_ref[0, 0, :, sl] if sb == 0 else vc_ref[0, 0, prev, sl]
            k = jnp.concatenate([k_prev, kc_ref[0, 0, rows, sl]], axis=0)
            v = jnp.concatenate([v_prev, vc_ref[0, 0, rows, sl]], axis=0)
            logits = lax.dot_general(q, k, NT_DIMS, preferred_element_type=F32)
            logits = logits + bias_ref[hh]
            if sb == 0:
                logits = logits + pen
            m = jnp.max(logits, axis=-1, keepdims=True)
            p = jnp.exp(logits - m).astype(BF16)
            v_ext = jnp.concatenate([v, ones], axis=1)
            pv = jnp.dot(p, v_ext, preferred_element_type=F32)
            denom = pv[:, HEAD_DIM:]
            o_ref[0, 0, rows, sl] = (pv[:, :HEAD_DIM] / denom).astype(o_ref.dtype)
            lse_pack = jnp.where(lane_head == hh, m + jnp.log(denom), lse_pack)
        lse_ref[0, 0, rows, :] = lse_pack


def dilated_group_attention(qkv, bias, *, n_heads, q_blocks=2):
    b, dil, sub_len, c = qkv.shape
    assert BLOCK % n_heads == 0
    width = n_heads * HEAD_DIM
    q_blocks = min(q_blocks, sub_len // BLOCK)
    rows = q_blocks * BLOCK
    steps = sub_len // rows

    def cur(which):
        return pl.BlockSpec((1, 1, rows, width), lambda bi, r, n: (bi, r, n, which))

    def prev(which):
        return pl.BlockSpec((1, 1, BLOCK, width),
                            lambda bi, r, n: (bi, r, jnp.maximum(q_blocks * n - 1, 0), which))

    return pl.pallas_call(
        functools.partial(_dil_attn_kernel, n_heads=n_heads),
        grid=(b, dil, steps),
        in_specs=[cur(0), prev(1), cur(1), prev(2), cur(2),
                  pl.BlockSpec((n_heads, BLOCK, 2 * BLOCK), lambda bi, r, n: (0, 0, 0))],
        out_specs=[pl.BlockSpec((1, 1, rows, width), lambda bi, r, n: (bi, r, n, 0)),
                   pl.BlockSpec((1, 1, rows, BLOCK), lambda bi, r, n: (bi, r, n, 0))],
        out_shape=[jax.ShapeDtypeStruct((b, dil, sub_len, width), BF16),
                   jax.ShapeDtypeStruct((b, dil, sub_len, BLOCK), F32)],
        compiler_params=_params("arbitrary", "arbitrary", "arbitrary"),
        name=f"dilated_attn_d{dil}",
    )(qkv, qkv, qkv, qkv, qkv, bias)


def _rel_bucket(dist):
    exact = REL_BUCKETS // 2
    d_f = jnp.maximum(dist, exact).astype(F32)
    log_b = exact + (jnp.log(d_f / exact) / math.log(REL_MAX_DIST / exact)
                     * (REL_BUCKETS - exact)).astype(jnp.int32)
    return jnp.where(dist < exact, dist, jnp.minimum(log_b, REL_BUCKETS - 1))


def _band_bias(table_g, window, dil):
    w_steps = window // dil
    period = 3 * BLOCK
    per_delta = table_g[_rel_bucket(jnp.arange(w_steps + 1) * dil)].astype(F32)
    row = jnp.full((period, table_g.shape[1]), NEG_BIG, F32)
    row = row.at[BLOCK - w_steps:BLOCK + 1].set(per_delta[::-1])
    flat = jnp.tile(row.T, (1, BLOCK))[:, :BLOCK * (period - 1)]
    return flat.reshape(-1, BLOCK, period - 1)[:, :, :2 * BLOCK]


def _out_proj_merge_kernel(o0_ref, o1_ref, o2_ref, l0_ref, l1_ref, l2_ref, w_ref, h_ref, g_ref, out_ref,
                           os1_ref, os2_ref, ls1_ref, ls2_ref, y_ref):
    tm = out_ref.shape[0]
    n_heads = y_ref.shape[1] // HEAD_DIM

    def to_token_order(src_ref, dst_ref, sl):
        dil = src_ref.shape[1]
        for r in range(dil):
            dst_ref[pl.ds(r, tm // dil, stride=dil), :] = src_ref[0, r, :, sl].astype(F32)

    to_token_order(l1_ref, ls1_ref, slice(None))
    to_token_order(l2_ref, ls2_ref, slice(None))
    l0, l1, l2 = l0_ref[0, 0], ls1_ref[...], ls2_ref[...]
    m = jnp.maximum(jnp.maximum(l0, l1), l2)
    e0, e1, e2 = jnp.exp(l0 - m), jnp.exp(l1 - m), jnp.exp(l2 - m)
    inv = 1.0 / (e0 + e1 + e2)
    a0, a1, a2 = e0 * inv, e1 * inv, e2 * inv
    for hh in range(n_heads):
        sl = slice(hh * HEAD_DIM, (hh + 1) * HEAD_DIM)
        col = slice(hh * (BLOCK // n_heads), hh * (BLOCK // n_heads) + 1)
        to_token_order(o1_ref, os1_ref, sl)
        to_token_order(o2_ref, os2_ref, sl)
        y = (a0[:, col] * o0_ref[0, 0, :, sl].astype(F32) + a1[:, col] * os1_ref[...]
             + a2[:, col] * os2_ref[...])
        y_ref[:, sl] = y.astype(BF16)
    proj = jnp.dot(y_ref[...], w_ref[...], preferred_element_type=F32)
    out_ref[...] = h_ref[...] + g_ref[0] * proj


def out_proj_merge(outs, lses, w, h, gate, *, layer, seq, tm=256):
    t, d = h.shape
    width = w.shape[1]
    per_batch = seq // tm
    row = pl.BlockSpec((tm, d), lambda i: (i, 0))

    def grp_spec(a):
        dil, cols = a.shape[1], a.shape[3]
        return pl.BlockSpec((1, dil, tm // dil, cols), lambda i: (i // per_batch, 0, i % per_batch, 0))

    return pl.pallas_call(
        _out_proj_merge_kernel,
        grid=(t // tm,),
        in_specs=[grp_spec(a) for a in (*outs, *lses)] + [
            pl.BlockSpec((None, width, d), lambda i: (layer, 0, 0)),
            row,
            pl.BlockSpec((1, 1, d), lambda i: (i // per_batch, 0, 0)),
        ],
        out_specs=row,
        out_shape=jax.ShapeDtypeStruct((t, d), F32),
        scratch_shapes=[pltpu.VMEM((tm, HEAD_DIM), F32)] * 4 + [pltpu.VMEM((tm, width), BF16)],
        compiler_params=_params("arbitrary"),
        name="out_proj_merge",
    )(*outs, *lses, w, h, gate)


def _out_proj_kernel(y_ref, w_ref, h_ref, g_ref, out_ref):
    proj = jnp.dot(y_ref[...], w_ref[...], preferred_element_type=F32)
    out_ref[...] = h_ref[...] + g_ref[0] * proj


def out_proj(y, w, h, gate, *, layer, seq, tm=512):
    t, d = h.shape
    width = w.shape[1]
    per_batch = seq // tm
    row = lambda cols: pl.BlockSpec((tm, cols), lambda i: (i, 0))
    return pl.pallas_call(
        _out_proj_kernel,
        grid=(t // tm,),
        in_specs=[row(width), pl.BlockSpec((None, width, d), lambda i: (layer, 0, 0)), row(d),
                  pl.BlockSpec((1, 1, d), lambda i: (i // per_batch, 0, 0))],
        out_specs=row(d),
        out_shape=jax.ShapeDtypeStruct((t, d), F32),
        compiler_params=_params("arbitrary"),
        name="out_proj",
    )(y, w, h, gate)


def _split2(x):
    hi = x.astype(BF16)
    lo = (x - hi.astype(F32)).astype(BF16)
    return hi, lo


def _sb_kernel(q_ref, k_ref, v_ref, o_ref, acc_ref, decay_ref, *, scale):
    nq = q_ref.shape[1] // BLOCK
    q_base = pl.program_id(2) * q_ref.shape[1]
    n_heads = acc_ref.shape[0]
    row = lax.broadcasted_iota(jnp.int32, (BLOCK, BLOCK), 0)
    col = lax.broadcasted_iota(jnp.int32, (BLOCK, BLOCK), 1)
    before = col < row
    tri = jnp.where(row > col, 1.0, 0.0).astype(BF16)
    tri_ext = jnp.concatenate([tri, jnp.ones((BLOCK, BLOCK), BF16)], axis=1)

    def tails_and_sums(sps):
        parts = [t for sp in sps for t in _split2(sp)]
        r = jnp.dot(jnp.concatenate(parts, axis=0), tri_ext, preferred_element_type=F32)
        out = []
        for g in range(len(sps)):
            hi = r[(2 * g) * BLOCK:(2 * g + 1) * BLOCK]
            lo = r[(2 * g + 1) * BLOCK:(2 * g + 2) * BLOCK]
            both = hi + lo
            out.append((both[:, :BLOCK], both[:, BLOCK:]))
        return out

    def softplus(z):
        return jnp.maximum(z, 0.0) + jnp.log1p(jnp.exp(-jnp.abs(z)))

    def logits(q0, k0, g):
        cs = slice(g * HEAD_DIM, (g + 1) * HEAD_DIM)
        return lax.dot_general(q_ref[0, pl.ds(q0, BLOCK), cs], k_ref[0, pl.ds(k0, BLOCK), cs],
                               NT_DIMS, preferred_element_type=F32) * scale

    def values(k0, g):
        return v_ref[0, pl.ds(k0, BLOCK), g * HEAD_DIM:(g + 1) * HEAD_DIM]

    def qblock(i, carry):
        q0 = pl.multiple_of(i * BLOCK, BLOCK)
        d0 = pl.multiple_of(q_base + q0, BLOCK)
        zs = [logits(q0, d0, g) for g in range(n_heads)]
        sps = [softplus(z) for z in zs]
        ts = tails_and_sums([jnp.where(before, sp, 0.0) for sp in sps])
        live = None
        for g in range(n_heads):
            tail, rsum = ts[g]
            a = jnp.where(before, jnp.exp(zs[g] - sps[g] - tail), 0.0)
            acc_ref[g] = jnp.dot(a.astype(BF16), values(d0, g), preferred_element_type=F32)
            decay_ref[g] = rsum
            lo = jnp.min(rsum)
            live = lo if live is None else jnp.minimum(live, lo)

        def cond(c):
            j, live = c
            return jnp.logical_and(j >= 0, live < SB_DEAD_DECAY)

        def body(c):
            j, _ = c
            k0 = pl.multiple_of(j * BLOCK, BLOCK)
            zs = [logits(q0, k0, g) for g in range(n_heads)]
            sps = [softplus(z) for z in zs]
            ts = tails_and_sums(sps)
            live = None
            for g in range(n_heads):
                tail, rsum = ts[g]
                decay = decay_ref[g]
                a = jnp.exp(zs[g] - sps[g] - (decay + tail))
                acc_ref[g] += jnp.dot(a.astype(BF16), values(k0, g), preferred_element_type=F32)
                decay = decay + rsum
                decay_ref[g] = decay
                lo = jnp.min(decay)
                live = lo if live is None else jnp.minimum(live, lo)
            return j - 1, live

        lax.while_loop(cond, body, (d0 // BLOCK - 1, live))
        for g in range(n_heads):
            o_ref[0, pl.ds(q0, BLOCK), g * HEAD_DIM:(g + 1) * HEAD_DIM] = acc_ref[g].astype(o_ref.dtype)
        return carry

    lax.fori_loop(0, nq, qblock, 0)


def stick_breaking_attention(qkv, *, n_heads):
    b, s, _ = qkv.shape
    hps = min(SB_HEADS_PER_STEP, n_heads)
    steps = n_heads // hps
    width = hps * HEAD_DIM
    qc = min(SB_Q_CHUNK, s)

    def spec(which):
        return pl.BlockSpec((1, s, width), lambda bi, h, c: (bi, 0, which * steps + h))

    return pl.pallas_call(
        functools.partial(_sb_kernel, scale=1.0 / math.sqrt(HEAD_DIM)),
        grid=(b, steps, s // qc),
        in_specs=[pl.BlockSpec((1, qc, width), lambda bi, h, c: (bi, c, h)), spec(1), spec(2)],
        out_specs=pl.BlockSpec((1, qc, width), lambda bi, h, c: (bi, c, h)),
        out_shape=jax.ShapeDtypeStruct((b, s, n_heads * HEAD_DIM), BF16),
        scratch_shapes=[pltpu.VMEM((hps, BLOCK, HEAD_DIM), F32),
                        pltpu.VMEM((hps, BLOCK, BLOCK), F32)],
        compiler_params=_params("arbitrary", "arbitrary", "arbitrary"),
        name="stick_breaking",
    )(qkv, qkv, qkv)


UNRANKED = 127.0


def _as_f32(r):
    return jnp.asarray(r).astype(F32)


def _extract_ranked(s, dst_ref, count):
    n = s.shape[0]
    idx = lax.broadcasted_iota(jnp.int32, s.shape, 0).astype(F32)

    def body(r, carry):
        s, rank = carry
        m = jnp.max(s, axis=0, keepdims=True)
        first = jnp.min(jnp.where(s == m, idx, float(n)), axis=0, keepdims=True)
        pick = idx == first
        dst_ref[pl.ds(r, 1), :] = m
        return jnp.where(pick, -jnp.inf, s), jnp.where(pick, _as_f32(r), rank)

    _, rank = lax.fori_loop(0, count, body, (s, jnp.full(s.shape, UNRANKED, F32)))
    return rank


def _candidates(sv1, sv2):
    k = PEER_TOPK
    groups = [sv1[0:1] + sv2]
    groups += [sv1[a:a + 1] + sv2[0:k // 2] for a in range(1, k // 2)]
    groups += [sv1[k // 2:] + sv2[0:1]]
    spans = [(0, k)] + [(k + (a - 1) * (k // 2), k // 2) for a in range(1, k // 2)]
    tail0 = k + (k // 2 - 1) * (k // 2)
    spans += [(tail0 + a, 1) for a in range(k // 2)]
    return jnp.concatenate(groups, axis=0), spans


def _staircase_counts(taken, spans):
    return [jnp.sum(taken[lo:lo + n], axis=0, keepdims=True) for lo, n in spans]


def _peer_select_kernel(h_ref, g_ref, sh_ref, sc_ref, wq_ref, keys_ref,
                        xf_ref, r2_ref, a2_ref, lim_ref, a1_ref,
                        qt_ref, sv1_tiles, sv2_tiles, best_ref):
    xf = _norm_modulate(h_ref[...], g_ref[...], sh_ref[0], sc_ref[0]).astype(BF16)
    xf_ref[...] = xf
    qt_ref[...] = lax.dot_general(wq_ref[...], xf, NT_DIMS, preferred_element_type=F32).astype(BF16)
    k = PEER_TOPK
    half = N_SUBKEYS
    kf = float(k)
    n_tiles = sv2_tiles.shape[0]

    def head(h, carry):
        r0 = pl.multiple_of(h * 2 * half, 2 * half)
        s1 = jnp.dot(keys_ref[h, 0], qt_ref[pl.ds(r0, half), :], preferred_element_type=F32)
        s2 = jnp.dot(keys_ref[h, 1], qt_ref[pl.ds(r0 + half, half), :], preferred_element_type=F32)

        def sorted1():
            return jnp.concatenate([sv1_tiles[lt] for lt in range(n_tiles)], axis=1)

        def sorted2():
            return jnp.concatenate([sv2_tiles[lt] for lt in range(n_tiles)], axis=1)

        def emit(rank2, lim):
            sv1, sv2, best = sorted1(), sorted2(), best_ref[...]
            z = jnp.sum(jnp.exp(best - best[0:1]), axis=0, keepdims=True)
            r2_ref[h] = rank2.astype(BF16)
            a2_ref[h] = jnp.exp(s2 - sv2[0:1]).astype(BF16)
            lim_ref[h] = lim
            a1_ref[h] = jnp.exp(s1 - sv1[0:1]) / z

        ranks = []
        for lt in range(n_tiles):
            def round12(r, c, lt=lt):
                w1, w2, rank = c
                m1 = jnp.max(w1, axis=0, keepdims=True)
                m2 = jnp.max(w2, axis=0, keepdims=True)
                sv1_tiles[lt, pl.ds(r, 1), :] = m1
                sv2_tiles[lt, pl.ds(r, 1), :] = m2
                pick = w2 == m2
                return (jnp.where(w1 == m1, -jnp.inf, w1), jnp.where(pick, -jnp.inf, w2),
                        jnp.where(pick, _as_f32(r), rank))

            ls = slice(lt * 128, (lt + 1) * 128)
            init = (s1[:, ls], s2[:, ls], jnp.full((half, 128), UNRANKED, F32))
            ranks.append(lax.fori_loop(0, k, round12, init)[2])
        rank2 = jnp.concatenate(ranks, axis=1)

        sv1 = sorted1()
        cand, spans = _candidates(sv1, sorted2())

        def round_c(r, w):
            m = jnp.max(w, axis=0, keepdims=True)
            best_ref[pl.ds(r, 1), :] = m
            return jnp.where(w == m, -jnp.inf, w)

        lax.fori_loop(0, k, round_c, cand)
        taken = jnp.where(cand >= best_ref[k - 1:k, :], 1.0, 0.0)
        counts = _staircase_counts(taken, spans)
        lim = jnp.zeros_like(s1)
        for a in range(k):
            lim = jnp.where(s1 == sv1[a:a + 1], counts[a], lim)
        emit(rank2, lim)

        n1 = jnp.sum(jnp.where(s1 >= sv1[k - 1:k], 1.0, 0.0), axis=0, keepdims=True)
        n2 = jnp.sum(jnp.where(rank2 < kf, 1.0, 0.0), axis=0, keepdims=True)
        nc = jnp.sum(taken, axis=0, keepdims=True)
        tied = jnp.max(jnp.maximum(jnp.maximum(n1, n2), nc)) > kf

        @pl.when(tied)
        def _():
            tiles = [slice(lt * 128, (lt + 1) * 128) for lt in range(n_tiles)]
            rank1 = jnp.concatenate([_extract_ranked(s1[:, ls], sv1_tiles.at[lt], k)
                                     for lt, ls in enumerate(tiles)], axis=1)
            ranks = [_extract_ranked(s2[:, ls], sv2_tiles.at[lt], k) for lt, ls in enumerate(tiles)]
            cand, spans = _candidates(sorted1(), sorted2())
            taken = jnp.where(_extract_ranked(cand, best_ref, k) < kf, 1.0, 0.0)
            counts = _staircase_counts(taken, spans)
            lim = jnp.zeros_like(s1)
            for a in range(k):
                lim = jnp.where(rank1 == float(a), counts[a], lim)
            emit(jnp.concatenate(ranks, axis=1), lim)

        return carry

    lax.fori_loop(0, PEER_HEADS, head, 0)


def peer_select(h, g, shift, scale, wq_t, keys, *, layer, seq, tm=256):
    t, d = h.shape
    nq = wq_t.shape[1]
    per_batch = seq // tm
    tab_spec = pl.BlockSpec((PEER_HEADS, N_SUBKEYS, tm), lambda i: (0, 0, i))
    tab = lambda dt: jax.ShapeDtypeStruct((PEER_HEADS, N_SUBKEYS, t), dt)
    return pl.pallas_call(
        _peer_select_kernel,
        grid=(t // tm,),
        in_specs=[
            pl.BlockSpec((tm, d), lambda i: (i, 0)),
            pl.BlockSpec((1, d), lambda i: (0, 0)),
            pl.BlockSpec((1, 1, d), lambda i: (i // per_batch, 0, 0)),
            pl.BlockSpec((1, 1, d), lambda i: (i // per_batch, 0, 0)),
            pl.BlockSpec((None, nq, d), lambda i: (layer, 0, 0)),
            pl.BlockSpec((None,) + keys.shape[1:], lambda i: (layer, 0, 0, 0, 0)),
        ],
        out_specs=[pl.BlockSpec((tm, d), lambda i: (i, 0)), tab_spec, tab_spec, tab_spec, tab_spec],
        out_shape=[jax.ShapeDtypeStruct((t, d), BF16), tab(BF16), tab(BF16), tab(F32), tab(F32)],
        scratch_shapes=[pltpu.VMEM((nq, tm), BF16),
                        pltpu.VMEM((tm // 128, PEER_TOPK, 128), F32),
                        pltpu.VMEM((tm // 128, PEER_TOPK, 128), F32),
                        pltpu.VMEM((PEER_TOPK, tm), F32)],
        compiler_params=_params("arbitrary"),
        name="peer_select",
    )(h, g, shift, scale, wq_t, keys)


def _peer_mix_kernel(xf_ref, u_ref, vt_ref, r2_ref, a2_ref, lim_ref, a1_ref, h_ref, g_ref,
                     o_ref, acc_ref, hid_a_ref, hid_b_ref, p_ref, *, n_blocks):
    g = pl.program_id(0)
    eb, tc = hid_a_ref.shape
    rows_per_step = eb // N_SUBKEYS
    mix_block = jnp.maximum(g - 1, 0) % n_blocks

    @pl.when(mix_block == 0)
    def _():
        acc_ref[...] = jnp.zeros_like(acc_ref)

    @pl.when(g == 0)
    def _():
        hid_b_ref[...] = jnp.zeros_like(hid_b_ref)

    def step(cur_ref, prev_ref):
        cur_ref[...] = lax.dot_general(u_ref[...], xf_ref[...], NT_DIMS, preferred_element_type=F32)
        for kk in range(rows_per_step):
            gate = None
            for h in range(PEER_HEADS):
                lim = jnp.broadcast_to(lim_ref[h, kk:kk + 1, :], (16, tc)).astype(BF16)
                a1 = jnp.broadcast_to(a1_ref[h, kk:kk + 1, :], (16, tc)).astype(BF16)
                lim = jnp.concatenate([lim] * (N_SUBKEYS // 16), axis=0)
                a1 = jnp.concatenate([a1] * (N_SUBKEYS // 16), axis=0)
                w = jnp.where(r2_ref[h] < lim, a2_ref[h] * a1, jnp.zeros((), BF16))
                gate = w if gate is None else gate + w
            sl = slice(kk * N_SUBKEYS, (kk + 1) * N_SUBKEYS)
            hid = prev_ref[sl, :]
            act = 0.5 * hid * (1.0 + lax.erf(hid * INV_SQRT2))
            p_ref[sl, :] = act.astype(BF16) * gate
        acc_ref[...] += jnp.dot(vt_ref[...], p_ref[...], preferred_element_type=F32)

    @pl.when(g % 2 == 0)
    def _():
        step(hid_a_ref, hid_b_ref)

    @pl.when(g % 2 == 1)
    def _():
        step(hid_b_ref, hid_a_ref)

    @pl.when(jnp.logical_and(mix_block == n_blocks - 1, g > 0))
    def _():
        o_ref[...] = h_ref[...] + g_ref[0] * acc_ref[...].T


def peer_mix(xf, u, vt, r2, a2, lim, a1, h, gate, *, layer, seq, tc=512, eb=1024):
    t, d = h.shape
    n_blocks = u.shape[1] // eb
    n_items = (t // tc) * n_blocks
    per_batch = seq // tc
    rows = eb // N_SUBKEYS

    def hid_item(g):
        return jnp.minimum(g, n_items - 1)

    def mix_item(g):
        return jnp.maximum(g - 1, 0)

    tab_spec = pl.BlockSpec((PEER_HEADS, N_SUBKEYS, tc), lambda g: (0, 0, mix_item(g) // n_blocks))
    row_spec = pl.BlockSpec((PEER_HEADS, rows, tc),
                            lambda g: (0, mix_item(g) % n_blocks, mix_item(g) // n_blocks))
    chunk_spec = pl.BlockSpec((tc, d), lambda g: (mix_item(g) // n_blocks, 0))
    return pl.pallas_call(
        functools.partial(_peer_mix_kernel, n_blocks=n_blocks),
        grid=(n_items + 1,),
        in_specs=[
            pl.BlockSpec((tc, d), lambda g: (hid_item(g) // n_blocks, 0)),
            pl.BlockSpec((None, eb, d), lambda g: (layer, hid_item(g) % n_blocks, 0)),
            pl.BlockSpec((None, d, eb), lambda g: (layer, 0, mix_item(g) % n_blocks)),
            tab_spec, tab_spec, row_spec, row_spec,
            chunk_spec,
            pl.BlockSpec((1, 1, d), lambda g: (mix_item(g) // n_blocks // per_batch, 0, 0)),
        ],
        out_specs=chunk_spec,
        out_shape=jax.ShapeDtypeStruct((t, d), F32),
        scratch_shapes=[pltpu.VMEM((d, tc), F32),
                        pltpu.VMEM((eb, tc), F32),
                        pltpu.VMEM((eb, tc), F32),
                        pltpu.VMEM((eb, tc), BF16)],
        compiler_params=_params("arbitrary"),
        name="peer_mix",
    )(xf, u, vt, r2, a2, lim, a1, h, gate)


def kernel(x, c, rel_bias, norm1_g, norm2_g, w_ada, b_ada, a_w_in, a_q_gain, a_k_gain,
           b_w_in, w_out, peer_w_query, peer_sub_keys, peer_u, peer_v):
    b, s, d = x.shape
    depth = w_ada.shape[0]
    n_groups = len(DIL_GROUPS)
    heads = d // HEAD_DIM

    mod = adaln(c, w_ada, b_ada).reshape(depth, b, 6, 1, d)
    a_w, b_w, w_o = a_w_in.astype(BF16), b_w_in.astype(BF16), w_out.astype(BF16)
    wq_t = jnp.swapaxes(peer_w_query, 1, 2).astype(BF16)
    keys = peer_sub_keys.astype(BF16)
    u, vt = peer_u.astype(BF16), jnp.swapaxes(peer_v, 1, 2).astype(BF16)
    h = x.reshape(b * s, d)
    for i in range(depth):
        sh1, sc1, g1, sh2, sc2, g2 = (mod[i, :, k] for k in range(6))
        j = i // 2
        if i % 2 == 0:
            ones = jnp.ones((HEAD_DIM,), F32)
            gain = jnp.stack([jnp.tile(row, heads) for g in range(n_groups)
                              for row in (a_q_gain[j, g] / math.sqrt(HEAD_DIM), a_k_gain[j, g], ones)])
            qkvs = in_proj_dilated(h, norm1_g[i][None], sh1, sc1, a_w, gain[:, None, :], layer=j, seq=s)
            outs, lses = [], []
            for g, (window, dil) in enumerate(DIL_GROUPS):
                bias = _band_bias(rel_bias[:, g * heads:(g + 1) * heads], window, dil)
                o, lse = dilated_group_attention(qkvs[g], bias, n_heads=heads)
                outs.append(o)
                lses.append(lse)
            h = out_proj_merge(outs, lses, w_o, h, g1, layer=i, seq=s)
        else:
            qkv = in_proj(h, norm1_g[i][None], sh1, sc1, b_w, layer=j, seq=s)
            y = stick_breaking_attention(qkv.reshape(b, s, -1), n_heads=heads)
            h = out_proj(y.reshape(b * s, d), w_o, h, g1, layer=i, seq=s)
        xf, r2, a2, lim, a1 = peer_select(h, norm2_g[i][None], sh2, sc2, wq_t, keys, layer=i, seq=s)
        h = peer_mix(xf, u, vt, r2, a2, lim, a1, h, g2, layer=i, seq=s)
    return h.reshape(b, s, d)
```

```python
import functools
import math

import jax
import jax.numpy as jnp
from jax import lax
from jax.experimental import pallas as pl
from jax.experimental.pallas import tpu as pltpu

F32 = jnp.float32
BF16 = jnp.bfloat16

HEAD_DIM = 128
BLOCK = 128
DIL_GROUPS = ((128, 1), (512, 4), (2048, 16))
REL_BUCKETS = 32
REL_MAX_DIST = 2048
PEER_HEADS = 8
N_SUBKEYS = 128
PEER_TOPK = 16
PEER_BLOCK_ROWS = 4
EPS = 1e-6
NEG_BIG = -1e30
SB_DEAD_DECAY = 104.0
SB_HEADS_PER_STEP = 8
SB_Q_CHUNK = 1024
INV_SQRT2 = 0.7071067811865476

VMEM_LIMIT = 56 * 1024 * 1024

NT_DIMS = (((1,), (1,)), ((), ()))


def _params(*sem):
    return pltpu.CompilerParams(dimension_semantics=sem, vmem_limit_bytes=VMEM_LIMIT)


def _adaln_kernel(c_ref, w_lo_ref, w_hi_ref, b_ref, o_ref):
    c = c_ref[...]
    cs = c / (1.0 + jnp.exp(-c))
    half = w_lo_ref.shape[1]
    dot = functools.partial(jnp.dot, preferred_element_type=F32, precision=lax.Precision.HIGHEST)
    o_ref[0] = dot(cs[:, :half], w_lo_ref[0]) + dot(cs[:, half:], w_hi_ref[0]) + b_ref[0]


def adaln(c, w_ada, b_ada, *, tn=1024):
    depth, d, n = w_ada.shape
    b = c.shape[0]
    rows = 8
    c_pad = jnp.pad(c, ((0, rows - b), (0, 0)))
    out = pl.pallas_call(
        _adaln_kernel,
        grid=(depth, n // tn),
        in_specs=[
            pl.BlockSpec((rows, d), lambda i, j: (0, 0)),
            pl.BlockSpec((1, d // 2, tn), lambda i, j: (i, 0, j)),
            pl.BlockSpec((1, d // 2, tn), lambda i, j: (i, 1, j)),
            pl.BlockSpec((1, 1, tn), lambda i, j: (i, 0, j)),
        ],
        out_specs=pl.BlockSpec((1, rows, tn), lambda i, j: (i, 0, j)),
        out_shape=jax.ShapeDtypeStruct((depth, rows, n), F32),
        compiler_params=_params("arbitrary", "arbitrary"),
        name="adaln",
    )(c_pad, w_ada, w_ada, b_ada.reshape(depth, 1, n))
    return out[:, :b]


def _norm_modulate(x, g, shift, scale):
    ms = jnp.mean(x * x, axis=-1, keepdims=True)
    y = x * lax.rsqrt(ms + EPS) * g
    return y * (1.0 + scale) + shift


def _in_proj_kernel(h_ref, g_ref, sh_ref, sc_ref, w_ref, o_ref, xm_ref):
    @pl.when(pl.program_id(1) == 0)
    def _():
        xm_ref[...] = _norm_modulate(h_ref[...], g_ref[...], sh_ref[0], sc_ref[0]).astype(BF16)

    o_ref[...] = jnp.dot(xm_ref[...], w_ref[...], preferred_element_type=F32).astype(o_ref.dtype)


def in_proj(h, g, shift, scale, w, *, layer, seq, tm=1024, tn=2048):
    t, d = h.shape
    n = w.shape[2]
    per_batch = seq // tm
    return pl.pallas_call(
        _in_proj_kernel,
        grid=(t // tm, n // tn),
        in_specs=[
            pl.BlockSpec((tm, d), lambda i, j: (i, 0)),
            pl.BlockSpec((1, d), lambda i, j: (0, 0)),
            pl.BlockSpec((1, 1, d), lambda i, j: (i // per_batch, 0, 0)),
            pl.BlockSpec((1, 1, d), lambda i, j: (i // per_batch, 0, 0)),
            pl.BlockSpec((None, d, tn), lambda i, j: (layer, 0, j)),
        ],
        out_specs=pl.BlockSpec((tm, tn), lambda i, j: (i, j)),
        out_shape=jax.ShapeDtypeStruct((t, n), BF16),
        scratch_shapes=[pltpu.VMEM((tm, d), BF16)],
        compiler_params=_params("arbitrary", "arbitrary"),
        name="in_proj",
    )(h, g, shift, scale, w)


def _in_proj_dilated_kernel(h_ref, g_ref, sh_ref, sc_ref, w_ref, gain_ref, o0_ref, o1_ref, o2_ref,
                            xm0_ref, xm1_ref, xm2_ref, slab_ref, *, parts):
    j = pl.program_id(1)
    seg = j // parts
    tm, d = xm0_ref.shape
    xm_refs = (xm0_ref, xm1_ref, xm2_ref)

    @pl.when(j == 0)
    def _():
        xm = _norm_modulate(h_ref[...], g_ref[...], sh_ref[0], sc_ref[0])
        for sb in range(d // HEAD_DIM):
            slab_ref[sb] = xm[:, sb * HEAD_DIM:(sb + 1) * HEAD_DIM]
        for (_, dil), xm_ref in zip(DIL_GROUPS, xm_refs):
            if dil == 1:
                xm_ref[...] = xm.astype(BF16)
                continue
            rows = tm // dil
            for sb in range(d // HEAD_DIM):
                for r in range(dil):
                    xm_ref[r * rows:(r + 1) * rows, sb * HEAD_DIM:(sb + 1) * HEAD_DIM] = (
                        slab_ref[sb, pl.ds(r, rows, stride=dil), :].astype(BF16))

    def project(xm_ref, o_ref, dil):
        rows = tm // dil
        acc = jnp.dot(xm_ref[...], w_ref[...], preferred_element_type=F32)

        @pl.when(seg % 3 == 2)
        def _():
            for r in range(dil):
                o_ref[0, r] = acc[r * rows:(r + 1) * rows].astype(BF16)

        @pl.when(seg % 3 != 2)
        def _():
            for hh in range(acc.shape[1] // HEAD_DIM):
                sl = slice(hh * HEAD_DIM, (hh + 1) * HEAD_DIM)
                a = acc[:, sl]
                ms = jnp.mean(a * a, axis=-1, keepdims=True)
                y = (a * lax.rsqrt(ms + EPS) * gain_ref[0, :, sl]).astype(BF16)
                for r in range(dil):
                    o_ref[0, r, :, sl] = y[r * rows:(r + 1) * rows]

    for grp, ((_, dil), xm_ref, o_ref) in enumerate(zip(DIL_GROUPS, xm_refs, (o0_ref, o1_ref, o2_ref))):
        @pl.when(seg // 3 == grp)
        def _(dil=dil, xm_ref=xm_ref, o_ref=o_ref):
            project(xm_ref, o_ref, dil)


def in_proj_dilated(h, g, shift, scale, w, gain, *, layer, seq, tm=1024, tn=1024):
    t, d = h.shape
    n = w.shape[2]
    b = t // seq
    per_batch = seq // tm
    n_groups = len(DIL_GROUPS)
    width = gain.shape[2]
    parts = width // tn
    assert n == n_groups * 3 * width

    def out_spec(grp, dil):
        def imap(i, j):
            return (i // per_batch, 0, i % per_batch, jnp.clip(j - 3 * parts * grp, 0, 3 * parts - 1))
        return pl.BlockSpec((1, dil, tm // dil, tn), imap)

    return pl.pallas_call(
        functools.partial(_in_proj_dilated_kernel, parts=parts),
        grid=(t // tm, n // tn),
        in_specs=[
            pl.BlockSpec((tm, d), lambda i, j: (i, 0), pipeline_mode=pl.Buffered(1)),
            pl.BlockSpec((1, d), lambda i, j: (0, 0)),
            pl.BlockSpec((1, 1, d), lambda i, j: (i // per_batch, 0, 0)),
            pl.BlockSpec((1, 1, d), lambda i, j: (i // per_batch, 0, 0)),
            pl.BlockSpec((None, d, tn), lambda i, j: (layer, 0, j)),
            pl.BlockSpec((1, 1, tn), lambda i, j: (j // parts, 0, j % parts)),
        ],
        out_specs=[out_spec(grp, dil) for grp, (_, dil) in enumerate(DIL_GROUPS)],
        out_shape=[jax.ShapeDtypeStruct((b, dil, seq // dil, 3 * width), BF16) for _, dil in DIL_GROUPS],
        scratch_shapes=[pltpu.VMEM((tm, d), BF16)] * 3 + [pltpu.VMEM((d // HEAD_DIM, tm, HEAD_DIM), F32)],
        compiler_params=_params("arbitrary", "arbitrary"),
        name="in_proj_dilated",
    )(h, g, shift, scale, w, gain)


def _dil_attn_kernel(q_ref, kp_ref, kc_ref, vp_ref, vc_ref, bias_ref, o_ref, lse_ref, *, n_heads):
    n = pl.program_id(2)
    n_sub = q_ref.shape[2] // BLOCK
    lane = lax.broadcasted_iota(jnp.int32, (1, 2 * BLOCK), 1)
    pen = jnp.where(jnp.logical_and(lane < BLOCK, n == 0), NEG_BIG, 0.0).astype(F32)
    ones = jnp.ones((2 * BLOCK, HEAD_DIM), BF16)
    lse_lanes = BLOCK // n_heads
    lane_head = lax.broadcasted_iota(jnp.int32, (1, BLOCK), 1) // lse_lanes
    for sb in range(n_sub):
        rows = slice(sb * BLOCK, (sb + 1) * BLOCK)
        prev = slice((sb - 1) * BLOCK, sb * BLOCK)
        lse_pack = jnp.zeros((BLOCK, BLOCK), F32)
        for hh in range(n_heads):
            sl = slice(hh * HEAD_DIM, (hh + 1) * HEAD_DIM)
            q = q_ref[0, 0, rows, sl]
            k_prev = kp_ref[0, 0, :, sl] if sb == 0 else kc_ref[0, 0, prev, sl]
            v_prev = vp_ref[0, 0, :, sl] if sb == 0 else vc_ref[0, 0, prev, sl]
            k = jnp.concatenate([k_prev, kc_ref[0, 0, rows, sl]], axis=0)
            v = jnp.concatenate([v_prev, vc_ref[0, 0, rows, sl]], axis=0)
            logits = lax.dot_general(q, k, NT_DIMS, preferred_element_type=F32)
            logits = logits + bias_ref[hh]
            if sb == 0:
                logits = logits + pen
            m = jnp.max(logits, axis=-1, keepdims=True)
            p = jnp.exp(logits - m).astype(BF16)
            v_ext = jnp.concatenate([v, ones], axis=1)
            pv = jnp.dot(p, v_ext, preferred_element_type=F32)
            denom = pv[:, HEAD_DIM:]
            o_ref[0, 0, rows, sl] = (pv[:, :HEAD_DIM] / denom).astype(o_ref.dtype)
            lse_pack = jnp.where(lane_head == hh, m + jnp.log(denom), lse_pack)
        lse_ref[0, 0, rows, :] = lse_pack


def dilated_group_attention(qkv, bias, *, n_heads, q_blocks=2):
    b, dil, sub_len, c = qkv.shape
    assert BLOCK % n_heads == 0
    width = n_heads * HEAD_DIM
    q_blocks = min(q_blocks, sub_len // BLOCK)
    rows = q_blocks * BLOCK
    steps = sub_len // rows

    def cur(which):
        return pl.BlockSpec((1, 1, rows, width), lambda bi, r, n: (bi, r, n, which))

    def prev(which):
        return pl.BlockSpec((1, 1, BLOCK, width),
                            lambda bi, r, n: (bi, r, jnp.maximum(q_blocks * n - 1, 0), which))

    return pl.pallas_call(
        functools.partial(_dil_attn_kernel, n_heads=n_heads),
        grid=(b, dil, steps),
        in_specs=[cur(0), prev(1), cur(1), prev(2), cur(2),
                  pl.BlockSpec((n_heads, BLOCK, 2 * BLOCK), lambda bi, r, n: (0, 0, 0))],
        out_specs=[pl.BlockSpec((1, 1, rows, width), lambda bi, r, n: (bi, r, n, 0)),
                   pl.BlockSpec((1, 1, rows, BLOCK), lambda bi, r, n: (bi, r, n, 0))],
        out_shape=[jax.ShapeDtypeStruct((b, dil, sub_len, width), BF16),
                   jax.ShapeDtypeStruct((b, dil, sub_len, BLOCK), F32)],
        compiler_params=_params("arbitrary", "arbitrary", "arbitrary"),
        name=f"dilated_attn_d{dil}",
    )(qkv, qkv, qkv, qkv, qkv, bias)


def _rel_bucket(dist):
    exact = REL_BUCKETS // 2
    d_f = jnp.maximum(dist, exact).astype(F32)
    log_b = exact + (jnp.log(d_f / exact) / math.log(REL_MAX_DIST / exact)
                     * (REL_BUCKETS - exact)).astype(jnp.int32)
    return jnp.where(dist < exact, dist, jnp.minimum(log_b, REL_BUCKETS - 1))


def _band_bias(table_g, window, dil):
    w_steps = window // dil
    period = 3 * BLOCK
    per_delta = table_g[_rel_bucket(jnp.arange(w_steps + 1) * dil)].astype(F32)
    row = jnp.full((period, table_g.shape[1]), NEG_BIG, F32)
    row = row.at[BLOCK - w_steps:BLOCK + 1].set(per_delta[::-1])
    flat = jnp.tile(row.T, (1, BLOCK))[:, :BLOCK * (period - 1)]
    return flat.reshape(-1, BLOCK, period - 1)[:, :, :2 * BLOCK]


def _out_proj_merge_kernel(o0_ref, o1_ref, o2_ref, l0_ref, l1_ref, l2_ref, w_ref, h_ref, g_ref, out_ref,
                           os1_ref, os2_ref, ls1_ref, ls2_ref, y_ref):
    tm = out_ref.shape[0]
    n_heads = y_ref.shape[1] // HEAD_DIM

    def to_token_order(src_ref, dst_ref, sl):
        dil = src_ref.shape[1]
        for r in range(dil):
            dst_ref[pl.ds(r, tm // dil, stride=dil), :] = src_ref[0, r, :, sl].astype(F32)

    to_token_order(l1_ref, ls1_ref, slice(None))
    to_token_order(l2_ref, ls2_ref, slice(None))
    l0, l1, l2 = l0_ref[0, 0], ls1_ref[...], ls2_ref[...]
    m = jnp.maximum(jnp.maximum(l0, l1), l2)
    e0, e1, e2 = jnp.exp(l0 - m), jnp.exp(l1 - m), jnp.exp(l2 - m)
    inv = 1.0 / (e0 + e1 + e2)
    a0, a1, a2 = e0 * inv, e1 * inv, e2 * inv
    for hh in range(n_heads):
        sl = slice(hh * HEAD_DIM, (hh + 1) * HEAD_DIM)
        col = slice(hh * (BLOCK // n_heads), hh * (BLOCK // n_heads) + 1)
        to_token_order(o1_ref, os1_ref, sl)
        to_token_order(o2_ref, os2_ref, sl)
        y = (a0[:, col] * o0_ref[0, 0, :, sl].astype(F32) + a1[:, col] * os1_ref[...]
             + a2[:, col] * os2_ref[...])
        y_ref[:, sl] = y.astype(BF16)
    proj = jnp.dot(y_ref[...], w_ref[...], preferred_element_type=F32)
    out_ref[...] = h_ref[...] + g_ref[0] * proj


def out_proj_merge(outs, lses, w, h, gate, *, layer, seq, tm=256):
    t, d = h.shape
    width = w.shape[1]
    per_batch = seq // tm
    row = pl.BlockSpec((tm, d), lambda i: (i, 0))

    def grp_spec(a):
        dil, cols = a.shape[1], a.shape[3]
        return pl.BlockSpec((1, dil, tm // dil, cols), lambda i: (i // per_batch, 0, i % per_batch, 0))

    return pl.pallas_call(
        _out_proj_merge_kernel,
        grid=(t // tm,),
        in_specs=[grp_spec(a) for a in (*outs, *lses)] + [
            pl.BlockSpec((None, width, d), lambda i: (layer, 0, 0)),
            row,
            pl.BlockSpec((1, 1, d), lambda i: (i // per_batch, 0, 0)),
        ],
        out_specs=row,
        out_shape=jax.ShapeDtypeStruct((t, d), F32),
        scratch_shapes=[pltpu.VMEM((tm, HEAD_DIM), F32)] * 4 + [pltpu.VMEM((tm, width), BF16)],
        compiler_params=_params("arbitrary"),
        name="out_proj_merge",
    )(*outs, *lses, w, h, gate)


def _out_proj_kernel(y_ref, w_ref, h_ref, g_ref, out_ref):
    proj = jnp.dot(y_ref[...], w_ref[...], preferred_element_type=F32)
    out_ref[...] = h_ref[...] + g_ref[0] * proj


def out_proj(y, w, h, gate, *, layer, seq, tm=512):
    t, d = h.shape
    width = w.shape[1]
    per_batch = seq // tm
    row = lambda cols: pl.BlockSpec((tm, cols), lambda i: (i, 0))
    return pl.pallas_call(
        _out_proj_kernel,
        grid=(t // tm,),
        in_specs=[row(width), pl.BlockSpec((None, width, d), lambda i: (layer, 0, 0)), row(d),
                  pl.BlockSpec((1, 1, d), lambda i: (i // per_batch, 0, 0))],
        out_specs=row(d),
        out_shape=jax.ShapeDtypeStruct((t, d), F32),
        compiler_params=_params("arbitrary"),
        name="out_proj",
    )(y, w, h, gate)


def _split2(x):
    hi = x.astype(BF16)
    lo = (x - hi.astype(F32)).astype(BF16)
    return hi, lo


def _sb_kernel(q_ref, k_ref, v_ref, o_ref, acc_ref, decay_ref, *, scale):
    nq = q_ref.shape[1] // BLOCK
    q_base = pl.program_id(2) * q_ref.shape[1]
    n_heads = acc_ref.shape[0]
    row = lax.broadcasted_iota(jnp.int32, (BLOCK, BLOCK), 0)
    col = lax.broadcasted_iota(jnp.int32, (BLOCK, BLOCK), 1)
    before = col < row
    tri = jnp.where(row > col, 1.0, 0.0).astype(BF16)
    tri_ext = jnp.concatenate([tri, jnp.ones((BLOCK, BLOCK), BF16)], axis=1)

    def tails_and_sums(sps):
        parts = [t for sp in sps for t in _split2(sp)]
        r = jnp.dot(jnp.concatenate(parts, axis=0), tri_ext, preferred_element_type=F32)
        out = []
        for g in range(len(sps)):
            hi = r[(2 * g) * BLOCK:(2 * g + 1) * BLOCK]
            lo = r[(2 * g + 1) * BLOCK:(2 * g + 2) * BLOCK]
            both = hi + lo
            out.append((both[:, :BLOCK], both[:, BLOCK:]))
        return out

    def softplus(z):
        return jnp.maximum(z, 0.0) + jnp.log1p(jnp.exp(-jnp.abs(z)))

    def logits(q0, k0, g):
        cs = slice(g * HEAD_DIM, (g + 1) * HEAD_DIM)
        return lax.dot_general(q_ref[0, pl.ds(q0, BLOCK), cs], k_ref[0, pl.ds(k0, BLOCK), cs],
                               NT_DIMS, preferred_element_type=F32) * scale

    def values(k0, g):
        return v_ref[0, pl.ds(k0, BLOCK), g * HEAD_DIM:(g + 1) * HEAD_DIM]

    def qblock(i, carry):
        q0 = pl.multiple_of(i * BLOCK, BLOCK)
        d0 = pl.multiple_of(q_base + q0, BLOCK)
        zs = [logits(q0, d0, g) for g in range(n_heads)]
        sps = [softplus(z) for z in zs]
        ts = tails_and_sums([jnp.where(before, sp, 0.0) for sp in sps])
        live = None
        for g in range(n_heads):
            tail, rsum = ts[g]
            a = jnp.where(before, jnp.exp(zs[g] - sps[g] - tail), 0.0)
            acc_ref[g] = jnp.dot(a.astype(BF16), values(d0, g), preferred_element_type=F32)
            decay_ref[g] = rsum
            lo = jnp.min(rsum)
            live = lo if live is None else jnp.minimum(live, lo)

        def cond(c):
            j, live = c
            return jnp.logical_and(j >= 0, live < SB_DEAD_DECAY)

        def body(c):
            j, _ = c
            k0 = pl.multiple_of(j * BLOCK, BLOCK)
            zs = [logits(q0, k0, g) for g in range(n_heads)]
            sps = [softplus(z) for z in zs]
            ts = tails_and_sums(sps)
            live = None
            for g in range(n_heads):
                tail, rsum = ts[g]
                decay = decay_ref[g]
                a = jnp.exp(zs[g] - sps[g] - (decay + tail))
                acc_ref[g] += jnp.dot(a.astype(BF16), values(k0, g), preferred_element_type=F32)
                decay = decay + rsum
                decay_ref[g] = decay
                lo = jnp.min(decay)
                live = lo if live is None else jnp.minimum(live, lo)
            return j - 1, live

        lax.while_loop(cond, body, (d0 // BLOCK - 1, live))
        for g in range(n_heads):
            o_ref[0, pl.ds(q0, BLOCK), g * HEAD_DIM:(g + 1) * HEAD_DIM] = acc_ref[g].astype(o_ref.dtype)
        return carry

    lax.fori_loop(0, nq, qblock, 0)


def stick_breaking_attention(qkv, *, n_heads):
    b, s, _ = qkv.shape
    hps = min(SB_HEADS_PER_STEP, n_heads)
    steps = n_heads // hps
    width = hps * HEAD_DIM
    qc = min(SB_Q_CHUNK, s)

    def spec(which):
        return pl.BlockSpec((1, s, width), lambda bi, h, c: (bi, 0, which * steps + h))

    return pl.pallas_call(
        functools.partial(_sb_kernel, scale=1.0 / math.sqrt(HEAD_DIM)),
        grid=(b, steps, s // qc),
        in_specs=[pl.BlockSpec((1, qc, width), lambda bi, h, c: (bi, c, h)), spec(1), spec(2)],
        out_specs=pl.BlockSpec((1, qc, width), lambda bi, h, c: (bi, c, h)),
        out_shape=jax.ShapeDtypeStruct((b, s, n_heads * HEAD_DIM), BF16),
        scratch_shapes=[pltpu.VMEM((hps, BLOCK, HEAD_DIM), F32),
                        pltpu.VMEM((hps, BLOCK, BLOCK), F32)],
        compiler_params=_params("arbitrary", "arbitrary", "arbitrary"),
        name="stick_breaking",
    )(qkv, qkv, qkv)


UNRANKED = 127.0
RANK_MARK = -(2.0 ** 100)


def _as_f32(r):
    return jnp.asarray(r).astype(F32)


def _extract_ranked(s, dst_ref, count):
    n = s.shape[0]
    idx = lax.broadcasted_iota(jnp.int32, s.shape, 0).astype(F32)

    def body(r, carry):
        s, rank = carry
        m = jnp.max(s, axis=0, keepdims=True)
        first = jnp.min(jnp.where(s == m, idx, float(n)), axis=0, keepdims=True)
        pick = idx == first
        dst_ref[pl.ds(r, 1), :] = m
        return jnp.where(pick, -jnp.inf, s), jnp.where(pick, _as_f32(r), rank)

    _, rank = lax.fori_loop(0, count, body, (s, jnp.full(s.shape, UNRANKED, F32)))
    return rank


def _candidates(sv1, sv2):
    k = PEER_TOPK
    groups = [sv1[0:1] + sv2]
    groups += [sv1[a:a + 1] + sv2[0:k // 2] for a in range(1, k // 2)]
    groups += [sv1[k // 2:] + sv2[0:1]]
    spans = [(0, k)] + [(k + (a - 1) * (k // 2), k // 2) for a in range(1, k // 2)]
    tail0 = k + (k // 2 - 1) * (k // 2)
    spans += [(tail0 + a, 1) for a in range(k // 2)]
    return jnp.concatenate(groups, axis=0), spans


def _staircase_counts(taken, spans):
    return [jnp.sum(taken[lo:lo + n], axis=0, keepdims=True) for lo, n in spans]


def _peer_select_kernel(h_ref, g_ref, sh_ref, sc_ref, wq_ref, keys_ref,
                        xf_ref, r2_ref, a2_ref, lim_ref, a1_ref,
                        qt_ref, sv1_tiles, sv2_tiles, best_ref):
    xf = _norm_modulate(h_ref[...], g_ref[...], sh_ref[0], sc_ref[0]).astype(BF16)
    xf_ref[...] = xf
    qt_ref[...] = lax.dot_general(wq_ref[...], xf, NT_DIMS, preferred_element_type=F32).astype(BF16)
    k = PEER_TOPK
    half = N_SUBKEYS
    kf = float(k)
    n_tiles = sv2_tiles.shape[0]

    def head(h, carry):
        r0 = pl.multiple_of(h * 2 * half, 2 * half)
        s1 = jnp.dot(keys_ref[h, 0], qt_ref[pl.ds(r0, half), :], preferred_element_type=F32)
        s2 = jnp.dot(keys_ref[h, 1], qt_ref[pl.ds(r0 + half, half), :], preferred_element_type=F32)

        def sorted1():
            return jnp.concatenate([sv1_tiles[lt] for lt in range(n_tiles)], axis=1)

        def sorted2():
            return jnp.concatenate([sv2_tiles[lt] for lt in range(n_tiles)], axis=1)

        def emit(rank2, lim):
            sv1, sv2, best = sorted1(), sorted2(), best_ref[...]
            z = jnp.sum(jnp.exp(best - best[0:1]), axis=0, keepdims=True)
            r2_ref[h] = rank2.astype(BF16)
            a2_ref[h] = jnp.exp(s2 - sv2[0:1]).astype(BF16)
            a1 = jnp.exp(s1 - sv1[0:1]) / z
            rows = lim_ref.shape[2]
            for grp in range(half // rows):
                lim_ref[h, grp] = lim[grp * rows:(grp + 1) * rows]
                a1_ref[h, grp] = a1[grp * rows:(grp + 1) * rows]

        marked = []
        for lt in range(n_tiles):
            def round12(r, c, lt=lt):
                w1, w2 = c
                m1 = jnp.max(w1, axis=0, keepdims=True)
                m2 = jnp.max(w2, axis=0, keepdims=True)
                sv1_tiles[lt, pl.ds(r, 1), :] = m1
                sv2_tiles[lt, pl.ds(r, 1), :] = m2
                mark = RANK_MARK * (1.0 + _as_f32(r) / kf)
                return jnp.where(w1 == m1, -jnp.inf, w1), jnp.where(w2 == m2, mark, w2)

            ls = slice(lt * 128, (lt + 1) * 128)
            marked.append(lax.fori_loop(0, k, round12, (s1[:, ls], s2[:, ls]))[1])
        w2 = jnp.concatenate(marked, axis=1)
        rank2 = jnp.where(w2 <= RANK_MARK, w2 * (kf / RANK_MARK) - kf, UNRANKED)

        sv1 = sorted1()
        cand, spans = _candidates(sv1, sorted2())

        def round_c(r, w):
            m = jnp.max(w, axis=0, keepdims=True)
            best_ref[pl.ds(r, 1), :] = m
            return jnp.where(w == m, -jnp.inf, w)

        lax.fori_loop(0, k, round_c, cand)
        taken = jnp.where(cand >= best_ref[k - 1:k, :], 1.0, 0.0)
        counts = _staircase_counts(taken, spans)
        lim = jnp.zeros_like(s1)
        for a in range(k):
            lim = jnp.where(s1 == sv1[a:a + 1], counts[a], lim)
        emit(rank2, lim)

        n1 = jnp.sum(jnp.where(s1 >= sv1[k - 1:k], 1.0, 0.0), axis=0, keepdims=True)
        n2 = jnp.sum(jnp.where(rank2 < kf, 1.0, 0.0), axis=0, keepdims=True)
        nc = jnp.sum(taken, axis=0, keepdims=True)
        tied = jnp.logical_or(jnp.max(jnp.maximum(jnp.maximum(n1, n2), nc)) > kf,
                              jnp.min(s2) <= RANK_MARK)

        @pl.when(tied)
        def _():
            tiles = [slice(lt * 128, (lt + 1) * 128) for lt in range(n_tiles)]
            rank1 = jnp.concatenate([_extract_ranked(s1[:, ls], sv1_tiles.at[lt], k)
                                     for lt, ls in enumerate(tiles)], axis=1)
            rank2 = jnp.concatenate([_extract_ranked(s2[:, ls], sv2_tiles.at[lt], k)
                                     for lt, ls in enumerate(tiles)], axis=1)
            cand, spans = _candidates(sorted1(), sorted2())
            taken = jnp.where(_extract_ranked(cand, best_ref, k) < kf, 1.0, 0.0)
            counts = _staircase_counts(taken, spans)
            lim = jnp.zeros_like(s1)
            for a in range(k):
                lim = jnp.where(rank1 == float(a), counts[a], lim)
            emit(rank2, lim)

        return carry

    lax.fori_loop(0, PEER_HEADS, head, 0)


def peer_select(h, g, shift, scale, wq_t, keys, *, layer, seq, rows, tm=512):
    t, d = h.shape
    nq = wq_t.shape[1]
    per_batch = seq // tm
    tab_spec = pl.BlockSpec((PEER_HEADS, N_SUBKEYS, tm), lambda i: (0, 0, i))
    tab = lambda dt: jax.ShapeDtypeStruct((PEER_HEADS, N_SUBKEYS, t), dt)
    grp_spec = pl.BlockSpec((PEER_HEADS, N_SUBKEYS // rows, rows, tm), lambda i: (0, 0, 0, i))
    grp = jax.ShapeDtypeStruct((PEER_HEADS, N_SUBKEYS // rows, rows, t), F32)
    return pl.pallas_call(
        _peer_select_kernel,
        grid=(t // tm,),
        in_specs=[
            pl.BlockSpec((tm, d), lambda i: (i, 0)),
            pl.BlockSpec((1, d), lambda i: (0, 0)),
            pl.BlockSpec((1, 1, d), lambda i: (i // per_batch, 0, 0)),
            pl.BlockSpec((1, 1, d), lambda i: (i // per_batch, 0, 0)),
            pl.BlockSpec((None, nq, d), lambda i: (layer, 0, 0)),
            pl.BlockSpec((None,) + keys.shape[1:], lambda i: (layer, 0, 0, 0, 0)),
        ],
        out_specs=[pl.BlockSpec((tm, d), lambda i: (i, 0)), tab_spec, tab_spec, grp_spec, grp_spec],
        out_shape=[jax.ShapeDtypeStruct((t, d), BF16), tab(BF16), tab(BF16), grp, grp],
        scratch_shapes=[pltpu.VMEM((nq, tm), BF16),
                        pltpu.VMEM((tm // 128, PEER_TOPK, 128), F32),
                        pltpu.VMEM((tm // 128, PEER_TOPK, 128), F32),
                        pltpu.VMEM((PEER_TOPK, tm), F32)],
        compiler_params=_params("arbitrary"),
        name="peer_select",
    )(h, g, shift, scale, wq_t, keys)


def _peer_mix_kernel(xf_ref, u_ref, vt_ref, r2_ref, a2_ref, lim_ref, a1_ref, h_ref, g_ref,
                     o_ref, acc_ref, hid_a_ref, hid_b_ref, p_ref, *, n_blocks):
    g = pl.program_id(0)
    eb, tc = hid_a_ref.shape
    rows_per_step = eb // N_SUBKEYS
    mix_block = jnp.maximum(g - 1, 0) % n_blocks

    @pl.when(mix_block == 0)
    def _():
        acc_ref[...] = jnp.zeros_like(acc_ref)

    @pl.when(g == 0)
    def _():
        hid_b_ref[...] = jnp.zeros_like(hid_b_ref)

    def step(cur_ref, prev_ref):
        cur_ref[...] = lax.dot_general(u_ref[...], xf_ref[...], NT_DIMS, preferred_element_type=F32)
        for kk in range(rows_per_step):
            gate = None
            for h in range(PEER_HEADS):
                lim = jnp.broadcast_to(lim_ref[h, 0, kk:kk + 1, :], (16, tc)).astype(BF16)
                a1 = jnp.broadcast_to(a1_ref[h, 0, kk:kk + 1, :], (16, tc)).astype(BF16)
                lim = jnp.concatenate([lim] * (N_SUBKEYS // 16), axis=0)
                a1 = jnp.concatenate([a1] * (N_SUBKEYS // 16), axis=0)
                w = jnp.where(r2_ref[h] < lim, a2_ref[h] * a1, jnp.zeros((), BF16))
                gate = w if gate is None else gate + w
            sl = slice(kk * N_SUBKEYS, (kk + 1) * N_SUBKEYS)
            hid = prev_ref[sl, :]
            act = 0.5 * hid * (1.0 + lax.erf(hid * INV_SQRT2))
            p_ref[sl, :] = act.astype(BF16) * gate
        acc_ref[...] += jnp.dot(vt_ref[...], p_ref[...], preferred_element_type=F32)

    @pl.when(g % 2 == 0)
    def _():
        step(hid_a_ref, hid_b_ref)

    @pl.when(g % 2 == 1)
    def _():
        step(hid_b_ref, hid_a_ref)

    @pl.when(jnp.logical_and(mix_block == n_blocks - 1, g > 0))
    def _():
        o_ref[...] = h_ref[...] + g_ref[0] * acc_ref[...].T


def peer_mix(xf, u, vt, r2, a2, lim, a1, h, gate, *, layer, seq, tc=1024):
    t, d = h.shape
    rows = lim.shape[2]
    eb = rows * N_SUBKEYS
    n_blocks = u.shape[1] // eb
    n_items = (t // tc) * n_blocks
    per_batch = seq // tc
    once = pl.Buffered(1)

    def hid_item(g):
        return jnp.minimum(g, n_items - 1)

    def mix_item(g):
        return jnp.maximum(g - 1, 0)

    tab_spec = pl.BlockSpec((PEER_HEADS, N_SUBKEYS, tc), lambda g: (0, 0, mix_item(g) // n_blocks),
                            pipeline_mode=once)
    row_spec = pl.BlockSpec((PEER_HEADS, 1, rows, tc),
                            lambda g: (0, mix_item(g) % n_blocks, 0, mix_item(g) // n_blocks))
    chunk_spec = pl.BlockSpec((tc, d), lambda g: (mix_item(g) // n_blocks, 0), pipeline_mode=once)
    return pl.pallas_call(
        functools.partial(_peer_mix_kernel, n_blocks=n_blocks),
        grid=(n_items + 1,),
        in_specs=[
            pl.BlockSpec((tc, d), lambda g: (hid_item(g) // n_blocks, 0), pipeline_mode=once),
            pl.BlockSpec((None, eb, d), lambda g: (layer, hid_item(g) % n_blocks, 0)),
            pl.BlockSpec((None, d, eb), lambda g: (layer, 0, mix_item(g) % n_blocks)),
            tab_spec, tab_spec, row_spec, row_spec,
            chunk_spec,
            pl.BlockSpec((1, 1, d), lambda g: (mix_item(g) // n_blocks // per_batch, 0, 0)),
        ],
        out_specs=chunk_spec,
        out_shape=jax.ShapeDtypeStruct((t, d), F32),
        scratch_shapes=[pltpu.VMEM((d, tc), F32),
                        pltpu.VMEM((eb, tc), F32),
                        pltpu.VMEM((eb, tc), F32),
                        pltpu.VMEM((eb, tc), BF16)],
        compiler_params=_params("arbitrary"),
        name="peer_mix",
    )(xf, u, vt, r2, a2, lim, a1, h, gate)


def kernel(x, c, rel_bias, norm1_g, norm2_g, w_ada, b_ada, a_w_in, a_q_gain, a_k_gain,
           b_w_in, w_out, peer_w_query, peer_sub_keys, peer_u, peer_v):
    b, s, d = x.shape
    depth = w_ada.shape[0]
    n_groups = len(DIL_GROUPS)
    heads = d // HEAD_DIM

    mod = adaln(c, w_ada, b_ada).reshape(depth, b, 6, 1, d)
    a_w, b_w, w_o = a_w_in.astype(BF16), b_w_in.astype(BF16), w_out.astype(BF16)
    wq_t = jnp.swapaxes(peer_w_query, 1, 2).astype(BF16)
    keys = peer_sub_keys.astype(BF16)
    u, vt = peer_u.astype(BF16), jnp.swapaxes(peer_v, 1, 2).astype(BF16)
    h = x.reshape(b * s, d)
    for i in range(depth):
        sh1, sc1, g1, sh2, sc2, g2 = (mod[i, :, k] for k in range(6))
        j = i // 2
        if i % 2 == 0:
            ones = jnp.ones((HEAD_DIM,), F32)
            gain = jnp.stack([jnp.tile(row, heads) for g in range(n_groups)
                              for row in (a_q_gain[j, g] / math.sqrt(HEAD_DIM), a_k_gain[j, g], ones)])
            qkvs = in_proj_dilated(h, norm1_g[i][None], sh1, sc1, a_w, gain[:, None, :], layer=j, seq=s)
            outs, lses = [], []
            for g, (window, dil) in enumerate(DIL_GROUPS):
                bias = _band_bias(rel_bias[:, g * heads:(g + 1) * heads], window, dil)
                o, lse = dilated_group_attention(qkvs[g], bias, n_heads=heads)
                outs.append(o)
                lses.append(lse)
            h = out_proj_merge(outs, lses, w_o, h, g1, layer=i, seq=s)
        else:
            qkv = in_proj(h, norm1_g[i][None], sh1, sc1, b_w, layer=j, seq=s)
            y = stick_breaking_attention(qkv.reshape(b, s, -1), n_heads=heads)
            h = out_proj(y.reshape(b * s, d), w_o, h, g1, layer=i, seq=s)
        xf, r2, a2, lim, a1 = peer_select(h, norm2_g[i][None], sh2, sc2, wq_t, keys, layer=i, seq=s,
                                          rows=PEER_BLOCK_ROWS)
        h = peer_mix(xf, u, vt, r2, a2, lim, a1, h, g2, layer=i, seq=s)
    return h.reshape(b, s, d)
```

```python
import functools
import math

import jax
import jax.numpy as jnp
from jax import lax
from jax.experimental import pallas as pl
from jax.experimental.pallas import tpu as pltpu

F32 = jnp.float32
BF16 = jnp.bfloat16

HEAD_DIM = 128
BLOCK = 128
DIL_GROUPS = ((128, 1), (512, 4), (2048, 16))
REL_BUCKETS = 32
REL_MAX_DIST = 2048
PEER_HEADS = 8
N_SUBKEYS = 128
PEER_TOPK = 16
PEER_BLOCK_ROWS = 8
EPS = 1e-6
NEG_BIG = -1e30
SB_DEAD_DECAY = 104.0
SB_HEADS_PER_STEP = 8
SB_Q_CHUNK = 1024
INV_SQRT2 = 0.7071067811865476

VMEM_LIMIT = 56 * 1024 * 1024

NT_DIMS = (((1,), (1,)), ((), ()))


def _params(*sem):
    return pltpu.CompilerParams(dimension_semantics=sem, vmem_limit_bytes=VMEM_LIMIT)


def _adaln_kernel(c_ref, w_lo_ref, w_hi_ref, b_ref, o_ref):
    c = c_ref[...]
    cs = c / (1.0 + jnp.exp(-c))
    half = w_lo_ref.shape[1]
    dot = functools.partial(jnp.dot, preferred_element_type=F32, precision=lax.Precision.HIGHEST)
    o_ref[0] = dot(cs[:, :half], w_lo_ref[0]) + dot(cs[:, half:], w_hi_ref[0]) + b_ref[0]


def adaln(c, w_ada, b_ada, *, tn=1024):
    depth, d, n = w_ada.shape
    b = c.shape[0]
    rows = 8
    c_pad = jnp.pad(c, ((0, rows - b), (0, 0)))
    out = pl.pallas_call(
        _adaln_kernel,
        grid=(depth, n // tn),
        in_specs=[
            pl.BlockSpec((rows, d), lambda i, j: (0, 0)),
            pl.BlockSpec((1, d // 2, tn), lambda i, j: (i, 0, j)),
            pl.BlockSpec((1, d // 2, tn), lambda i, j: (i, 1, j)),
            pl.BlockSpec((1, 1, tn), lambda i, j: (i, 0, j)),
        ],
        out_specs=pl.BlockSpec((1, rows, tn), lambda i, j: (i, 0, j)),
        out_shape=jax.ShapeDtypeStruct((depth, rows, n), F32),
        compiler_params=_params("arbitrary", "arbitrary"),
        name="adaln",
    )(c_pad, w_ada, w_ada, b_ada.reshape(depth, 1, n))
    return out[:, :b]


def _norm_modulate(x, g, shift, scale):
    ms = jnp.mean(x * x, axis=-1, keepdims=True)
    y = x * lax.rsqrt(ms + EPS) * g
    return y * (1.0 + scale) + shift


def _in_proj_kernel(h_ref, g_ref, sh_ref, sc_ref, w_ref, o_ref, xm_ref):
    @pl.when(pl.program_id(1) == 0)
    def _():
        xm_ref[...] = _norm_modulate(h_ref[...], g_ref[...], sh_ref[0], sc_ref[0]).astype(BF16)

    o_ref[...] = jnp.dot(xm_ref[...], w_ref[...], preferred_element_type=F32).astype(o_ref.dtype)


def in_proj(h, g, shift, scale, w, *, layer, seq, tm=1024, tn=2048):
    t, d = h.shape
    n = w.shape[2]
    per_batch = seq // tm
    return pl.pallas_call(
        _in_proj_kernel,
        grid=(t // tm, n // tn),
        in_specs=[
            pl.BlockSpec((tm, d), lambda i, j: (i, 0)),
            pl.BlockSpec((1, d), lambda i, j: (0, 0)),
            pl.BlockSpec((1, 1, d), lambda i, j: (i // per_batch, 0, 0)),
            pl.BlockSpec((1, 1, d), lambda i, j: (i // per_batch, 0, 0)),
            pl.BlockSpec((None, d, tn), lambda i, j: (layer, 0, j)),
        ],
        out_specs=pl.BlockSpec((tm, tn), lambda i, j: (i, j)),
        out_shape=jax.ShapeDtypeStruct((t, n), BF16),
        scratch_shapes=[pltpu.VMEM((tm, d), BF16)],
        compiler_params=_params("arbitrary", "arbitrary"),
        name="in_proj",
    )(h, g, shift, scale, w)


def _in_proj_dilated_kernel(h_ref, g_ref, sh_ref, sc_ref, w_ref, gain_ref, o0_ref, o1_ref, o2_ref,
                            xm0_ref, xm1_ref, xm2_ref, slab_ref, *, parts):
    j = pl.program_id(1)
    seg = j // parts
    tm, d = xm0_ref.shape
    xm_refs = (xm0_ref, xm1_ref, xm2_ref)

    @pl.when(j == 0)
    def _():
        xm = _norm_modulate(h_ref[...], g_ref[...], sh_ref[0], sc_ref[0])
        for sb in range(d // HEAD_DIM):
            slab_ref[sb] = xm[:, sb * HEAD_DIM:(sb + 1) * HEAD_DIM]
        for (_, dil), xm_ref in zip(DIL_GROUPS, xm_refs):
            if dil == 1:
                xm_ref[...] = xm.astype(BF16)
                continue
            rows = tm // dil
            for sb in range(d // HEAD_DIM):
                for r in range(dil):
                    xm_ref[r * rows:(r + 1) * rows, sb * HEAD_DIM:(sb + 1) * HEAD_DIM] = (
                        slab_ref[sb, pl.ds(r, rows, stride=dil), :].astype(BF16))

    def project(xm_ref, o_ref, dil):
        rows = tm // dil
        slab = min(2 * HEAD_DIM, w_ref.shape[1])
        n_slabs = w_ref.shape[1] // slab

        @pl.when(seg % 3 == 2)
        def _():
            for sb in range(n_slabs):
                cols = slice(sb * slab, (sb + 1) * slab)
                acc = jnp.dot(xm_ref[...], w_ref[:, cols], preferred_element_type=F32).astype(BF16)
                for r in range(dil):
                    o_ref[0, r, :, cols] = acc[r * rows:(r + 1) * rows]

        @pl.when(seg % 3 != 2)
        def _():
            for sb in range(n_slabs):
                acc = jnp.dot(xm_ref[...], w_ref[:, sb * slab:(sb + 1) * slab], preferred_element_type=F32)
                for hh in range(slab // HEAD_DIM):
                    sl = slice(sb * slab + hh * HEAD_DIM, sb * slab + (hh + 1) * HEAD_DIM)
                    a = acc[:, hh * HEAD_DIM:(hh + 1) * HEAD_DIM]
                    ms = jnp.mean(a * a, axis=-1, keepdims=True)
                    y = (a * lax.rsqrt(ms + EPS) * gain_ref[0, :, sl]).astype(BF16)
                    for r in range(dil):
                        o_ref[0, r, :, sl] = y[r * rows:(r + 1) * rows]

    for grp, ((_, dil), xm_ref, o_ref) in enumerate(zip(DIL_GROUPS, xm_refs, (o0_ref, o1_ref, o2_ref))):
        @pl.when(seg // 3 == grp)
        def _(dil=dil, xm_ref=xm_ref, o_ref=o_ref):
            project(xm_ref, o_ref, dil)


def in_proj_dilated(h, g, shift, scale, w, gain, *, layer, seq, tm=512, tn=2048):
    t, d = h.shape
    n = w.shape[2]
    b = t // seq
    per_batch = seq // tm
    n_groups = len(DIL_GROUPS)
    width = gain.shape[2]
    parts = width // tn
    assert n == n_groups * 3 * width

    def out_spec(grp, dil):
        def imap(i, j):
            return (i // per_batch, 0, i % per_batch, jnp.clip(j - 3 * parts * grp, 0, 3 * parts - 1))
        return pl.BlockSpec((1, dil, tm // dil, tn), imap)

    return pl.pallas_call(
        functools.partial(_in_proj_dilated_kernel, parts=parts),
        grid=(t // tm, n // tn),
        in_specs=[
            pl.BlockSpec((tm, d), lambda i, j: (i, 0)),
            pl.BlockSpec((1, d), lambda i, j: (0, 0)),
            pl.BlockSpec((1, 1, d), lambda i, j: (i // per_batch, 0, 0)),
            pl.BlockSpec((1, 1, d), lambda i, j: (i // per_batch, 0, 0)),
            pl.BlockSpec((None, d, tn), lambda i, j: (layer, 0, j)),
            pl.BlockSpec((1, 1, tn), lambda i, j: (j // parts, 0, j % parts)),
        ],
        out_specs=[out_spec(grp, dil) for grp, (_, dil) in enumerate(DIL_GROUPS)],
        out_shape=[jax.ShapeDtypeStruct((b, dil, seq // dil, 3 * width), BF16) for _, dil in DIL_GROUPS],
        scratch_shapes=[pltpu.VMEM((tm, d), BF16)] * 3 + [pltpu.VMEM((d // HEAD_DIM, tm, HEAD_DIM), F32)],
        compiler_params=_params("arbitrary", "arbitrary"),
        name="in_proj_dilated",
    )(h, g, shift, scale, w, gain)


def _dil_attn_kernel(q_ref, kp_ref, kc_ref, vp_ref, vc_ref, bias_ref, o_ref, lse_ref, *, n_heads):
    n = pl.program_id(2)
    n_sub = q_ref.shape[2] // BLOCK
    lane = lax.broadcasted_iota(jnp.int32, (1, 2 * BLOCK), 1)
    pen = jnp.where(jnp.logical_and(lane < BLOCK, n == 0), NEG_BIG, 0.0).astype(F32)
    ones = jnp.ones((2 * BLOCK, HEAD_DIM), BF16)
    lse_lanes = BLOCK // n_heads
    lane_head = lax.broadcasted_iota(jnp.int32, (1, BLOCK), 1) // lse_lanes
    for sb in range(n_sub):
        rows = slice(sb * BLOCK, (sb + 1) * BLOCK)
        prev = slice((sb - 1) * BLOCK, sb * BLOCK)
        lse_pack = jnp.zeros((BLOCK, BLOCK), F32)
        for hh in range(n_heads):
            sl = slice(hh * HEAD_DIM, (hh + 1) * HEAD_DIM)
            q = q_ref[0, 0, rows, sl]
            k_prev = kp_ref[0, 0, :, sl] if sb == 0 else kc_ref[0, 0, prev, sl]
            v_prev = vp_ref[0, 0, :, sl] if sb == 0 else vc_ref[0, 0, prev, sl]
            k = jnp.concatenate([k_prev, kc_ref[0, 0, rows, sl]], axis=0)
            v = jnp.concatenate([v_prev, vc_ref[0, 0, rows, sl]], axis=0)
            logits = lax.dot_general(q, k, NT_DIMS, preferred_element_type=F32)
            logits = logits + bias_ref[hh]
            if sb == 0:
                logits = logits + pen
            m = jnp.max(logits, axis=-1, keepdims=True)
            p = jnp.exp(logits - m).astype(BF16)
            v_ext = jnp.concatenate([v, ones], axis=1)
            pv = jnp.dot(p, v_ext, preferred_element_type=F32)
            denom = pv[:, HEAD_DIM:]
            o_ref[0, 0, rows, sl] = (pv[:, :HEAD_DIM] / denom).astype(o_ref.dtype)
            lse_pack = jnp.where(lane_head == hh, m + jnp.log(denom), lse_pack)
        lse_ref[0, 0, rows, :] = lse_pack


def dilated_group_attention(qkv, bias, *, n_heads, q_blocks=2):
    b, dil, sub_len, c = qkv.shape
    assert BLOCK % n_heads == 0
    width = n_heads * HEAD_DIM
    q_blocks = min(q_blocks, sub_len // BLOCK)
    rows = q_blocks * BLOCK
    steps = sub_len // rows

    def cur(which):
        return pl.BlockSpec((1, 1, rows, width), lambda bi, r, n: (bi, r, n, which))

    def prev(which):
        return pl.BlockSpec((1, 1, BLOCK, width),
                            lambda bi, r, n: (bi, r, jnp.maximum(q_blocks * n - 1, 0), which))

    return pl.pallas_call(
        functools.partial(_dil_attn_kernel, n_heads=n_heads),
        grid=(b, dil, steps),
        in_specs=[cur(0), prev(1), cur(1), prev(2), cur(2),
                  pl.BlockSpec((n_heads, BLOCK, 2 * BLOCK), lambda bi, r, n: (0, 0, 0))],
        out_specs=[pl.BlockSpec((1, 1, rows, width), lambda bi, r, n: (bi, r, n, 0)),
                   pl.BlockSpec((1, 1, rows, BLOCK), lambda bi, r, n: (bi, r, n, 0))],
        out_shape=[jax.ShapeDtypeStruct((b, dil, sub_len, width), BF16),
                   jax.ShapeDtypeStruct((b, dil, sub_len, BLOCK), F32)],
        compiler_params=_params("arbitrary", "arbitrary", "arbitrary"),
        name=f"dilated_attn_d{dil}",
    )(qkv, qkv, qkv, qkv, qkv, bias)


def _rel_bucket(dist):
    exact = REL_BUCKETS // 2
    d_f = jnp.maximum(dist, exact).astype(F32)
    log_b = exact + (jnp.log(d_f / exact) / math.log(REL_MAX_DIST / exact)
                     * (REL_BUCKETS - exact)).astype(jnp.int32)
    return jnp.where(dist < exact, dist, jnp.minimum(log_b, REL_BUCKETS - 1))


def _band_bias(table_g, window, dil):
    w_steps = window // dil
    period = 3 * BLOCK
    per_delta = table_g[_rel_bucket(jnp.arange(w_steps + 1) * dil)].astype(F32)
    row = jnp.full((period, table_g.shape[1]), NEG_BIG, F32)
    row = row.at[BLOCK - w_steps:BLOCK + 1].set(per_delta[::-1])
    flat = jnp.tile(row.T, (1, BLOCK))[:, :BLOCK * (period - 1)]
    return flat.reshape(-1, BLOCK, period - 1)[:, :, :2 * BLOCK]


def _out_proj_merge_kernel(o0_ref, o1_ref, o2_ref, l0_ref, l1_ref, l2_ref, w_ref, h_ref, g_ref, out_ref,
                           os1_ref, os2_ref, ls1_ref, ls2_ref, y_ref):
    tm = out_ref.shape[0]
    n_heads = y_ref.shape[1] // HEAD_DIM

    def to_token_order(src_ref, dst_ref, sl):
        dil = src_ref.shape[1]
        for r in range(dil):
            dst_ref[pl.ds(r, tm // dil, stride=dil), :] = src_ref[0, r, :, sl].astype(F32)

    to_token_order(l1_ref, ls1_ref, slice(None))
    to_token_order(l2_ref, ls2_ref, slice(None))
    l0, l1, l2 = l0_ref[0, 0], ls1_ref[...], ls2_ref[...]
    m = jnp.maximum(jnp.maximum(l0, l1), l2)
    e0, e1, e2 = jnp.exp(l0 - m), jnp.exp(l1 - m), jnp.exp(l2 - m)
    inv = 1.0 / (e0 + e1 + e2)
    a0, a1, a2 = e0 * inv, e1 * inv, e2 * inv
    for hh in range(n_heads):
        sl = slice(hh * HEAD_DIM, (hh + 1) * HEAD_DIM)
        col = slice(hh * (BLOCK // n_heads), hh * (BLOCK // n_heads) + 1)
        to_token_order(o1_ref, os1_ref, sl)
        to_token_order(o2_ref, os2_ref, sl)
        y = (a0[:, col] * o0_ref[0, 0, :, sl].astype(F32) + a1[:, col] * os1_ref[...]
             + a2[:, col] * os2_ref[...])
        y_ref[:, sl] = y.astype(BF16)
    proj = jnp.dot(y_ref[...], w_ref[...], preferred_element_type=F32)
    out_ref[...] = h_ref[...] + g_ref[0] * proj


def out_proj_merge(outs, lses, w, h, gate, *, layer, seq, tm=256):
    t, d = h.shape
    width = w.shape[1]
    per_batch = seq // tm
    row = pl.BlockSpec((tm, d), lambda i: (i, 0))

    def grp_spec(a):
        dil, cols = a.shape[1], a.shape[3]
        return pl.BlockSpec((1, dil, tm // dil, cols), lambda i: (i // per_batch, 0, i % per_batch, 0))

    return pl.pallas_call(
        _out_proj_merge_kernel,
        grid=(t // tm,),
        in_specs=[grp_spec(a) for a in (*outs, *lses)] + [
            pl.BlockSpec((None, width, d), lambda i: (layer, 0, 0)),
            row,
            pl.BlockSpec((1, 1, d), lambda i: (i // per_batch, 0, 0)),
        ],
        out_specs=row,
        out_shape=jax.ShapeDtypeStruct((t, d), F32),
        scratch_shapes=[pltpu.VMEM((tm, HEAD_DIM), F32)] * 4 + [pltpu.VMEM((tm, width), BF16)],
        compiler_params=_params("arbitrary"),
        name="out_proj_merge",
    )(*outs, *lses, w, h, gate)


def _out_proj_kernel(y_ref, w_ref, h_ref, g_ref, out_ref):
    proj = jnp.dot(y_ref[...], w_ref[...], preferred_element_type=F32)
    out_ref[...] = h_ref[...] + g_ref[0] * proj


def out_proj(y, w, h, gate, *, layer, seq, tm=512):
    t, d = h.shape
    width = w.shape[1]
    per_batch = seq // tm
    row = lambda cols: pl.BlockSpec((tm, cols), lambda i: (i, 0))
    return pl.pallas_call(
        _out_proj_kernel,
        grid=(t // tm,),
        in_specs=[row(width), pl.BlockSpec((None, width, d), lambda i: (layer, 0, 0)), row(d),
                  pl.BlockSpec((1, 1, d), lambda i: (i // per_batch, 0, 0))],
        out_specs=row(d),
        out_shape=jax.ShapeDtypeStruct((t, d), F32),
        compiler_params=_params("arbitrary"),
        name="out_proj",
    )(y, w, h, gate)


def _split2(x):
    hi = x.astype(BF16)
    lo = (x - hi.astype(F32)).astype(BF16)
    return hi, lo


def _sb_kernel(q_ref, k_ref, v_ref, o_ref, acc_ref, decay_ref, *, scale):
    nq = q_ref.shape[1] // BLOCK
    q_base = pl.program_id(2) * q_ref.shape[1]
    n_heads = acc_ref.shape[0]
    row = lax.broadcasted_iota(jnp.int32, (BLOCK, BLOCK), 0)
    col = lax.broadcasted_iota(jnp.int32, (BLOCK, BLOCK), 1)
    before = col < row
    tri = jnp.where(row > col, 1.0, 0.0).astype(BF16)
    tri_ext = jnp.concatenate([tri, jnp.ones((BLOCK, BLOCK), BF16)], axis=1)

    def tails_and_sums(sps):
        parts = [t for sp in sps for t in _split2(sp)]
        r = jnp.dot(jnp.concatenate(parts, axis=0), tri_ext, preferred_element_type=F32)
        out = []
        for g in range(len(sps)):
            hi = r[(2 * g) * BLOCK:(2 * g + 1) * BLOCK]
            lo = r[(2 * g + 1) * BLOCK:(2 * g + 2) * BLOCK]
            both = hi + lo
            out.append((both[:, :BLOCK], both[:, BLOCK:]))
        return out

    def softplus(z):
        return jnp.maximum(z, 0.0) + jnp.log1p(jnp.exp(-jnp.abs(z)))

    def logits(q0, k0, g):
        cs = slice(g * HEAD_DIM, (g + 1) * HEAD_DIM)
        return lax.dot_general(q_ref[0, pl.ds(q0, BLOCK), cs], k_ref[0, pl.ds(k0, BLOCK), cs],
                               NT_DIMS, preferred_element_type=F32) * scale

    def values(k0, g):
        return v_ref[0, pl.ds(k0, BLOCK), g * HEAD_DIM:(g + 1) * HEAD_DIM]

    def qblock(i, carry):
        q0 = pl.multiple_of(i * BLOCK, BLOCK)
        d0 = pl.multiple_of(q_base + q0, BLOCK)
        zs = [logits(q0, d0, g) for g in range(n_heads)]
        sps = [softplus(z) for z in zs]
        ts = tails_and_sums([jnp.where(before, sp, 0.0) for sp in sps])
        live = None
        for g in range(n_heads):
            tail, rsum = ts[g]
            a = jnp.where(before, jnp.exp(zs[g] - sps[g] - tail), 0.0)
            acc_ref[g] = jnp.dot(a.astype(BF16), values(d0, g), preferred_element_type=F32)
            decay_ref[g] = rsum
            lo = jnp.min(rsum)
            live = lo if live is None else jnp.minimum(live, lo)

        def cond(c):
            j, live = c
            return jnp.logical_and(j >= 0, live < SB_DEAD_DECAY)

        def body(c):
            j, _ = c
            k0 = pl.multiple_of(j * BLOCK, BLOCK)
            zs = [logits(q0, k0, g) for g in range(n_heads)]
            sps = [softplus(z) for z in zs]
            ts = tails_and_sums(sps)
            live = None
            for g in range(n_heads):
                tail, rsum = ts[g]
                decay = decay_ref[g]
                a = jnp.exp(zs[g] - sps[g] - (decay + tail))
                acc_ref[g] += jnp.dot(a.astype(BF16), values(k0, g), preferred_element_type=F32)
                decay = decay + rsum
                decay_ref[g] = decay
                lo = jnp.min(decay)
                live = lo if live is None else jnp.minimum(live, lo)
            return j - 1, live

        lax.while_loop(cond, body, (d0 // BLOCK - 1, live))
        for g in range(n_heads):
            o_ref[0, pl.ds(q0, BLOCK), g * HEAD_DIM:(g + 1) * HEAD_DIM] = acc_ref[g].astype(o_ref.dtype)
        return carry

    lax.fori_loop(0, nq, qblock, 0)


def stick_breaking_attention(qkv, *, n_heads):
    b, s, _ = qkv.shape
    hps = min(SB_HEADS_PER_STEP, n_heads)
    steps = n_heads // hps
    width = hps * HEAD_DIM
    qc = min(SB_Q_CHUNK, s)

    def spec(which):
        return pl.BlockSpec((1, s, width), lambda bi, h, c: (bi, 0, which * steps + h))

    return pl.pallas_call(
        functools.partial(_sb_kernel, scale=1.0 / math.sqrt(HEAD_DIM)),
        grid=(b, steps, s // qc),
        in_specs=[pl.BlockSpec((1, qc, width), lambda bi, h, c: (bi, c, h)), spec(1), spec(2)],
        out_specs=pl.BlockSpec((1, qc, width), lambda bi, h, c: (bi, c, h)),
        out_shape=jax.ShapeDtypeStruct((b, s, n_heads * HEAD_DIM), BF16),
        scratch_shapes=[pltpu.VMEM((hps, BLOCK, HEAD_DIM), F32),
                        pltpu.VMEM((hps, BLOCK, BLOCK), F32)],
        compiler_params=_params("arbitrary", "arbitrary", "arbitrary"),
        name="stick_breaking",
    )(qkv, qkv, qkv)


UNRANKED = 127.0
RANK_MARK = -(2.0 ** 100)


def _as_f32(r):
    return jnp.asarray(r).astype(F32)


def _extract_ranked(s, dst_ref, count):
    n = s.shape[0]
    idx = lax.broadcasted_iota(jnp.int32, s.shape, 0).astype(F32)

    def body(r, carry):
        s, rank = carry
        m = jnp.max(s, axis=0, keepdims=True)
        first = jnp.min(jnp.where(s == m, idx, float(n)), axis=0, keepdims=True)
        pick = idx == first
        dst_ref[pl.ds(r, 1), :] = m
        return jnp.where(pick, -jnp.inf, s), jnp.where(pick, _as_f32(r), rank)

    _, rank = lax.fori_loop(0, count, body, (s, jnp.full(s.shape, UNRANKED, F32)))
    return rank


def _candidates(sv1, sv2):
    k = PEER_TOPK
    groups = [sv1[0:1] + sv2]
    groups += [sv1[a:a + 1] + sv2[0:k // 2] for a in range(1, k // 2)]
    groups += [sv1[k // 2:] + sv2[0:1]]
    spans = [(0, k)] + [(k + (a - 1) * (k // 2), k // 2) for a in range(1, k // 2)]
    tail0 = k + (k // 2 - 1) * (k // 2)
    spans += [(tail0 + a, 1) for a in range(k // 2)]
    return jnp.concatenate(groups, axis=0), spans


def _staircase_counts(taken, spans):
    return [jnp.sum(taken[lo:lo + n], axis=0, keepdims=True) for lo, n in spans]


def _peer_select_kernel(h_ref, g_ref, sh_ref, sc_ref, wq_ref, keys_ref,
                        xf_ref, r2_ref, a2_ref, lim_ref, a1_ref,
                        qt_ref, sv1_tiles, sv2_tiles, best_ref):
    xf = _norm_modulate(h_ref[...], g_ref[...], sh_ref[0], sc_ref[0]).astype(BF16)
    xf_ref[...] = xf
    qt_ref[...] = lax.dot_general(wq_ref[...], xf, NT_DIMS, preferred_element_type=F32).astype(BF16)
    k = PEER_TOPK
    half = N_SUBKEYS
    kf = float(k)
    n_tiles = sv2_tiles.shape[0]

    def head(h, carry):
        r0 = pl.multiple_of(h * 2 * half, 2 * half)
        s1 = jnp.dot(keys_ref[h, 0], qt_ref[pl.ds(r0, half), :], preferred_element_type=F32)
        s2 = jnp.dot(keys_ref[h, 1], qt_ref[pl.ds(r0 + half, half), :], preferred_element_type=F32)

        def sorted1():
            return jnp.concatenate([sv1_tiles[lt] for lt in range(n_tiles)], axis=1)

        def sorted2():
            return jnp.concatenate([sv2_tiles[lt] for lt in range(n_tiles)], axis=1)

        def emit(rank2, lim):
            sv1, sv2, best = sorted1(), sorted2(), best_ref[...]
            z = jnp.sum(jnp.exp(best - best[0:1]), axis=0, keepdims=True)
            r2_ref[h] = rank2.astype(BF16)
            a2_ref[h] = jnp.exp(s2 - sv2[0:1]).astype(BF16)
            a1 = jnp.exp(s1 - sv1[0:1]) / z
            rows = lim_ref.shape[2]
            for grp in range(half // rows):
                lim_ref[h, grp] = lim[grp * rows:(grp + 1) * rows]
                a1_ref[h, grp] = a1[grp * rows:(grp + 1) * rows]

        marked = []
        for lt in range(n_tiles):
            def round12(r, c, lt=lt):
                w1, w2 = c
                m1 = jnp.max(w1, axis=0, keepdims=True)
                m2 = jnp.max(w2, axis=0, keepdims=True)
                sv1_tiles[lt, pl.ds(r, 1), :] = m1
                sv2_tiles[lt, pl.ds(r, 1), :] = m2
                mark = RANK_MARK * (1.0 + _as_f32(r) / kf)
                return jnp.where(w1 == m1, -jnp.inf, w1), jnp.where(w2 == m2, mark, w2)

            ls = slice(lt * 128, (lt + 1) * 128)
            marked.append(lax.fori_loop(0, k, round12, (s1[:, ls], s2[:, ls]))[1])
        w2 = jnp.concatenate(marked, axis=1)
        rank2 = jnp.where(w2 <= RANK_MARK, w2 * (kf / RANK_MARK) - kf, UNRANKED)

        sv1 = sorted1()
        cand, spans = _candidates(sv1, sorted2())

        def round_c(r, w):
            m = jnp.max(w, axis=0, keepdims=True)
            best_ref[pl.ds(r, 1), :] = m
            return jnp.where(w == m, -jnp.inf, w)

        lax.fori_loop(0, k, round_c, cand)
        taken = jnp.where(cand >= best_ref[k - 1:k, :], 1.0, 0.0)
        counts = _staircase_counts(taken, spans)
        lim = jnp.zeros_like(s1)
        for a in range(k):
            lim = jnp.where(s1 == sv1[a:a + 1], counts[a], lim)
        emit(rank2, lim)

        n1 = jnp.sum(jnp.where(s1 >= sv1[k - 1:k], 1.0, 0.0), axis=0, keepdims=True)
        n2 = jnp.sum(jnp.where(rank2 < kf, 1.0, 0.0), axis=0, keepdims=True)
        nc = jnp.sum(taken, axis=0, keepdims=True)
        tied = jnp.logical_or(jnp.max(jnp.maximum(jnp.maximum(n1, n2), nc)) > kf,
                              jnp.min(s2) <= RANK_MARK)

        @pl.when(tied)
        def _():
            tiles = [slice(lt * 128, (lt + 1) * 128) for lt in range(n_tiles)]
            rank1 = jnp.concatenate([_extract_ranked(s1[:, ls], sv1_tiles.at[lt], k)
                                     for lt, ls in enumerate(tiles)], axis=1)
            rank2 = jnp.concatenate([_extract_ranked(s2[:, ls], sv2_tiles.at[lt], k)
                                     for lt, ls in enumerate(tiles)], axis=1)
            cand, spans = _candidates(sorted1(), sorted2())
            taken = jnp.where(_extract_ranked(cand, best_ref, k) < kf, 1.0, 0.0)
            counts = _staircase_counts(taken, spans)
            lim = jnp.zeros_like(s1)
            for a in range(k):
                lim = jnp.where(rank1 == float(a), counts[a], lim)
            emit(rank2, lim)

        return carry

    lax.fori_loop(0, PEER_HEADS, head, 0)


def peer_select(h, g, shift, scale, wq_t, keys, *, layer, seq, rows, tm=512):
    t, d = h.shape
    nq = wq_t.shape[1]
    per_batch = seq // tm
    tab_spec = pl.BlockSpec((PEER_HEADS, N_SUBKEYS, tm), lambda i: (0, 0, i))
    tab = lambda dt: jax.ShapeDtypeStruct((PEER_HEADS, N_SUBKEYS, t), dt)
    grp_spec = pl.BlockSpec((PEER_HEADS, N_SUBKEYS // rows, rows, tm), lambda i: (0, 0, 0, i))
    grp = jax.ShapeDtypeStruct((PEER_HEADS, N_SUBKEYS // rows, rows, t), F32)
    return pl.pallas_call(
        _peer_select_kernel,
        grid=(t // tm,),
        in_specs=[
            pl.BlockSpec((tm, d), lambda i: (i, 0)),
            pl.BlockSpec((1, d), lambda i: (0, 0)),
            pl.BlockSpec((1, 1, d), lambda i: (i // per_batch, 0, 0)),
            pl.BlockSpec((1, 1, d), lambda i: (i // per_batch, 0, 0)),
            pl.BlockSpec((None, nq, d), lambda i: (layer, 0, 0)),
            pl.BlockSpec((None,) + keys.shape[1:], lambda i: (layer, 0, 0, 0, 0)),
        ],
        out_specs=[pl.BlockSpec((tm, d), lambda i: (i, 0)), tab_spec, tab_spec, grp_spec, grp_spec],
        out_shape=[jax.ShapeDtypeStruct((t, d), BF16), tab(BF16), tab(BF16), grp, grp],
        scratch_shapes=[pltpu.VMEM((nq, tm), BF16),
                        pltpu.VMEM((tm // 128, PEER_TOPK, 128), F32),
                        pltpu.VMEM((tm // 128, PEER_TOPK, 128), F32),
                        pltpu.VMEM((PEER_TOPK, tm), F32)],
        compiler_params=_params("arbitrary"),
        name="peer_select",
    )(h, g, shift, scale, wq_t, keys)


def _peer_mix_kernel(xf_ref, u_ref, vt_ref, r2_ref, a2_ref, lim_ref, a1_ref, h_ref, g_ref,
                     o_ref, acc_ref, hid_a_ref, hid_b_ref, p_ref, *, n_blocks):
    g = pl.program_id(0)
    eb, tc = hid_a_ref.shape
    rows_per_step = eb // N_SUBKEYS
    mix_block = jnp.maximum(g - 1, 0) % n_blocks

    @pl.when(mix_block == 0)
    def _():
        acc_ref[...] = jnp.zeros_like(acc_ref)

    @pl.when(g == 0)
    def _():
        hid_b_ref[...] = jnp.zeros_like(hid_b_ref)

    def step(cur_ref, prev_ref):
        cur_ref[...] = lax.dot_general(u_ref[...], xf_ref[...], NT_DIMS, preferred_element_type=F32)
        for kk in range(rows_per_step):
            gate = None
            for h in range(PEER_HEADS):
                lim = jnp.broadcast_to(lim_ref[h, 0, kk:kk + 1, :], (16, tc)).astype(BF16)
                a1 = jnp.broadcast_to(a1_ref[h, 0, kk:kk + 1, :], (16, tc)).astype(BF16)
                lim = jnp.concatenate([lim] * (N_SUBKEYS // 16), axis=0)
                a1 = jnp.concatenate([a1] * (N_SUBKEYS // 16), axis=0)
                w = jnp.where(r2_ref[h] < lim, a2_ref[h] * a1, jnp.zeros((), BF16))
                gate = w if gate is None else gate + w
            sl = slice(kk * N_SUBKEYS, (kk + 1) * N_SUBKEYS)
            hid = prev_ref[sl, :]
            act = 0.5 * hid * (1.0 + lax.erf(hid * INV_SQRT2))
            p_ref[sl, :] = act.astype(BF16) * gate
        acc_ref[...] += jnp.dot(vt_ref[...], p_ref[...], preferred_element_type=F32)

    @pl.when(g % 2 == 0)
    def _():
        step(hid_a_ref, hid_b_ref)

    @pl.when(g % 2 == 1)
    def _():
        step(hid_b_ref, hid_a_ref)

    @pl.when(jnp.logical_and(mix_block == n_blocks - 1, g > 0))
    def _():
        o_ref[...] = h_ref[...] + g_ref[0] * acc_ref[...].T


def peer_mix(xf, u, vt, r2, a2, lim, a1, h, gate, *, layer, seq, tc=512):
    t, d = h.shape
    rows = lim.shape[2]
    eb = rows * N_SUBKEYS
    n_blocks = u.shape[1] // eb
    n_items = (t // tc) * n_blocks
    per_batch = seq // tc

    def hid_item(g):
        return jnp.minimum(g, n_items - 1)

    def mix_item(g):
        return jnp.maximum(g - 1, 0)

    tab_spec = pl.BlockSpec((PEER_HEADS, N_SUBKEYS, tc), lambda g: (0, 0, mix_item(g) // n_blocks))
    row_spec = pl.BlockSpec((PEER_HEADS, 1, rows, tc),
                            lambda g: (0, mix_item(g) % n_blocks, 0, mix_item(g) // n_blocks))
    chunk_spec = pl.BlockSpec((tc, d), lambda g: (mix_item(g) // n_blocks, 0))
    return pl.pallas_call(
        functools.partial(_peer_mix_kernel, n_blocks=n_blocks),
        grid=(n_items + 1,),
        in_specs=[
            pl.BlockSpec((tc, d), lambda g: (hid_item(g) // n_blocks, 0)),
            pl.BlockSpec((None, eb, d), lambda g: (layer, hid_item(g) % n_blocks, 0)),
            pl.BlockSpec((None, d, eb), lambda g: (layer, 0, mix_item(g) % n_blocks)),
            tab_spec, tab_spec, row_spec, row_spec,
            chunk_spec,
            pl.BlockSpec((1, 1, d), lambda g: (mix_item(g) // n_blocks // per_batch, 0, 0)),
        ],
        out_specs=chunk_spec,
        out_shape=jax.ShapeDtypeStruct((t, d), F32),
        scratch_shapes=[pltpu.VMEM((d, tc), F32),
                        pltpu.VMEM((eb, tc), F32),
                        pltpu.VMEM((eb, tc), F32),
                        pltpu.VMEM((eb, tc), BF16)],
        compiler_params=_params("arbitrary"),
        name="peer_mix",
    )(xf, u, vt, r2, a2, lim, a1, h, gate)


def kernel(x, c, rel_bias, norm1_g, norm2_g, w_ada, b_ada, a_w_in, a_q_gain, a_k_gain,
           b_w_in, w_out, peer_w_query, peer_sub_keys, peer_u, peer_v):
    b, s, d = x.shape
    depth = w_ada.shape[0]
    n_groups = len(DIL_GROUPS)
    heads = d // HEAD_DIM

    mod = adaln(c, w_ada, b_ada).reshape(depth, b, 6, 1, d)
    a_w, b_w, w_o = a_w_in.astype(BF16), b_w_in.astype(BF16), w_out.astype(BF16)
    wq_t = jnp.swapaxes(peer_w_query, 1, 2).astype(BF16)
    keys = peer_sub_keys.astype(BF16)
    u, vt = peer_u.astype(BF16), jnp.swapaxes(peer_v, 1, 2).astype(BF16)
    h = x.reshape(b * s, d)
    for i in range(depth):
        sh1, sc1, g1, sh2, sc2, g2 = (mod[i, :, k] for k in range(6))
        j = i // 2
        if i % 2 == 0:
            ones = jnp.ones((HEAD_DIM,), F32)
            gain = jnp.stack([jnp.tile(row, heads) for g in range(n_groups)
                              for row in (a_q_gain[j, g] / math.sqrt(HEAD_DIM), a_k_gain[j, g], ones)])
            qkvs = in_proj_dilated(h, norm1_g[i][None], sh1, sc1, a_w, gain[:, None, :], layer=j, seq=s)
            outs, lses = [], []
            for g, (window, dil) in enumerate(DIL_GROUPS):
                bias = _band_bias(rel_bias[:, g * heads:(g + 1) * heads], window, dil)
                o, lse = dilated_group_attention(qkvs[g], bias, n_heads=heads)
                outs.append(o)
                lses.append(lse)
            h = out_proj_merge(outs, lses, w_o, h, g1, layer=i, seq=s)
        else:
            qkv = in_proj(h, norm1_g[i][None], sh1, sc1, b_w, layer=j, seq=s)
            y = stick_breaking_attention(qkv.reshape(b, s, -1), n_heads=heads)
            h = out_proj(y.reshape(b * s, d), w_o, h, g1, layer=i, seq=s)
        xf, r2, a2, lim, a1 = peer_select(h, norm2_g[i][None], sh2, sc2, wq_t, keys, layer=i, seq=s,
                                          rows=PEER_BLOCK_ROWS)
        h = peer_mix(xf, u, vt, r2, a2, lim, a1, h, g2, layer=i, seq=s)
    return h.reshape(b, s, d)
```

```python
import functools
import math

import jax
import jax.numpy as jnp
from jax import lax
from jax.experimental import pallas as pl
from jax.experimental.pallas import tpu as pltpu

F32 = jnp.float32
BF16 = jnp.bfloat16

HEAD_DIM = 128
BLOCK = 128
DIL_GROUPS = ((128, 1), (512, 4), (2048, 16))
REL_BUCKETS = 32
REL_MAX_DIST = 2048
PEER_HEADS = 8
N_SUBKEYS = 128
PEER_TOPK = 16
PEER_BLOCK_ROWS = 8
EPS = 1e-6
NEG_BIG = -1e30
SB_DEAD_DECAY = 104.0
SB_HEADS_PER_STEP = 8
SB_Q_CHUNK = 1024
INV_SQRT2 = 0.7071067811865476

VMEM_LIMIT = 60 * 1024 * 1024

NT_DIMS = (((1,), (1,)), ((), ()))


def _params(*sem):
    return pltpu.CompilerParams(dimension_semantics=sem, vmem_limit_bytes=VMEM_LIMIT)


def _adaln_kernel(c_ref, w_lo_ref, w_hi_ref, b_ref, o_ref):
    c = c_ref[...]
    cs = c / (1.0 + jnp.exp(-c))
    half = w_lo_ref.shape[1]
    dot = functools.partial(jnp.dot, preferred_element_type=F32, precision=lax.Precision.HIGHEST)
    o_ref[0] = dot(cs[:, :half], w_lo_ref[0]) + dot(cs[:, half:], w_hi_ref[0]) + b_ref[0]


def adaln(c, w_ada, b_ada, *, tn=1024):
    depth, d, n = w_ada.shape
    b = c.shape[0]
    rows = 8
    c_pad = jnp.pad(c, ((0, rows - b), (0, 0)))
    out = pl.pallas_call(
        _adaln_kernel,
        grid=(depth, n // tn),
        in_specs=[
            pl.BlockSpec((rows, d), lambda i, j: (0, 0)),
            pl.BlockSpec((1, d // 2, tn), lambda i, j: (i, 0, j)),
            pl.BlockSpec((1, d // 2, tn), lambda i, j: (i, 1, j)),
            pl.BlockSpec((1, 1, tn), lambda i, j: (i, 0, j)),
        ],
        out_specs=pl.BlockSpec((1, rows, tn), lambda i, j: (i, 0, j)),
        out_shape=jax.ShapeDtypeStruct((depth, rows, n), F32),
        compiler_params=_params("arbitrary", "arbitrary"),
        name="adaln",
    )(c_pad, w_ada, w_ada, b_ada.reshape(depth, 1, n))
    return out[:, :b]


CONVERT_ROWS = 128


def _convert_expert_slice(cu_ref, cv_ref, ou_ref, ovt_ref):
    ou_ref[...] = cu_ref[...].astype(BF16)
    ovt_ref[...] = cv_ref[...].T.astype(BF16)


def _convert_specs(peer_u, peer_v, layer, step_of):
    n_exp, d = peer_u.shape[1:]
    last = n_exp // CONVERT_ROWS - 1

    def slice_of(*idx):
        return jnp.minimum(step_of(*idx), last)

    src = pl.BlockSpec((None, CONVERT_ROWS, d), lambda *idx: (layer, slice_of(*idx), 0))
    in_specs = [src, src]
    out_specs = [pl.BlockSpec((CONVERT_ROWS, d), lambda *idx: (slice_of(*idx), 0)),
                 pl.BlockSpec((d, CONVERT_ROWS), lambda *idx: (0, slice_of(*idx)))]
    out_shape = [jax.ShapeDtypeStruct((n_exp, d), BF16), jax.ShapeDtypeStruct((d, n_exp), BF16)]
    return in_specs, out_specs, out_shape


def _norm_modulate(x, g, shift, scale):
    ms = jnp.mean(x * x, axis=-1, keepdims=True)
    y = x * lax.rsqrt(ms + EPS) * g
    return y * (1.0 + scale) + shift


def _in_proj_kernel(h_ref, g_ref, sh_ref, sc_ref, w_ref, o_ref, xm_ref):
    @pl.when(pl.program_id(1) == 0)
    def _():
        xm_ref[...] = _norm_modulate(h_ref[...], g_ref[...], sh_ref[0], sc_ref[0]).astype(BF16)

    o_ref[...] = jnp.dot(xm_ref[...], w_ref[...], preferred_element_type=F32).astype(o_ref.dtype)


def in_proj(h, g, shift, scale, w, *, layer, seq, tm=1024, tn=2048):
    t, d = h.shape
    n = w.shape[2]
    per_batch = seq // tm
    return pl.pallas_call(
        _in_proj_kernel,
        grid=(t // tm, n // tn),
        in_specs=[
            pl.BlockSpec((tm, d), lambda i, j: (i, 0)),
            pl.BlockSpec((1, d), lambda i, j: (0, 0)),
            pl.BlockSpec((1, 1, d), lambda i, j: (i // per_batch, 0, 0)),
            pl.BlockSpec((1, 1, d), lambda i, j: (i // per_batch, 0, 0)),
            pl.BlockSpec((None, d, tn), lambda i, j: (layer, 0, j)),
        ],
        out_specs=pl.BlockSpec((tm, tn), lambda i, j: (i, j)),
        out_shape=jax.ShapeDtypeStruct((t, n), BF16),
        scratch_shapes=[pltpu.VMEM((tm, d), BF16)],
        compiler_params=_params("arbitrary", "arbitrary"),
        name="in_proj",
    )(h, g, shift, scale, w)


def _in_proj_dilated_kernel(h_ref, g_ref, sh_ref, sc_ref, w_ref, gain_ref, cu_ref, cv_ref,
                            o0_ref, o1_ref, o2_ref, ou_ref, ovt_ref,
                            xm0_ref, xm1_ref, xm2_ref, slab_ref, *, parts):
    j = pl.program_id(1)
    seg = j // parts
    tm, d = xm0_ref.shape
    xm_refs = (xm0_ref, xm1_ref, xm2_ref)

    @pl.when(j == 0)
    def _():
        xm = _norm_modulate(h_ref[...], g_ref[...], sh_ref[0], sc_ref[0])
        for sb in range(d // HEAD_DIM):
            slab_ref[sb] = xm[:, sb * HEAD_DIM:(sb + 1) * HEAD_DIM]
        for (_, dil), xm_ref in zip(DIL_GROUPS, xm_refs):
            if dil == 1:
                xm_ref[...] = xm.astype(BF16)
                continue
            rows = tm // dil
            for sb in range(d // HEAD_DIM):
                for r in range(dil):
                    xm_ref[r * rows:(r + 1) * rows, sb * HEAD_DIM:(sb + 1) * HEAD_DIM] = (
                        slab_ref[sb, pl.ds(r, rows, stride=dil), :].astype(BF16))

    def project(xm_ref, o_ref, dil):
        rows = tm // dil
        slab = min(2 * HEAD_DIM, w_ref.shape[1])
        n_slabs = w_ref.shape[1] // slab

        @pl.when(seg % 3 == 2)
        def _():
            _convert_expert_slice(cu_ref, cv_ref, ou_ref, ovt_ref)
            for sb in range(n_slabs):
                cols = slice(sb * slab, (sb + 1) * slab)
                acc = jnp.dot(xm_ref[...], w_ref[:, cols], preferred_element_type=F32).astype(BF16)
                for r in range(dil):
                    o_ref[0, r, :, cols] = acc[r * rows:(r + 1) * rows]

        @pl.when(seg % 3 != 2)
        def _():
            _convert_expert_slice(cu_ref, cv_ref, ou_ref, ovt_ref)
            for sb in range(n_slabs):
                acc = jnp.dot(xm_ref[...], w_ref[:, sb * slab:(sb + 1) * slab], preferred_element_type=F32)
                for hh in range(slab // HEAD_DIM):
                    sl = slice(sb * slab + hh * HEAD_DIM, sb * slab + (hh + 1) * HEAD_DIM)
                    a = acc[:, hh * HEAD_DIM:(hh + 1) * HEAD_DIM]
                    ms = jnp.mean(a * a, axis=-1, keepdims=True)
                    y = (a * lax.rsqrt(ms + EPS) * gain_ref[0, :, sl]).astype(BF16)
                    for r in range(dil):
                        o_ref[0, r, :, sl] = y[r * rows:(r + 1) * rows]

    for grp, ((_, dil), xm_ref, o_ref) in enumerate(zip(DIL_GROUPS, xm_refs, (o0_ref, o1_ref, o2_ref))):
        @pl.when(seg // 3 == grp)
        def _(dil=dil, xm_ref=xm_ref, o_ref=o_ref):
            project(xm_ref, o_ref, dil)


def in_proj_dilated(h, g, shift, scale, w, gain, peer_u, peer_v, *, layer, peer_layer, seq, tm=512, tn=2048):
    t, d = h.shape
    n = w.shape[2]
    b = t // seq
    per_batch = seq // tm
    n_groups = len(DIL_GROUPS)
    width = gain.shape[2]
    parts = width // tn
    assert n == n_groups * 3 * width

    def out_spec(grp, dil):
        def imap(i, j):
            return (i // per_batch, 0, i % per_batch, jnp.clip(j - 3 * parts * grp, 0, 3 * parts - 1))
        return pl.BlockSpec((1, dil, tm // dil, tn), imap)

    n_cols = n // tn
    assert (t // tm) * n_cols >= peer_u.shape[1] // CONVERT_ROWS
    cv_in, cv_out, cv_shape = _convert_specs(peer_u, peer_v, peer_layer, lambda i, j: i * n_cols + j)
    *qkvs, u_bf, vt_bf = pl.pallas_call(
        functools.partial(_in_proj_dilated_kernel, parts=parts),
        grid=(t // tm, n // tn),
        in_specs=[
            pl.BlockSpec((tm, d), lambda i, j: (i, 0)),
            pl.BlockSpec((1, d), lambda i, j: (0, 0)),
            pl.BlockSpec((1, 1, d), lambda i, j: (i // per_batch, 0, 0)),
            pl.BlockSpec((1, 1, d), lambda i, j: (i // per_batch, 0, 0)),
            pl.BlockSpec((None, d, tn), lambda i, j: (layer, 0, j)),
            pl.BlockSpec((1, 1, tn), lambda i, j: (j // parts, 0, j % parts)),
        ] + cv_in,
        out_specs=[out_spec(grp, dil) for grp, (_, dil) in enumerate(DIL_GROUPS)] + cv_out,
        out_shape=[jax.ShapeDtypeStruct((b, dil, seq // dil, 3 * width), BF16)
                   for _, dil in DIL_GROUPS] + cv_shape,
        scratch_shapes=[pltpu.VMEM((tm, d), BF16)] * 3 + [pltpu.VMEM((d // HEAD_DIM, tm, HEAD_DIM), F32)],
        compiler_params=_params("arbitrary", "arbitrary"),
        name="in_proj_dilated",
    )(h, g, shift, scale, w, gain, peer_u, peer_v)
    return qkvs, u_bf, vt_bf


def _dil_attn_kernel(q_ref, kp_ref, kc_ref, vp_ref, vc_ref, bias_ref, o_ref, lse_ref, *, n_heads):
    n = pl.program_id(2)
    n_sub = q_ref.shape[2] // BLOCK
    lane = lax.broadcasted_iota(jnp.int32, (1, 2 * BLOCK), 1)
    pen = jnp.where(jnp.logical_and(lane < BLOCK, n == 0), NEG_BIG, 0.0).astype(F32)
    ones = jnp.ones((2 * BLOCK, HEAD_DIM), BF16)
    lse_lanes = BLOCK // n_heads
    lane_head = lax.broadcasted_iota(jnp.int32, (1, BLOCK), 1) // lse_lanes
    for sb in range(n_sub):
        rows = slice(sb * BLOCK, (sb + 1) * BLOCK)
        prev = slice((sb - 1) * BLOCK, sb * BLOCK)
        lse_pack = jnp.zeros((BLOCK, BLOCK), F32)
        for hh in range(n_heads):
            sl = slice(hh * HEAD_DIM, (hh + 1) * HEAD_DIM)
            q = q_ref[0, 0, rows, sl]
            k_prev = kp_ref[0, 0, :, sl] if sb == 0 else kc_ref[0, 0, prev, sl]
            v_prev = vp_ref[0, 0, :, sl] if sb == 0 else vc_ref[0, 0, prev, sl]
            k = jnp.concatenate([k_prev, kc_ref[0, 0, rows, sl]], axis=0)
            v = jnp.concatenate([v_prev, vc_ref[0, 0, rows, sl]], axis=0)
            logits = lax.dot_general(q, k, NT_DIMS, preferred_element_type=F32)
            logits = logits + bias_ref[hh]
            if sb == 0:
                logits = logits + pen
            m = jnp.max(logits, axis=-1, keepdims=True)
            p = jnp.exp(logits - m).astype(BF16)
            v_ext = jnp.concatenate([v, ones], axis=1)
            pv = jnp.dot(p, v_ext, preferred_element_type=F32)
            denom = pv[:, HEAD_DIM:]
            o_ref[0, 0, rows, sl] = (pv[:, :HEAD_DIM] / denom).astype(o_ref.dtype)
            lse_pack = jnp.where(lane_head == hh, m + jnp.log(denom), lse_pack)
        lse_ref[0, 0, rows, :] = lse_pack


def dilated_group_attention(qkv, bias, *, n_heads, q_blocks=2):
    b, dil, sub_len, c = qkv.shape
    assert BLOCK % n_heads == 0
    width = n_heads * HEAD_DIM
    q_blocks = min(q_blocks, sub_len // BLOCK)
    rows = q_blocks * BLOCK
    steps = sub_len // rows

    def cur(which):
        return pl.BlockSpec((1, 1, rows, width), lambda bi, r, n: (bi, r, n, which))

    def prev(which):
        return pl.BlockSpec((1, 1, BLOCK, width),
                            lambda bi, r, n: (bi, r, jnp.maximum(q_blocks * n - 1, 0), which))

    return pl.pallas_call(
        functools.partial(_dil_attn_kernel, n_heads=n_heads),
        grid=(b, dil, steps),
        in_specs=[cur(0), prev(1), cur(1), prev(2), cur(2),
                  pl.BlockSpec((n_heads, BLOCK, 2 * BLOCK), lambda bi, r, n: (0, 0, 0))],
        out_specs=[pl.BlockSpec((1, 1, rows, width), lambda bi, r, n: (bi, r, n, 0)),
                   pl.BlockSpec((1, 1, rows, BLOCK), lambda bi, r, n: (bi, r, n, 0))],
        out_shape=[jax.ShapeDtypeStruct((b, dil, sub_len, width), BF16),
                   jax.ShapeDtypeStruct((b, dil, sub_len, BLOCK), F32)],
        compiler_params=_params("arbitrary", "arbitrary", "arbitrary"),
        name=f"dilated_attn_d{dil}",
    )(qkv, qkv, qkv, qkv, qkv, bias)


def _rel_bucket(dist):
    exact = REL_BUCKETS // 2
    d_f = jnp.maximum(dist, exact).astype(F32)
    log_b = exact + (jnp.log(d_f / exact) / math.log(REL_MAX_DIST / exact)
                     * (REL_BUCKETS - exact)).astype(jnp.int32)
    return jnp.where(dist < exact, dist, jnp.minimum(log_b, REL_BUCKETS - 1))


def _band_bias(table_g, window, dil):
    w_steps = window // dil
    period = 3 * BLOCK
    per_delta = table_g[_rel_bucket(jnp.arange(w_steps + 1) * dil)].astype(F32)
    row = jnp.full((period, table_g.shape[1]), NEG_BIG, F32)
    row = row.at[BLOCK - w_steps:BLOCK + 1].set(per_delta[::-1])
    flat = jnp.tile(row.T, (1, BLOCK))[:, :BLOCK * (period - 1)]
    return flat.reshape(-1, BLOCK, period - 1)[:, :, :2 * BLOCK]


def _out_proj_merge_kernel(o0_ref, o1_ref, o2_ref, l0_ref, l1_ref, l2_ref, w_ref, h_ref, g_ref, out_ref,
                           os1_ref, os2_ref, ls1_ref, ls2_ref, y_ref):
    tm = out_ref.shape[0]
    n_heads = y_ref.shape[1] // HEAD_DIM

    def to_token_order(src_ref, dst_ref, sl):
        dil = src_ref.shape[1]
        for r in range(dil):
            dst_ref[pl.ds(r, tm // dil, stride=dil), :] = src_ref[0, r, :, sl].astype(F32)

    to_token_order(l1_ref, ls1_ref, slice(None))
    to_token_order(l2_ref, ls2_ref, slice(None))
    l0, l1, l2 = l0_ref[0, 0], ls1_ref[...], ls2_ref[...]
    m = jnp.maximum(jnp.maximum(l0, l1), l2)
    e0, e1, e2 = jnp.exp(l0 - m), jnp.exp(l1 - m), jnp.exp(l2 - m)
    inv = 1.0 / (e0 + e1 + e2)
    a0, a1, a2 = e0 * inv, e1 * inv, e2 * inv
    for hh in range(n_heads):
        sl = slice(hh * HEAD_DIM, (hh + 1) * HEAD_DIM)
        col = slice(hh * (BLOCK // n_heads), hh * (BLOCK // n_heads) + 1)
        to_token_order(o1_ref, os1_ref, sl)
        to_token_order(o2_ref, os2_ref, sl)
        y = (a0[:, col] * o0_ref[0, 0, :, sl].astype(F32) + a1[:, col] * os1_ref[...]
             + a2[:, col] * os2_ref[...])
        y_ref[:, sl] = y.astype(BF16)
    proj = jnp.dot(y_ref[...], w_ref[...], preferred_element_type=F32)
    out_ref[...] = h_ref[...] + g_ref[0] * proj


def out_proj_merge(outs, lses, w, h, gate, *, layer, seq, tm=256):
    t, d = h.shape
    width = w.shape[1]
    per_batch = seq // tm
    row = pl.BlockSpec((tm, d), lambda i: (i, 0))

    def grp_spec(a):
        dil, cols = a.shape[1], a.shape[3]
        return pl.BlockSpec((1, dil, tm // dil, cols), lambda i: (i // per_batch, 0, i % per_batch, 0))

    return pl.pallas_call(
        _out_proj_merge_kernel,
        grid=(t // tm,),
        in_specs=[grp_spec(a) for a in (*outs, *lses)] + [
            pl.BlockSpec((None, width, d), lambda i: (layer, 0, 0)),
            row,
            pl.BlockSpec((1, 1, d), lambda i: (i // per_batch, 0, 0)),
        ],
        out_specs=row,
        out_shape=jax.ShapeDtypeStruct((t, d), F32),
        scratch_shapes=[pltpu.VMEM((tm, HEAD_DIM), F32)] * 4 + [pltpu.VMEM((tm, width), BF16)],
        compiler_params=_params("arbitrary"),
        name="out_proj_merge",
    )(*outs, *lses, w, h, gate)


def _out_proj_kernel(y_ref, w_ref, h_ref, g_ref, out_ref):
    proj = jnp.dot(y_ref[...], w_ref[...], preferred_element_type=F32)
    out_ref[...] = h_ref[...] + g_ref[0] * proj


def out_proj(y, w, h, gate, *, layer, seq, tm=512):
    t, d = h.shape
    width = w.shape[1]
    per_batch = seq // tm
    row = lambda cols: pl.BlockSpec((tm, cols), lambda i: (i, 0))
    return pl.pallas_call(
        _out_proj_kernel,
        grid=(t // tm,),
        in_specs=[row(width), pl.BlockSpec((None, width, d), lambda i: (layer, 0, 0)), row(d),
                  pl.BlockSpec((1, 1, d), lambda i: (i // per_batch, 0, 0))],
        out_specs=row(d),
        out_shape=jax.ShapeDtypeStruct((t, d), F32),
        compiler_params=_params("arbitrary"),
        name="out_proj",
    )(y, w, h, gate)


def _split2(x):
    hi = x.astype(BF16)
    lo = (x - hi.astype(F32)).astype(BF16)
    return hi, lo


def _sb_kernel(q_ref, k_ref, v_ref, o_ref, acc_ref, decay_ref, *, scale):
    nq = q_ref.shape[1] // BLOCK
    q_base = pl.program_id(2) * q_ref.shape[1]
    n_heads = acc_ref.shape[0]
    row = lax.broadcasted_iota(jnp.int32, (BLOCK, BLOCK), 0)
    col = lax.broadcasted_iota(jnp.int32, (BLOCK, BLOCK), 1)
    before = col < row
    tri = jnp.where(row > col, 1.0, 0.0).astype(BF16)
    tri_ext = jnp.concatenate([tri, jnp.ones((BLOCK, BLOCK), BF16)], axis=1)

    def tails_and_sums(sps):
        parts = [t for sp in sps for t in _split2(sp)]
        r = jnp.dot(jnp.concatenate(parts, axis=0), tri_ext, preferred_element_type=F32)
        out = []
        for g in range(len(sps)):
            hi = r[(2 * g) * BLOCK:(2 * g + 1) * BLOCK]
            lo = r[(2 * g + 1) * BLOCK:(2 * g + 2) * BLOCK]
            both = hi + lo
            out.append((both[:, :BLOCK], both[:, BLOCK:]))
        return out

    def softplus(z):
        return jnp.maximum(z, 0.0) + jnp.log1p(jnp.exp(-jnp.abs(z)))

    def logits(q0, k0, g):
        cs = slice(g * HEAD_DIM, (g + 1) * HEAD_DIM)
        return lax.dot_general(q_ref[0, pl.ds(q0, BLOCK), cs], k_ref[0, pl.ds(k0, BLOCK), cs],
                               NT_DIMS, preferred_element_type=F32) * scale

    def values(k0, g):
        return v_ref[0, pl.ds(k0, BLOCK), g * HEAD_DIM:(g + 1) * HEAD_DIM]

    def qblock(i, carry):
        q0 = pl.multiple_of(i * BLOCK, BLOCK)
        d0 = pl.multiple_of(q_base + q0, BLOCK)
        zs = [logits(q0, d0, g) for g in range(n_heads)]
        sps = [softplus(z) for z in zs]
        ts = tails_and_sums([jnp.where(before, sp, 0.0) for sp in sps])
        live = None
        for g in range(n_heads):
            tail, rsum = ts[g]
            a = jnp.where(before, jnp.exp(zs[g] - sps[g] - tail), 0.0)
            acc_ref[g] = jnp.dot(a.astype(BF16), values(d0, g), preferred_element_type=F32)
            decay_ref[g] = rsum
            lo = jnp.min(rsum)
            live = lo if live is None else jnp.minimum(live, lo)

        def cond(c):
            j, live = c
            return jnp.logical_and(j >= 0, live < SB_DEAD_DECAY)

        def body(c):
            j, _ = c
            k0 = pl.multiple_of(j * BLOCK, BLOCK)
            zs = [logits(q0, k0, g) for g in range(n_heads)]
            sps = [softplus(z) for z in zs]
            ts = tails_and_sums(sps)
            live = None
            for g in range(n_heads):
                tail, rsum = ts[g]
                decay = decay_ref[g]
                a = jnp.exp(zs[g] - sps[g] - (decay + tail))
                acc_ref[g] += jnp.dot(a.astype(BF16), values(k0, g), preferred_element_type=F32)
                decay = decay + rsum
                decay_ref[g] = decay
                lo = jnp.min(decay)
                live = lo if live is None else jnp.minimum(live, lo)
            return j - 1, live

        lax.while_loop(cond, body, (d0 // BLOCK - 1, live))
        for g in range(n_heads):
            o_ref[0, pl.ds(q0, BLOCK), g * HEAD_DIM:(g + 1) * HEAD_DIM] = acc_ref[g].astype(o_ref.dtype)
        return carry

    lax.fori_loop(0, nq, qblock, 0)


def stick_breaking_attention(qkv, *, n_heads):
    b, s, _ = qkv.shape
    hps = min(SB_HEADS_PER_STEP, n_heads)
    steps = n_heads // hps
    width = hps * HEAD_DIM
    qc = min(SB_Q_CHUNK, s)

    def spec(which):
        return pl.BlockSpec((1, s, width), lambda bi, h, c: (bi, 0, which * steps + h))

    return pl.pallas_call(
        functools.partial(_sb_kernel, scale=1.0 / math.sqrt(HEAD_DIM)),
        grid=(b, steps, s // qc),
        in_specs=[pl.BlockSpec((1, qc, width), lambda bi, h, c: (bi, c, h)), spec(1), spec(2)],
        out_specs=pl.BlockSpec((1, qc, width), lambda bi, h, c: (bi, c, h)),
        out_shape=jax.ShapeDtypeStruct((b, s, n_heads * HEAD_DIM), BF16),
        scratch_shapes=[pltpu.VMEM((hps, BLOCK, HEAD_DIM), F32),
                        pltpu.VMEM((hps, BLOCK, BLOCK), F32)],
        compiler_params=_params("arbitrary", "arbitrary", "arbitrary"),
        name="stick_breaking",
    )(qkv, qkv, qkv)


UNRANKED = 127.0
RANK_MARK = -(2.0 ** 100)


def _as_f32(r):
    return jnp.asarray(r).astype(F32)


def _extract_ranked(s, dst_ref, count):
    n = s.shape[0]
    idx = lax.broadcasted_iota(jnp.int32, s.shape, 0).astype(F32)

    def body(r, carry):
        s, rank = carry
        m = jnp.max(s, axis=0, keepdims=True)
        first = jnp.min(jnp.where(s == m, idx, float(n)), axis=0, keepdims=True)
        pick = idx == first
        dst_ref[pl.ds(r, 1), :] = m
        return jnp.where(pick, -jnp.inf, s), jnp.where(pick, _as_f32(r), rank)

    _, rank = lax.fori_loop(0, count, body, (s, jnp.full(s.shape, UNRANKED, F32)))
    return rank


def _candidates(sv1, sv2):
    k = PEER_TOPK
    groups = [sv1[0:1] + sv2]
    groups += [sv1[a:a + 1] + sv2[0:k // 2] for a in range(1, k // 2)]
    groups += [sv1[k // 2:] + sv2[0:1]]
    spans = [(0, k)] + [(k + (a - 1) * (k // 2), k // 2) for a in range(1, k // 2)]
    tail0 = k + (k // 2 - 1) * (k // 2)
    spans += [(tail0 + a, 1) for a in range(k // 2)]
    return jnp.concatenate(groups, axis=0), spans


def _staircase_counts(taken, spans):
    return [jnp.sum(taken[lo:lo + n], axis=0, keepdims=True) for lo, n in spans]


def _peer_select_kernel(h_ref, g_ref, sh_ref, sc_ref, wq_ref, keys_ref,
                        xf_ref, r2_ref, a2_ref, lim_ref, a1_ref,
                        qt_ref, sv1_tiles, sv2_tiles, best_ref):
    xf = _norm_modulate(h_ref[...], g_ref[...], sh_ref[0], sc_ref[0]).astype(BF16)
    xf_ref[...] = xf
    qt_ref[...] = lax.dot_general(wq_ref[...], xf, NT_DIMS, preferred_element_type=F32).astype(BF16)
    k = PEER_TOPK
    half = N_SUBKEYS
    kf = float(k)
    n_tiles = sv2_tiles.shape[0]

    def head(h, carry):
        r0 = pl.multiple_of(h * 2 * half, 2 * half)
        s1 = jnp.dot(keys_ref[h, 0], qt_ref[pl.ds(r0, half), :], preferred_element_type=F32)
        s2 = jnp.dot(keys_ref[h, 1], qt_ref[pl.ds(r0 + half, half), :], preferred_element_type=F32)

        def sorted1():
            return jnp.concatenate([sv1_tiles[lt] for lt in range(n_tiles)], axis=1)

        def sorted2():
            return jnp.concatenate([sv2_tiles[lt] for lt in range(n_tiles)], axis=1)

        def emit(rank2, lim):
            sv1, sv2, best = sorted1(), sorted2(), best_ref[...]
            z = jnp.sum(jnp.exp(best - best[0:1]), axis=0, keepdims=True)
            r2_ref[h] = rank2.astype(BF16)
            a2_ref[h] = jnp.exp(s2 - sv2[0:1]).astype(BF16)
            a1 = jnp.exp(s1 - sv1[0:1]) / z
            rows = lim_ref.shape[2]
            for grp in range(half // rows):
                lim_ref[h, grp] = lim[grp * rows:(grp + 1) * rows]
                a1_ref[h, grp] = a1[grp * rows:(grp + 1) * rows]

        marked = []
        for lt in range(n_tiles):
            def round12(r, c, lt=lt):
                w1, w2 = c
                m1 = jnp.max(w1, axis=0, keepdims=True)
                m2 = jnp.max(w2, axis=0, keepdims=True)
                sv1_tiles[lt, pl.ds(r, 1), :] = m1
                sv2_tiles[lt, pl.ds(r, 1), :] = m2
                mark = RANK_MARK * (1.0 + _as_f32(r) / kf)
                return jnp.where(w1 == m1, -jnp.inf, w1), jnp.where(w2 == m2, mark, w2)

            ls = slice(lt * 128, (lt + 1) * 128)
            marked.append(lax.fori_loop(0, k, round12, (s1[:, ls], s2[:, ls]))[1])
        w2 = jnp.concatenate(marked, axis=1)
        rank2 = jnp.where(w2 <= RANK_MARK, w2 * (kf / RANK_MARK) - kf, UNRANKED)

        sv1 = sorted1()
        cand, spans = _candidates(sv1, sorted2())

        def round_c(r, w):
            m = jnp.max(w, axis=0, keepdims=True)
            best_ref[pl.ds(r, 1), :] = m
            return jnp.where(w == m, -jnp.inf, w)

        lax.fori_loop(0, k, round_c, cand)
        taken = jnp.where(cand >= best_ref[k - 1:k, :], 1.0, 0.0)
        counts = _staircase_counts(taken, spans)
        lim = jnp.zeros_like(s1)
        for a in range(k):
            lim = jnp.where(s1 == sv1[a:a + 1], counts[a], lim)
        emit(rank2, lim)

        n1 = jnp.sum(jnp.where(s1 >= sv1[k - 1:k], 1.0, 0.0), axis=0, keepdims=True)
        n2 = jnp.sum(jnp.where(rank2 < kf, 1.0, 0.0), axis=0, keepdims=True)
        nc = jnp.sum(taken, axis=0, keepdims=True)
        tied = jnp.logical_or(jnp.max(jnp.maximum(jnp.maximum(n1, n2), nc)) > kf,
                              jnp.min(s2) <= RANK_MARK)

        @pl.when(tied)
        def _():
            tiles = [slice(lt * 128, (lt + 1) * 128) for lt in range(n_tiles)]
            rank1 = jnp.concatenate([_extract_ranked(s1[:, ls], sv1_tiles.at[lt], k)
                                     for lt, ls in enumerate(tiles)], axis=1)
            rank2 = jnp.concatenate([_extract_ranked(s2[:, ls], sv2_tiles.at[lt], k)
                                     for lt, ls in enumerate(tiles)], axis=1)
            cand, spans = _candidates(sorted1(), sorted2())
            taken = jnp.where(_extract_ranked(cand, best_ref, k) < kf, 1.0, 0.0)
            counts = _staircase_counts(taken, spans)
            lim = jnp.zeros_like(s1)
            for a in range(k):
                lim = jnp.where(rank1 == float(a), counts[a], lim)
            emit(rank2, lim)

        return carry

    lax.fori_loop(0, PEER_HEADS, head, 0)


def peer_select(h, g, shift, scale, wq_t, keys, *, layer, seq, rows, tm=512):
    t, d = h.shape
    nq = wq_t.shape[1]
    per_batch = seq // tm
    tab_spec = pl.BlockSpec((PEER_HEADS, N_SUBKEYS, tm), lambda i: (0, 0, i))
    tab = lambda dt: jax.ShapeDtypeStruct((PEER_HEADS, N_SUBKEYS, t), dt)
    grp_spec = pl.BlockSpec((PEER_HEADS, N_SUBKEYS // rows, rows, tm), lambda i: (0, 0, 0, i))
    grp = jax.ShapeDtypeStruct((PEER_HEADS, N_SUBKEYS // rows, rows, t), F32)
    return pl.pallas_call(
        _peer_select_kernel,
        grid=(t // tm,),
        in_specs=[
            pl.BlockSpec((tm, d), lambda i: (i, 0)),
            pl.BlockSpec((1, d), lambda i: (0, 0)),
            pl.BlockSpec((1, 1, d), lambda i: (i // per_batch, 0, 0)),
            pl.BlockSpec((1, 1, d), lambda i: (i // per_batch, 0, 0)),
            pl.BlockSpec((None, nq, d), lambda i: (layer, 0, 0)),
            pl.BlockSpec((None,) + keys.shape[1:], lambda i: (layer, 0, 0, 0, 0)),
        ],
        out_specs=[pl.BlockSpec((tm, d), lambda i: (i, 0)), tab_spec, tab_spec, grp_spec, grp_spec],
        out_shape=[jax.ShapeDtypeStruct((t, d), BF16), tab(BF16), tab(BF16), grp, grp],
        scratch_shapes=[pltpu.VMEM((nq, tm), BF16),
                        pltpu.VMEM((tm // 128, PEER_TOPK, 128), F32),
                        pltpu.VMEM((tm // 128, PEER_TOPK, 128), F32),
                        pltpu.VMEM((PEER_TOPK, tm), F32)],
        compiler_params=_params("arbitrary"),
        name="peer_select",
    )(h, g, shift, scale, wq_t, keys)


def _peer_mix_kernel(xf_ref, u_ref, vt_ref, r2_ref, a2_ref, lim_ref, a1_ref, h_ref, g_ref, *rest,
                     n_blocks, convert):
    if convert:
        cu_ref, cv_ref, o_ref, ou_ref, ovt_ref, acc_ref, hid_a_ref, hid_b_ref, p_ref = rest
    else:
        o_ref, acc_ref, hid_a_ref, hid_b_ref, p_ref = rest
    g = pl.program_id(0)
    eb, tc = hid_a_ref.shape
    rows_per_step = eb // N_SUBKEYS
    mix_block = jnp.maximum(g - 1, 0) % n_blocks

    @pl.when(mix_block == 0)
    def _():
        acc_ref[...] = jnp.zeros_like(acc_ref)

    @pl.when(g == 0)
    def _():
        hid_b_ref[...] = jnp.zeros_like(hid_b_ref)

    def step(cur_ref, prev_ref):
        if convert:
            _convert_expert_slice(cu_ref, cv_ref, ou_ref, ovt_ref)
        cur_ref[...] = lax.dot_general(u_ref[...], xf_ref[...], NT_DIMS, preferred_element_type=F32)
        for kk in range(rows_per_step):
            gate = None
            for h in range(PEER_HEADS):
                lim = jnp.broadcast_to(lim_ref[h, 0, kk:kk + 1, :], (16, tc)).astype(BF16)
                a1 = jnp.broadcast_to(a1_ref[h, 0, kk:kk + 1, :], (16, tc)).astype(BF16)
                lim = jnp.concatenate([lim] * (N_SUBKEYS // 16), axis=0)
                a1 = jnp.concatenate([a1] * (N_SUBKEYS // 16), axis=0)
                w = jnp.where(r2_ref[h] < lim, a2_ref[h] * a1, jnp.zeros((), BF16))
                gate = w if gate is None else gate + w
            sl = slice(kk * N_SUBKEYS, (kk + 1) * N_SUBKEYS)
            hid = prev_ref[sl, :]
            act = 0.5 * hid * (1.0 + lax.erf(hid * INV_SQRT2))
            p_ref[sl, :] = act.astype(BF16) * gate
        acc_ref[...] += jnp.dot(vt_ref[...], p_ref[...], preferred_element_type=F32)

    @pl.when(g % 2 == 0)
    def _():
        step(hid_a_ref, hid_b_ref)

    @pl.when(g % 2 == 1)
    def _():
        step(hid_b_ref, hid_a_ref)

    @pl.when(jnp.logical_and(mix_block == n_blocks - 1, g > 0))
    def _():
        o_ref[...] = h_ref[...] + g_ref[0] * acc_ref[...].T


def peer_mix(xf, u, vt, r2, a2, lim, a1, h, gate, *, seq, next_tables=None, tc=512):
    t, d = h.shape
    rows = lim.shape[2]
    eb = rows * N_SUBKEYS
    n_blocks = u.shape[0] // eb
    n_items = (t // tc) * n_blocks
    per_batch = seq // tc

    def hid_item(g):
        return jnp.minimum(g, n_items - 1)

    def mix_item(g):
        return jnp.maximum(g - 1, 0)

    tab_spec = pl.BlockSpec((PEER_HEADS, N_SUBKEYS, tc), lambda g: (0, 0, mix_item(g) // n_blocks))
    row_spec = pl.BlockSpec((PEER_HEADS, 1, rows, tc),
                            lambda g: (0, mix_item(g) % n_blocks, 0, mix_item(g) // n_blocks))
    chunk_spec = pl.BlockSpec((tc, d), lambda g: (mix_item(g) // n_blocks, 0))
    cv_in, cv_out, cv_shape, cv_args = [], [], [], []
    if next_tables is not None:
        peer_u, peer_v, layer = next_tables
        assert n_items + 1 >= peer_u.shape[1] // CONVERT_ROWS
        cv_in, cv_out, cv_shape = _convert_specs(peer_u, peer_v, layer, lambda g: g)
        cv_args = [peer_u, peer_v]
    out = pl.pallas_call(
        functools.partial(_peer_mix_kernel, n_blocks=n_blocks, convert=bool(cv_args)),
        grid=(n_items + 1,),
        in_specs=[
            pl.BlockSpec((tc, d), lambda g: (hid_item(g) // n_blocks, 0)),
            pl.BlockSpec((eb, d), lambda g: (hid_item(g) % n_blocks, 0)),
            pl.BlockSpec((d, eb), lambda g: (0, mix_item(g) % n_blocks)),
            tab_spec, tab_spec, row_spec, row_spec,
            chunk_spec,
            pl.BlockSpec((1, 1, d), lambda g: (mix_item(g) // n_blocks // per_batch, 0, 0)),
        ] + cv_in,
        out_specs=[chunk_spec] + cv_out,
        out_shape=[jax.ShapeDtypeStruct((t, d), F32)] + cv_shape,
        scratch_shapes=[pltpu.VMEM((d, tc), F32),
                        pltpu.VMEM((eb, tc), F32),
                        pltpu.VMEM((eb, tc), F32),
                        pltpu.VMEM((eb, tc), BF16)],
        compiler_params=_params("arbitrary"),
        name="peer_mix",
    )(xf, u, vt, r2, a2, lim, a1, h, gate, *cv_args)
    return out if cv_args else out[0]


def kernel(x, c, rel_bias, norm1_g, norm2_g, w_ada, b_ada, a_w_in, a_q_gain, a_k_gain,
           b_w_in, w_out, peer_w_query, peer_sub_keys, peer_u, peer_v):
    b, s, d = x.shape
    depth = w_ada.shape[0]
    n_groups = len(DIL_GROUPS)
    heads = d // HEAD_DIM

    mod = adaln(c, w_ada, b_ada).reshape(depth, b, 6, 1, d)
    a_w, b_w, w_o = a_w_in.astype(BF16), b_w_in.astype(BF16), w_out.astype(BF16)
    wq_t = jnp.swapaxes(peer_w_query, 1, 2).astype(BF16)
    keys = peer_sub_keys.astype(BF16)
    u = vt = None
    h = x.reshape(b * s, d)
    for i in range(depth):
        sh1, sc1, g1, sh2, sc2, g2 = (mod[i, :, k] for k in range(6))
        j = i // 2
        if i % 2 == 0:
            ones = jnp.ones((HEAD_DIM,), F32)
            gain = jnp.stack([jnp.tile(row, heads) for g in range(n_groups)
                              for row in (a_q_gain[j, g] / math.sqrt(HEAD_DIM), a_k_gain[j, g], ones)])
            qkvs, u0, vt0 = in_proj_dilated(h, norm1_g[i][None], sh1, sc1, a_w, gain[:, None, :],
                                            peer_u, peer_v, layer=j, peer_layer=i, seq=s)
            if u is None:
                u, vt = u0, vt0
            outs, lses = [], []
            for g, (window, dil) in enumerate(DIL_GROUPS):
                bias = _band_bias(rel_bias[:, g * heads:(g + 1) * heads], window, dil)
                o, lse = dilated_group_attention(qkvs[g], bias, n_heads=heads)
                outs.append(o)
                lses.append(lse)
            h = out_proj_merge(outs, lses, w_o, h, g1, layer=i, seq=s)
        else:
            qkv = in_proj(h, norm1_g[i][None], sh1, sc1, b_w, layer=j, seq=s)
            y = stick_breaking_attention(qkv.reshape(b, s, -1), n_heads=heads)
            h = out_proj(y.reshape(b * s, d), w_o, h, g1, layer=i, seq=s)
        xf, r2, a2, lim, a1 = peer_select(h, norm2_g[i][None], sh2, sc2, wq_t, keys, layer=i, seq=s,
                                          rows=PEER_BLOCK_ROWS)
        if i + 1 < depth:
            h, u, vt = peer_mix(xf, u, vt, r2, a2, lim, a1, h, g2, seq=s,
                                next_tables=(peer_u, peer_v, i + 1))
        else:
            h = peer_mix(xf, u, vt, r2, a2, lim, a1, h, g2, seq=s)
    return h.reshape(b, s, d)
```

```python
import functools
import math

import jax
import jax.numpy as jnp
from jax import lax
from jax.experimental import pallas as pl
from jax.experimental.pallas import tpu as pltpu

F32 = jnp.float32
BF16 = jnp.bfloat16

HEAD_DIM = 128
BLOCK = 128
DIL_GROUPS = ((128, 1), (512, 4), (2048, 16))
REL_BUCKETS = 32
REL_MAX_DIST = 2048
PEER_HEADS = 8
N_SUBKEYS = 128
PEER_TOPK = 16
PEER_BLOCK_ROWS = 8
EPS = 1e-6
NEG_BIG = -1e30
SB_DEAD_DECAY = 104.0
SB_HEADS_PER_STEP = 8
SB_Q_CHUNK = 1024
INV_SQRT2 = 0.7071067811865476

VMEM_LIMIT = 60 * 1024 * 1024

NT_DIMS = (((1,), (1,)), ((), ()))


def _params(*sem):
    return pltpu.CompilerParams(dimension_semantics=sem, vmem_limit_bytes=VMEM_LIMIT)


def _adaln_kernel(c_ref, w_lo_ref, w_hi_ref, b_ref, o_ref):
    c = c_ref[...]
    cs = c / (1.0 + jnp.exp(-c))
    half = w_lo_ref.shape[1]
    dot = functools.partial(jnp.dot, preferred_element_type=F32, precision=lax.Precision.HIGHEST)
    o_ref[0] = dot(cs[:, :half], w_lo_ref[0]) + dot(cs[:, half:], w_hi_ref[0]) + b_ref[0]


def adaln(c, w_ada, b_ada, *, tn=1024):
    depth, d, n = w_ada.shape
    b = c.shape[0]
    rows = 8
    c_pad = jnp.pad(c, ((0, rows - b), (0, 0)))
    out = pl.pallas_call(
        _adaln_kernel,
        grid=(depth, n // tn),
        in_specs=[
            pl.BlockSpec((rows, d), lambda i, j: (0, 0)),
            pl.BlockSpec((1, d // 2, tn), lambda i, j: (i, 0, j)),
            pl.BlockSpec((1, d // 2, tn), lambda i, j: (i, 1, j)),
            pl.BlockSpec((1, 1, tn), lambda i, j: (i, 0, j)),
        ],
        out_specs=pl.BlockSpec((1, rows, tn), lambda i, j: (i, 0, j)),
        out_shape=jax.ShapeDtypeStruct((depth, rows, n), F32),
        compiler_params=_params("arbitrary", "arbitrary"),
        name="adaln",
    )(c_pad, w_ada, w_ada, b_ada.reshape(depth, 1, n))
    return out[:, :b]


CONVERT_ROWS = 128


def _convert_expert_slice(cu_ref, cv_ref, ou_ref, ovt_ref):
    ou_ref[...] = cu_ref[...].astype(BF16)
    ovt_ref[...] = cv_ref[...].T.astype(BF16)


def _convert_specs(peer_u, peer_v, layer, step_of):
    n_exp, d = peer_u.shape[1:]
    last = n_exp // CONVERT_ROWS - 1

    def slice_of(*idx):
        return jnp.minimum(step_of(*idx), last)

    src = pl.BlockSpec((None, CONVERT_ROWS, d), lambda *idx: (layer, slice_of(*idx), 0))
    in_specs = [src, src]
    out_specs = [pl.BlockSpec((CONVERT_ROWS, d), lambda *idx: (slice_of(*idx), 0)),
                 pl.BlockSpec((d, CONVERT_ROWS), lambda *idx: (0, slice_of(*idx)))]
    out_shape = [jax.ShapeDtypeStruct((n_exp, d), BF16), jax.ShapeDtypeStruct((d, n_exp), BF16)]
    return in_specs, out_specs, out_shape


def _norm_modulate(x, g, shift, scale):
    ms = jnp.mean(x * x, axis=-1, keepdims=True)
    y = x * lax.rsqrt(ms + EPS) * g
    return y * (1.0 + scale) + shift


def _in_proj_kernel(h_ref, g_ref, sh_ref, sc_ref, w_ref, o_ref, xm_ref):
    @pl.when(pl.program_id(1) == 0)
    def _():
        xm_ref[...] = _norm_modulate(h_ref[...], g_ref[...], sh_ref[0], sc_ref[0]).astype(BF16)

    o_ref[...] = jnp.dot(xm_ref[...], w_ref[...], preferred_element_type=F32).astype(o_ref.dtype)


def in_proj(h, g, shift, scale, w, *, layer, seq, tm=1024, tn=2048):
    t, d = h.shape
    n = w.shape[2]
    per_batch = seq // tm
    return pl.pallas_call(
        _in_proj_kernel,
        grid=(t // tm, n // tn),
        in_specs=[
            pl.BlockSpec((tm, d), lambda i, j: (i, 0)),
            pl.BlockSpec((1, d), lambda i, j: (0, 0)),
            pl.BlockSpec((1, 1, d), lambda i, j: (i // per_batch, 0, 0)),
            pl.BlockSpec((1, 1, d), lambda i, j: (i // per_batch, 0, 0)),
            pl.BlockSpec((None, d, tn), lambda i, j: (layer, 0, j)),
        ],
        out_specs=pl.BlockSpec((tm, tn), lambda i, j: (i, j)),
        out_shape=jax.ShapeDtypeStruct((t, n), BF16),
        scratch_shapes=[pltpu.VMEM((tm, d), BF16)],
        compiler_params=_params("arbitrary", "arbitrary"),
        name="in_proj",
    )(h, g, shift, scale, w)


def _in_proj_dilated_kernel(h_ref, g_ref, sh_ref, sc_ref, w_ref, gain_ref, cu_ref, cv_ref,
                            o0_ref, o1_ref, o2_ref, ou_ref, ovt_ref,
                            xm0_ref, xm1_ref, xm2_ref, slab_ref, *, parts):
    j = pl.program_id(1)
    seg = j // parts
    tm, d = xm0_ref.shape
    xm_refs = (xm0_ref, xm1_ref, xm2_ref)

    @pl.when(j == 0)
    def _():
        xm = _norm_modulate(h_ref[...], g_ref[...], sh_ref[0], sc_ref[0])
        for sb in range(d // HEAD_DIM):
            slab_ref[sb] = xm[:, sb * HEAD_DIM:(sb + 1) * HEAD_DIM]
        for (_, dil), xm_ref in zip(DIL_GROUPS, xm_refs):
            if dil == 1:
                xm_ref[...] = xm.astype(BF16)
                continue
            rows = tm // dil
            for sb in range(d // HEAD_DIM):
                for r in range(dil):
                    xm_ref[r * rows:(r + 1) * rows, sb * HEAD_DIM:(sb + 1) * HEAD_DIM] = (
                        slab_ref[sb, pl.ds(r, rows, stride=dil), :].astype(BF16))

    def project(xm_ref, o_ref, dil):
        rows = tm // dil
        slab = min(2 * HEAD_DIM, w_ref.shape[1])
        n_slabs = w_ref.shape[1] // slab

        @pl.when(seg % 3 == 2)
        def _():
            _convert_expert_slice(cu_ref, cv_ref, ou_ref, ovt_ref)
            for sb in range(n_slabs):
                cols = slice(sb * slab, (sb + 1) * slab)
                acc = jnp.dot(xm_ref[...], w_ref[:, cols], preferred_element_type=F32).astype(BF16)
                for r in range(dil):
                    o_ref[0, r, :, cols] = acc[r * rows:(r + 1) * rows]

        @pl.when(seg % 3 != 2)
        def _():
            _convert_expert_slice(cu_ref, cv_ref, ou_ref, ovt_ref)
            for sb in range(n_slabs):
                acc = jnp.dot(xm_ref[...], w_ref[:, sb * slab:(sb + 1) * slab], preferred_element_type=F32)
                for hh in range(slab // HEAD_DIM):
                    sl = slice(sb * slab + hh * HEAD_DIM, sb * slab + (hh + 1) * HEAD_DIM)
                    a = acc[:, hh * HEAD_DIM:(hh + 1) * HEAD_DIM]
                    ms = jnp.mean(a * a, axis=-1, keepdims=True)
                    y = (a * lax.rsqrt(ms + EPS) * gain_ref[0, :, sl]).astype(BF16)
                    for r in range(dil):
                        o_ref[0, r, :, sl] = y[r * rows:(r + 1) * rows]

    for grp, ((_, dil), xm_ref, o_ref) in enumerate(zip(DIL_GROUPS, xm_refs, (o0_ref, o1_ref, o2_ref))):
        @pl.when(seg // 3 == grp)
        def _(dil=dil, xm_ref=xm_ref, o_ref=o_ref):
            project(xm_ref, o_ref, dil)


def in_proj_dilated(h, g, shift, scale, w, gain, peer_u, peer_v, *, layer, peer_layer, seq, tm=512, tn=2048):
    t, d = h.shape
    n = w.shape[2]
    b = t // seq
    per_batch = seq // tm
    n_groups = len(DIL_GROUPS)
    width = gain.shape[2]
    parts = width // tn
    assert n == n_groups * 3 * width

    def out_spec(grp, dil):
        def imap(i, j):
            return (i // per_batch, 0, i % per_batch, jnp.clip(j - 3 * parts * grp, 0, 3 * parts - 1))
        return pl.BlockSpec((1, dil, tm // dil, tn), imap)

    n_cols = n // tn
    assert (t // tm) * n_cols >= peer_u.shape[1] // CONVERT_ROWS
    cv_in, cv_out, cv_shape = _convert_specs(peer_u, peer_v, peer_layer, lambda i, j: i * n_cols + j)
    *qkvs, u_bf, vt_bf = pl.pallas_call(
        functools.partial(_in_proj_dilated_kernel, parts=parts),
        grid=(t // tm, n // tn),
        in_specs=[
            pl.BlockSpec((tm, d), lambda i, j: (i, 0)),
            pl.BlockSpec((1, d), lambda i, j: (0, 0)),
            pl.BlockSpec((1, 1, d), lambda i, j: (i // per_batch, 0, 0)),
            pl.BlockSpec((1, 1, d), lambda i, j: (i // per_batch, 0, 0)),
            pl.BlockSpec((None, d, tn), lambda i, j: (layer, 0, j)),
            pl.BlockSpec((1, 1, tn), lambda i, j: (j // parts, 0, j % parts)),
        ] + cv_in,
        out_specs=[out_spec(grp, dil) for grp, (_, dil) in enumerate(DIL_GROUPS)] + cv_out,
        out_shape=[jax.ShapeDtypeStruct((b, dil, seq // dil, 3 * width), BF16)
                   for _, dil in DIL_GROUPS] + cv_shape,
        scratch_shapes=[pltpu.VMEM((tm, d), BF16)] * 3 + [pltpu.VMEM((d // HEAD_DIM, tm, HEAD_DIM), F32)],
        compiler_params=_params("arbitrary", "arbitrary"),
        name="in_proj_dilated",
    )(h, g, shift, scale, w, gain, peer_u, peer_v)
    return qkvs, u_bf, vt_bf


def _dil_attn_kernel(q_ref, kp_ref, kc_ref, vp_ref, vc_ref, bias_ref, o_ref, lse_ref, *, n_heads):
    n = pl.program_id(2)
    n_sub = q_ref.shape[2] // BLOCK
    lane = lax.broadcasted_iota(jnp.int32, (1, 2 * BLOCK), 1)
    pen = jnp.where(jnp.logical_and(lane < BLOCK, n == 0), NEG_BIG, 0.0).astype(F32)
    ones = jnp.ones((2 * BLOCK, HEAD_DIM), BF16)
    lse_lanes = BLOCK // n_heads
    lane_head = lax.broadcasted_iota(jnp.int32, (1, BLOCK), 1) // lse_lanes
    for sb in range(n_sub):
        rows = slice(sb * BLOCK, (sb + 1) * BLOCK)
        prev = slice((sb - 1) * BLOCK, sb * BLOCK)
        lse_pack = jnp.zeros((BLOCK, BLOCK), F32)
        for hh in range(n_heads):
            sl = slice(hh * HEAD_DIM, (hh + 1) * HEAD_DIM)
            q = q_ref[0, 0, rows, sl]
            k_prev = kp_ref[0, 0, :, sl] if sb == 0 else kc_ref[0, 0, prev, sl]
            v_prev = vp_ref[0, 0, :, sl] if sb == 0 else vc_ref[0, 0, prev, sl]
            k = jnp.concatenate([k_prev, kc_ref[0, 0, rows, sl]], axis=0)
            v = jnp.concatenate([v_prev, vc_ref[0, 0, rows, sl]], axis=0)
            logits = lax.dot_general(q, k, NT_DIMS, preferred_element_type=F32)
            logits = logits + bias_ref[hh]
            if sb == 0:
                logits = logits + pen
            m = jnp.max(logits, axis=-1, keepdims=True)
            p = jnp.exp(logits - m).astype(BF16)
            v_ext = jnp.concatenate([v, ones], axis=1)
            pv = jnp.dot(p, v_ext, preferred_element_type=F32)
            denom = pv[:, HEAD_DIM:]
            o_ref[0, 0, rows, sl] = (pv[:, :HEAD_DIM] / denom).astype(o_ref.dtype)
            lse_pack = jnp.where(lane_head == hh, m + jnp.log(denom), lse_pack)
        lse_ref[0, 0, rows, :] = lse_pack


def dilated_group_attention(qkv, bias, *, n_heads, q_blocks=2):
    b, dil, sub_len, c = qkv.shape
    assert BLOCK % n_heads == 0
    width = n_heads * HEAD_DIM
    q_blocks = min(q_blocks, sub_len // BLOCK)
    rows = q_blocks * BLOCK
    steps = sub_len // rows

    def cur(which):
        return pl.BlockSpec((1, 1, rows, width), lambda bi, r, n: (bi, r, n, which))

    def prev(which):
        return pl.BlockSpec((1, 1, BLOCK, width),
                            lambda bi, r, n: (bi, r, jnp.maximum(q_blocks * n - 1, 0), which))

    return pl.pallas_call(
        functools.partial(_dil_attn_kernel, n_heads=n_heads),
        grid=(b, dil, steps),
        in_specs=[cur(0), prev(1), cur(1), prev(2), cur(2),
                  pl.BlockSpec((n_heads, BLOCK, 2 * BLOCK), lambda bi, r, n: (0, 0, 0))],
        out_specs=[pl.BlockSpec((1, 1, rows, width), lambda bi, r, n: (bi, r, n, 0)),
                   pl.BlockSpec((1, 1, rows, BLOCK), lambda bi, r, n: (bi, r, n, 0))],
        out_shape=[jax.ShapeDtypeStruct((b, dil, sub_len, width), BF16),
                   jax.ShapeDtypeStruct((b, dil, sub_len, BLOCK), F32)],
        compiler_params=_params("arbitrary", "arbitrary", "arbitrary"),
        name=f"dilated_attn_d{dil}",
    )(qkv, qkv, qkv, qkv, qkv, bias)


def _rel_bucket(dist):
    exact = REL_BUCKETS // 2
    d_f = jnp.maximum(dist, exact).astype(F32)
    log_b = exact + (jnp.log(d_f / exact) / math.log(REL_MAX_DIST / exact)
                     * (REL_BUCKETS - exact)).astype(jnp.int32)
    return jnp.where(dist < exact, dist, jnp.minimum(log_b, REL_BUCKETS - 1))


def _band_bias(table_g, window, dil):
    w_steps = window // dil
    period = 3 * BLOCK
    per_delta = table_g[_rel_bucket(jnp.arange(w_steps + 1) * dil)].astype(F32)
    row = jnp.full((period, table_g.shape[1]), NEG_BIG, F32)
    row = row.at[BLOCK - w_steps:BLOCK + 1].set(per_delta[::-1])
    flat = jnp.tile(row.T, (1, BLOCK))[:, :BLOCK * (period - 1)]
    return flat.reshape(-1, BLOCK, period - 1)[:, :, :2 * BLOCK]


def _out_proj_merge_kernel(o0_ref, o1_ref, o2_ref, l0_ref, l1_ref, l2_ref, w_ref, h_ref, g_ref, out_ref,
                           os1_ref, os2_ref, ls1_ref, ls2_ref, y_ref):
    tm = out_ref.shape[0]
    n_heads = y_ref.shape[1] // HEAD_DIM

    def to_token_order(src_ref, dst_ref, sl):
        dil = src_ref.shape[1]
        for r in range(dil):
            dst_ref[pl.ds(r, tm // dil, stride=dil), :] = src_ref[0, r, :, sl].astype(F32)

    to_token_order(l1_ref, ls1_ref, slice(None))
    to_token_order(l2_ref, ls2_ref, slice(None))
    l0, l1, l2 = l0_ref[0, 0], ls1_ref[...], ls2_ref[...]
    m = jnp.maximum(jnp.maximum(l0, l1), l2)
    e0, e1, e2 = jnp.exp(l0 - m), jnp.exp(l1 - m), jnp.exp(l2 - m)
    inv = 1.0 / (e0 + e1 + e2)
    a0, a1, a2 = e0 * inv, e1 * inv, e2 * inv
    for hh in range(n_heads):
        sl = slice(hh * HEAD_DIM, (hh + 1) * HEAD_DIM)
        col = slice(hh * (BLOCK // n_heads), hh * (BLOCK // n_heads) + 1)
        to_token_order(o1_ref, os1_ref, sl)
        to_token_order(o2_ref, os2_ref, sl)
        y = (a0[:, col] * o0_ref[0, 0, :, sl].astype(F32) + a1[:, col] * os1_ref[...]
             + a2[:, col] * os2_ref[...])
        y_ref[:, sl] = y.astype(BF16)
    proj = jnp.dot(y_ref[...], w_ref[...], preferred_element_type=F32)
    out_ref[...] = h_ref[...] + g_ref[0] * proj


def out_proj_merge(outs, lses, w, h, gate, *, layer, seq, tm=256):
    t, d = h.shape
    width = w.shape[1]
    per_batch = seq // tm
    row = pl.BlockSpec((tm, d), lambda i: (i, 0))

    def grp_spec(a):
        dil, cols = a.shape[1], a.shape[3]
        return pl.BlockSpec((1, dil, tm // dil, cols), lambda i: (i // per_batch, 0, i % per_batch, 0))

    return pl.pallas_call(
        _out_proj_merge_kernel,
        grid=(t // tm,),
        in_specs=[grp_spec(a) for a in (*outs, *lses)] + [
            pl.BlockSpec((None, width, d), lambda i: (layer, 0, 0)),
            row,
            pl.BlockSpec((1, 1, d), lambda i: (i // per_batch, 0, 0)),
        ],
        out_specs=row,
        out_shape=jax.ShapeDtypeStruct((t, d), F32),
        scratch_shapes=[pltpu.VMEM((tm, HEAD_DIM), F32)] * 4 + [pltpu.VMEM((tm, width), BF16)],
        compiler_params=_params("arbitrary"),
        name="out_proj_merge",
    )(*outs, *lses, w, h, gate)


def _out_proj_kernel(y_ref, w_ref, h_ref, g_ref, out_ref):
    proj = jnp.dot(y_ref[...], w_ref[...], preferred_element_type=F32)
    out_ref[...] = h_ref[...] + g_ref[0] * proj


def out_proj(y, w, h, gate, *, layer, seq, tm=512):
    t, d = h.shape
    width = w.shape[1]
    per_batch = seq // tm
    row = lambda cols: pl.BlockSpec((tm, cols), lambda i: (i, 0))
    return pl.pallas_call(
        _out_proj_kernel,
        grid=(t // tm,),
        in_specs=[row(width), pl.BlockSpec((None, width, d), lambda i: (layer, 0, 0)), row(d),
                  pl.BlockSpec((1, 1, d), lambda i: (i // per_batch, 0, 0))],
        out_specs=row(d),
        out_shape=jax.ShapeDtypeStruct((t, d), F32),
        compiler_params=_params("arbitrary"),
        name="out_proj",
    )(y, w, h, gate)


def _split2(x):
    hi = x.astype(BF16)
    lo = (x - hi.astype(F32)).astype(BF16)
    return hi, lo


def _sb_kernel(q_ref, k_ref, v_ref, o_ref, acc_ref, decay_ref, *, scale):
    nq = q_ref.shape[1] // BLOCK
    q_base = pl.program_id(2) * q_ref.shape[1]
    n_heads = acc_ref.shape[0]
    row = lax.broadcasted_iota(jnp.int32, (BLOCK, BLOCK), 0)
    col = lax.broadcasted_iota(jnp.int32, (BLOCK, BLOCK), 1)
    before = col < row
    tri = jnp.where(row > col, 1.0, 0.0).astype(BF16)
    tri_ext = jnp.concatenate([tri, jnp.ones((BLOCK, BLOCK), BF16)], axis=1)

    def tails_and_sums(sps):
        parts = [t for sp in sps for t in _split2(sp)]
        r = jnp.dot(jnp.concatenate(parts, axis=0), tri_ext, preferred_element_type=F32)
        out = []
        for g in range(len(sps)):
            hi = r[(2 * g) * BLOCK:(2 * g + 1) * BLOCK]
            lo = r[(2 * g + 1) * BLOCK:(2 * g + 2) * BLOCK]
            both = hi + lo
            out.append((both[:, :BLOCK], both[:, BLOCK:]))
        return out

    def softplus(z):
        return jnp.maximum(z, 0.0) + jnp.log1p(jnp.exp(-jnp.abs(z)))

    def logits(q0, k0, g):
        cs = slice(g * HEAD_DIM, (g + 1) * HEAD_DIM)
        return lax.dot_general(q_ref[0, pl.ds(q0, BLOCK), cs], k_ref[0, pl.ds(k0, BLOCK), cs],
                               NT_DIMS, preferred_element_type=F32) * scale

    def values(k0, g):
        return v_ref[0, pl.ds(k0, BLOCK), g * HEAD_DIM:(g + 1) * HEAD_DIM]

    def qblock(i, carry):
        q0 = pl.multiple_of(i * BLOCK, BLOCK)
        d0 = pl.multiple_of(q_base + q0, BLOCK)
        zs = [logits(q0, d0, g) for g in range(n_heads)]
        sps = [softplus(z) for z in zs]
        ts = tails_and_sums([jnp.where(before, sp, 0.0) for sp in sps])
        live = None
        for g in range(n_heads):
            tail, rsum = ts[g]
            a = jnp.where(before, jnp.exp(zs[g] - sps[g] - tail), 0.0)
            acc_ref[g] = jnp.dot(a.astype(BF16), values(d0, g), preferred_element_type=F32)
            decay_ref[g] = rsum
            lo = jnp.min(rsum)
            live = lo if live is None else jnp.minimum(live, lo)

        def cond(c):
            j, live = c
            return jnp.logical_and(j >= 0, live < SB_DEAD_DECAY)

        def body(c):
            j, _ = c
            k0 = pl.multiple_of(j * BLOCK, BLOCK)
            zs = [logits(q0, k0, g) for g in range(n_heads)]
            sps = [softplus(z) for z in zs]
            ts = tails_and_sums(sps)
            live = None
            for g in range(n_heads):
                tail, rsum = ts[g]
                decay = decay_ref[g]
                a = jnp.exp(zs[g] - sps[g] - (decay + tail))
                acc_ref[g] += jnp.dot(a.astype(BF16), values(k0, g), preferred_element_type=F32)
                decay = decay + rsum
                decay_ref[g] = decay
                lo = jnp.min(decay)
                live = lo if live is None else jnp.minimum(live, lo)
            return j - 1, live

        lax.while_loop(cond, body, (d0 // BLOCK - 1, live))
        for g in range(n_heads):
            o_ref[0, pl.ds(q0, BLOCK), g * HEAD_DIM:(g + 1) * HEAD_DIM] = acc_ref[g].astype(o_ref.dtype)
        return carry

    lax.fori_loop(0, nq, qblock, 0)


def stick_breaking_attention(qkv, *, n_heads):
    b, s, _ = qkv.shape
    hps = min(SB_HEADS_PER_STEP, n_heads)
    steps = n_heads // hps
    width = hps * HEAD_DIM
    qc = min(SB_Q_CHUNK, s)

    def spec(which):
        return pl.BlockSpec((1, s, width), lambda bi, h, c: (bi, 0, which * steps + h))

    return pl.pallas_call(
        functools.partial(_sb_kernel, scale=1.0 / math.sqrt(HEAD_DIM)),
        grid=(b, steps, s // qc),
        in_specs=[pl.BlockSpec((1, qc, width), lambda bi, h, c: (bi, c, h)), spec(1), spec(2)],
        out_specs=pl.BlockSpec((1, qc, width), lambda bi, h, c: (bi, c, h)),
        out_shape=jax.ShapeDtypeStruct((b, s, n_heads * HEAD_DIM), BF16),
        scratch_shapes=[pltpu.VMEM((hps, BLOCK, HEAD_DIM), F32),
                        pltpu.VMEM((hps, BLOCK, BLOCK), F32)],
        compiler_params=_params("arbitrary", "arbitrary", "arbitrary"),
        name="stick_breaking",
    )(qkv, qkv, qkv)


UNRANKED = 127.0
RANK_MARK = -(2.0 ** 100)


def _as_f32(r):
    return jnp.asarray(r).astype(F32)


def _extract_ranked(s, dst_ref, count):
    n = s.shape[0]
    idx = lax.broadcasted_iota(jnp.int32, s.shape, 0).astype(F32)

    def body(r, carry):
        s, rank = carry
        m = jnp.max(s, axis=0, keepdims=True)
        first = jnp.min(jnp.where(s == m, idx, float(n)), axis=0, keepdims=True)
        pick = idx == first
        dst_ref[pl.ds(r, 1), :] = m
        return jnp.where(pick, -jnp.inf, s), jnp.where(pick, _as_f32(r), rank)

    _, rank = lax.fori_loop(0, count, body, (s, jnp.full(s.shape, UNRANKED, F32)))
    return rank


def _candidates(sv1, sv2):
    k = PEER_TOPK
    groups = [sv1[0:1] + sv2]
    groups += [sv1[a:a + 1] + sv2[0:k // 2] for a in range(1, k // 2)]
    groups += [sv1[k // 2:] + sv2[0:1]]
    spans = [(0, k)] + [(k + (a - 1) * (k // 2), k // 2) for a in range(1, k // 2)]
    tail0 = k + (k // 2 - 1) * (k // 2)
    spans += [(tail0 + a, 1) for a in range(k // 2)]
    return jnp.concatenate(groups, axis=0), spans


def _staircase_counts(taken, spans):
    return [jnp.sum(taken[lo:lo + n], axis=0, keepdims=True) for lo, n in spans]


def _peer_select_kernel(h_ref, g_ref, sh_ref, sc_ref, wq_ref, keys_ref,
                        xf_ref, r2_ref, a2_ref, lim_ref, a1_ref,
                        qt_ref, sv1_tiles, sv2_tiles, best_ref):
    xf = _norm_modulate(h_ref[...], g_ref[...], sh_ref[0], sc_ref[0]).astype(BF16)
    xf_ref[...] = xf
    qt_ref[...] = lax.dot_general(wq_ref[...], xf, NT_DIMS, preferred_element_type=F32).astype(BF16)
    k = PEER_TOPK
    half = N_SUBKEYS
    kf = float(k)
    n_tiles = sv2_tiles.shape[0]

    def head(h, carry):
        r0 = pl.multiple_of(h * 2 * half, 2 * half)
        s1 = jnp.dot(keys_ref[h, 0], qt_ref[pl.ds(r0, half), :], preferred_element_type=F32)
        s2 = jnp.dot(keys_ref[h, 1], qt_ref[pl.ds(r0 + half, half), :], preferred_element_type=F32)

        def sorted1():
            return jnp.concatenate([sv1_tiles[lt] for lt in range(n_tiles)], axis=1)

        def sorted2():
            return jnp.concatenate([sv2_tiles[lt] for lt in range(n_tiles)], axis=1)

        def emit(rank2, lim):
            sv1, sv2, best = sorted1(), sorted2(), best_ref[...]
            z = jnp.sum(jnp.exp(best - best[0:1]), axis=0, keepdims=True)
            r2_ref[h] = rank2.astype(BF16)
            a2_ref[h] = jnp.exp(s2 - sv2[0:1]).astype(BF16)
            a1 = jnp.exp(s1 - sv1[0:1]) / z
            rows = lim_ref.shape[2]
            for grp in range(half // rows):
                lim_ref[h, grp] = lim[grp * rows:(grp + 1) * rows]
                a1_ref[h, grp] = a1[grp * rows:(grp + 1) * rows]

        marked = []
        for lt in range(n_tiles):
            def round12(r, c, lt=lt):
                w1, w2 = c
                m1 = jnp.max(w1, axis=0, keepdims=True)
                m2 = jnp.max(w2, axis=0, keepdims=True)
                sv1_tiles[lt, pl.ds(r, 1), :] = m1
                sv2_tiles[lt, pl.ds(r, 1), :] = m2
                mark = RANK_MARK * (1.0 + _as_f32(r) / kf)
                return jnp.where(w1 == m1, -jnp.inf, w1), jnp.where(w2 == m2, mark, w2)

            ls = slice(lt * 128, (lt + 1) * 128)
            marked.append(lax.fori_loop(0, k, round12, (s1[:, ls], s2[:, ls]), unroll=True)[1])
        w2 = jnp.concatenate(marked, axis=1)
        rank2 = jnp.where(w2 <= RANK_MARK, w2 * (kf / RANK_MARK) - kf, UNRANKED)

        sv1 = sorted1()
        cand, spans = _candidates(sv1, sorted2())

        def round_c(r, w):
            m = jnp.max(w, axis=0, keepdims=True)
            best_ref[pl.ds(r, 1), :] = m
            return jnp.where(w == m, -jnp.inf, w)

        lax.fori_loop(0, k, round_c, cand, unroll=True)
        taken = jnp.where(cand >= best_ref[k - 1:k, :], 1.0, 0.0)
        counts = _staircase_counts(taken, spans)
        lim = jnp.zeros_like(s1)
        for a in range(k):
            lim = jnp.where(s1 == sv1[a:a + 1], counts[a], lim)
        emit(rank2, lim)

        n1 = jnp.sum(jnp.where(s1 >= sv1[k - 1:k], 1.0, 0.0), axis=0, keepdims=True)
        n2 = jnp.sum(jnp.where(rank2 < kf, 1.0, 0.0), axis=0, keepdims=True)
        nc = jnp.sum(taken, axis=0, keepdims=True)
        tied = jnp.logical_or(jnp.max(jnp.maximum(jnp.maximum(n1, n2), nc)) > kf,
                              jnp.min(s2) <= RANK_MARK)

        @pl.when(tied)
        def _():
            tiles = [slice(lt * 128, (lt + 1) * 128) for lt in range(n_tiles)]
            rank1 = jnp.concatenate([_extract_ranked(s1[:, ls], sv1_tiles.at[lt], k)
                                     for lt, ls in enumerate(tiles)], axis=1)
            rank2 = jnp.concatenate([_extract_ranked(s2[:, ls], sv2_tiles.at[lt], k)
                                     for lt, ls in enumerate(tiles)], axis=1)
            cand, spans = _candidates(sorted1(), sorted2())
            taken = jnp.where(_extract_ranked(cand, best_ref, k) < kf, 1.0, 0.0)
            counts = _staircase_counts(taken, spans)
            lim = jnp.zeros_like(s1)
            for a in range(k):
                lim = jnp.where(rank1 == float(a), counts[a], lim)
            emit(rank2, lim)

        return carry

    lax.fori_loop(0, PEER_HEADS, head, 0)


def peer_select(h, g, shift, scale, wq_t, keys, *, layer, seq, rows, tm=512):
    t, d = h.shape
    nq = wq_t.shape[1]
    per_batch = seq // tm
    tab_spec = pl.BlockSpec((PEER_HEADS, N_SUBKEYS, tm), lambda i: (0, 0, i))
    tab = lambda dt: jax.ShapeDtypeStruct((PEER_HEADS, N_SUBKEYS, t), dt)
    grp_spec = pl.BlockSpec((PEER_HEADS, N_SUBKEYS // rows, rows, tm), lambda i: (0, 0, 0, i))
    grp = jax.ShapeDtypeStruct((PEER_HEADS, N_SUBKEYS // rows, rows, t), F32)
    return pl.pallas_call(
        _peer_select_kernel,
        grid=(t // tm,),
        in_specs=[
            pl.BlockSpec((tm, d), lambda i: (i, 0)),
            pl.BlockSpec((1, d), lambda i: (0, 0)),
            pl.BlockSpec((1, 1, d), lambda i: (i // per_batch, 0, 0)),
            pl.BlockSpec((1, 1, d), lambda i: (i // per_batch, 0, 0)),
            pl.BlockSpec((None, nq, d), lambda i: (layer, 0, 0)),
            pl.BlockSpec((None,) + keys.shape[1:], lambda i: (layer, 0, 0, 0, 0)),
        ],
        out_specs=[pl.BlockSpec((tm, d), lambda i: (i, 0)), tab_spec, tab_spec, grp_spec, grp_spec],
        out_shape=[jax.ShapeDtypeStruct((t, d), BF16), tab(BF16), tab(BF16), grp, grp],
        scratch_shapes=[pltpu.VMEM((nq, tm), BF16),
                        pltpu.VMEM((tm // 128, PEER_TOPK, 128), F32),
                        pltpu.VMEM((tm // 128, PEER_TOPK, 128), F32),
                        pltpu.VMEM((PEER_TOPK, tm), F32)],
        compiler_params=_params("arbitrary"),
        name="peer_select",
    )(h, g, shift, scale, wq_t, keys)


def _peer_mix_kernel(xf_ref, u_ref, vt_ref, r2_ref, a2_ref, lim_ref, a1_ref, h_ref, g_ref, *rest,
                     n_blocks, convert):
    if convert:
        cu_ref, cv_ref, o_ref, ou_ref, ovt_ref, acc_ref, hid_a_ref, hid_b_ref, p_ref = rest
    else:
        o_ref, acc_ref, hid_a_ref, hid_b_ref, p_ref = rest
    g = pl.program_id(0)
    eb, tc = hid_a_ref.shape
    rows_per_step = eb // N_SUBKEYS
    mix_block = jnp.maximum(g - 1, 0) % n_blocks

    @pl.when(mix_block == 0)
    def _():
        acc_ref[...] = jnp.zeros_like(acc_ref)

    @pl.when(g == 0)
    def _():
        hid_b_ref[...] = jnp.zeros_like(hid_b_ref)

    def step(cur_ref, prev_ref):
        if convert:
            _convert_expert_slice(cu_ref, cv_ref, ou_ref, ovt_ref)
        cur_ref[...] = lax.dot_general(u_ref[...], xf_ref[...], NT_DIMS, preferred_element_type=F32)
        for kk in range(rows_per_step):
            gate = None
            for h in range(PEER_HEADS):
                lim = jnp.broadcast_to(lim_ref[h, 0, kk:kk + 1, :], (16, tc)).astype(BF16)
                a1 = jnp.broadcast_to(a1_ref[h, 0, kk:kk + 1, :], (16, tc)).astype(BF16)
                lim = jnp.concatenate([lim] * (N_SUBKEYS // 16), axis=0)
                a1 = jnp.concatenate([a1] * (N_SUBKEYS // 16), axis=0)
                w = jnp.where(r2_ref[h] < lim, a2_ref[h] * a1, jnp.zeros((), BF16))
                gate = w if gate is None else gate + w
            sl = slice(kk * N_SUBKEYS, (kk + 1) * N_SUBKEYS)
            hid = prev_ref[sl, :]
            act = 0.5 * hid * (1.0 + lax.erf(hid * INV_SQRT2))
            p_ref[sl, :] = act.astype(BF16) * gate
        acc_ref[...] += jnp.dot(vt_ref[...], p_ref[...], preferred_element_type=F32)

    @pl.when(g % 2 == 0)
    def _():
        step(hid_a_ref, hid_b_ref)

    @pl.when(g % 2 == 1)
    def _():
        step(hid_b_ref, hid_a_ref)

    @pl.when(jnp.logical_and(mix_block == n_blocks - 1, g > 0))
    def _():
        o_ref[...] = h_ref[...] + g_ref[0] * acc_ref[...].T


def peer_mix(xf, u, vt, r2, a2, lim, a1, h, gate, *, seq, next_tables=None, tc=512):
    t, d = h.shape
    rows = lim.shape[2]
    eb = rows * N_SUBKEYS
    n_blocks = u.shape[0] // eb
    n_items = (t // tc) * n_blocks
    per_batch = seq // tc

    def hid_item(g):
        return jnp.minimum(g, n_items - 1)

    def mix_item(g):
        return jnp.maximum(g - 1, 0)

    tab_spec = pl.BlockSpec((PEER_HEADS, N_SUBKEYS, tc), lambda g: (0, 0, mix_item(g) // n_blocks))
    row_spec = pl.BlockSpec((PEER_HEADS, 1, rows, tc),
                            lambda g: (0, mix_item(g) % n_blocks, 0, mix_item(g) // n_blocks))
    chunk_spec = pl.BlockSpec((tc, d), lambda g: (mix_item(g) // n_blocks, 0))
    cv_in, cv_out, cv_shape, cv_args = [], [], [], []
    if next_tables is not None:
        peer_u, peer_v, layer = next_tables
        assert n_items + 1 >= peer_u.shape[1] // CONVERT_ROWS
        cv_in, cv_out, cv_shape = _convert_specs(peer_u, peer_v, layer, lambda g: g)
        cv_args = [peer_u, peer_v]
    out = pl.pallas_call(
        functools.partial(_peer_mix_kernel, n_blocks=n_blocks, convert=bool(cv_args)),
        grid=(n_items + 1,),
        in_specs=[
            pl.BlockSpec((tc, d), lambda g: (hid_item(g) // n_blocks, 0)),
            pl.BlockSpec((eb, d), lambda g: (hid_item(g) % n_blocks, 0)),
            pl.BlockSpec((d, eb), lambda g: (0, mix_item(g) % n_blocks)),
            tab_spec, tab_spec, row_spec, row_spec,
            chunk_spec,
            pl.BlockSpec((1, 1, d), lambda g: (mix_item(g) // n_blocks // per_batch, 0, 0)),
        ] + cv_in,
        out_specs=[chunk_spec] + cv_out,
        out_shape=[jax.ShapeDtypeStruct((t, d), F32)] + cv_shape,
        scratch_shapes=[pltpu.VMEM((d, tc), F32),
                        pltpu.VMEM((eb, tc), F32),
                        pltpu.VMEM((eb, tc), F32),
                        pltpu.VMEM((eb, tc), BF16)],
        compiler_params=_params("arbitrary"),
        name="peer_mix",
    )(xf, u, vt, r2, a2, lim, a1, h, gate, *cv_args)
    return out if cv_args else out[0]


def kernel(x, c, rel_bias, norm1_g, norm2_g, w_ada, b_ada, a_w_in, a_q_gain, a_k_gain,
           b_w_in, w_out, peer_w_query, peer_sub_keys, peer_u, peer_v):
    b, s, d = x.shape
    depth = w_ada.shape[0]
    n_groups = len(DIL_GROUPS)
    heads = d // HEAD_DIM

    mod = adaln(c, w_ada, b_ada).reshape(depth, b, 6, 1, d)
    a_w, b_w, w_o = a_w_in.astype(BF16), b_w_in.astype(BF16), w_out.astype(BF16)
    wq_t = jnp.swapaxes(peer_w_query, 1, 2).astype(BF16)
    keys = peer_sub_keys.astype(BF16)
    u = vt = None
    h = x.reshape(b * s, d)
    for i in range(depth):
        sh1, sc1, g1, sh2, sc2, g2 = (mod[i, :, k] for k in range(6))
        j = i // 2
        if i % 2 == 0:
            ones = jnp.ones((HEAD_DIM,), F32)
            gain = jnp.stack([jnp.tile(row, heads) for g in range(n_groups)
                              for row in (a_q_gain[j, g] / math.sqrt(HEAD_DIM), a_k_gain[j, g], ones)])
            qkvs, u0, vt0 = in_proj_dilated(h, norm1_g[i][None], sh1, sc1, a_w, gain[:, None, :],
                                            peer_u, peer_v, layer=j, peer_layer=i, seq=s)
            if u is None:
                u, vt = u0, vt0
            outs, lses = [], []
            for g, (window, dil) in enumerate(DIL_GROUPS):
                bias = _band_bias(rel_bias[:, g * heads:(g + 1) * heads], window, dil)
                o, lse = dilated_group_attention(qkvs[g], bias, n_heads=heads)
                outs.append(o)
                lses.append(lse)
            h = out_proj_merge(outs, lses, w_o, h, g1, layer=i, seq=s)
        else:
            qkv = in_proj(h, norm1_g[i][None], sh1, sc1, b_w, layer=j, seq=s)
            y = stick_breaking_attention(qkv.reshape(b, s, -1), n_heads=heads)
            h = out_proj(y.reshape(b * s, d), w_o, h, g1, layer=i, seq=s)
        xf, r2, a2, lim, a1 = peer_select(h, norm2_g[i][None], sh2, sc2, wq_t, keys, layer=i, seq=s,
                                          rows=PEER_BLOCK_ROWS)
        if i + 1 < depth:
            h, u, vt = peer_mix(xf, u, vt, r2, a2, lim, a1, h, g2, seq=s,
                                next_tables=(peer_u, peer_v, i + 1))
        else:
            h = peer_mix(xf, u, vt, r2, a2, lim, a1, h, g2, seq=s)
    return h.reshape(b, s, d)
```

```python
import functools
import math

import jax
import jax.numpy as jnp
from jax import lax
from jax.experimental import pallas as pl
from jax.experimental.pallas import tpu as pltpu

F32 = jnp.float32
BF16 = jnp.bfloat16

HEAD_DIM = 128
BLOCK = 128
DIL_GROUPS = ((128, 1), (512, 4), (2048, 16))
REL_BUCKETS = 32
REL_MAX_DIST = 2048
PEER_HEADS = 8
N_SUBKEYS = 128
PEER_TOPK = 16
PEER_BLOCK_ROWS = 8
EPS = 1e-6
NEG_BIG = -1e30
SB_DEAD_DECAY = 104.0
SB_HEADS_PER_STEP = 8
SB_Q_CHUNK = 1024
INV_SQRT2 = 0.7071067811865476

VMEM_LIMIT = 60 * 1024 * 1024

NT_DIMS = (((1,), (1,)), ((), ()))


def _params(*sem):
    return pltpu.CompilerParams(dimension_semantics=sem, vmem_limit_bytes=VMEM_LIMIT)


def _adaln_kernel(c_ref, w_lo_ref, w_hi_ref, b_ref, o_ref):
    c = c_ref[...]
    cs = c / (1.0 + jnp.exp(-c))
    half = w_lo_ref.shape[1]
    dot = functools.partial(jnp.dot, preferred_element_type=F32, precision=lax.Precision.HIGHEST)
    o_ref[0] = dot(cs[:, :half], w_lo_ref[0]) + dot(cs[:, half:], w_hi_ref[0]) + b_ref[0]


def adaln(c, w_ada, b_ada, *, tn=1024):
    depth, d, n = w_ada.shape
    b = c.shape[0]
    rows = 8
    c_pad = jnp.pad(c, ((0, rows - b), (0, 0)))
    out = pl.pallas_call(
        _adaln_kernel,
        grid=(depth, n // tn),
        in_specs=[
            pl.BlockSpec((rows, d), lambda i, j: (0, 0)),
            pl.BlockSpec((1, d // 2, tn), lambda i, j: (i, 0, j)),
            pl.BlockSpec((1, d // 2, tn), lambda i, j: (i, 1, j)),
            pl.BlockSpec((1, 1, tn), lambda i, j: (i, 0, j)),
        ],
        out_specs=pl.BlockSpec((1, rows, tn), lambda i, j: (i, 0, j)),
        out_shape=jax.ShapeDtypeStruct((depth, rows, n), F32),
        compiler_params=_params("arbitrary", "arbitrary"),
        name="adaln",
    )(c_pad, w_ada, w_ada, b_ada.reshape(depth, 1, n))
    return out[:, :b]


CONVERT_ROWS = 128


def _convert_expert_slice(cu_ref, cv_ref, ou_ref, ovt_ref):
    ou_ref[...] = cu_ref[...].astype(BF16)
    ovt_ref[...] = cv_ref[...].T.astype(BF16)


def _convert_specs(peer_u, peer_v, layer, step_of):
    n_exp, d = peer_u.shape[1:]
    last = n_exp // CONVERT_ROWS - 1

    def slice_of(*idx):
        return jnp.minimum(step_of(*idx), last)

    src = pl.BlockSpec((None, CONVERT_ROWS, d), lambda *idx: (layer, slice_of(*idx), 0))
    in_specs = [src, src]
    out_specs = [pl.BlockSpec((CONVERT_ROWS, d), lambda *idx: (slice_of(*idx), 0)),
                 pl.BlockSpec((d, CONVERT_ROWS), lambda *idx: (0, slice_of(*idx)))]
    out_shape = [jax.ShapeDtypeStruct((n_exp, d), BF16), jax.ShapeDtypeStruct((d, n_exp), BF16)]
    return in_specs, out_specs, out_shape


def _norm_modulate(x, g, shift, scale):
    ms = jnp.mean(x * x, axis=-1, keepdims=True)
    y = x * lax.rsqrt(ms + EPS) * g
    return y * (1.0 + scale) + shift


def _in_proj_kernel(h_ref, g_ref, sh_ref, sc_ref, w_ref, o_ref, xm_ref):
    @pl.when(pl.program_id(1) == 0)
    def _():
        xm_ref[...] = _norm_modulate(h_ref[...], g_ref[...], sh_ref[0], sc_ref[0]).astype(BF16)

    o_ref[...] = jnp.dot(xm_ref[...], w_ref[...], preferred_element_type=F32).astype(o_ref.dtype)


def in_proj(h, g, shift, scale, w, *, layer, seq, tm=1024, tn=2048):
    t, d = h.shape
    n = w.shape[2]
    per_batch = seq // tm
    return pl.pallas_call(
        _in_proj_kernel,
        grid=(t // tm, n // tn),
        in_specs=[
            pl.BlockSpec((tm, d), lambda i, j: (i, 0)),
            pl.BlockSpec((1, d), lambda i, j: (0, 0)),
            pl.BlockSpec((1, 1, d), lambda i, j: (i // per_batch, 0, 0)),
            pl.BlockSpec((1, 1, d), lambda i, j: (i // per_batch, 0, 0)),
            pl.BlockSpec((None, d, tn), lambda i, j: (layer, 0, j)),
        ],
        out_specs=pl.BlockSpec((tm, tn), lambda i, j: (i, j)),
        out_shape=jax.ShapeDtypeStruct((t, n), BF16),
        scratch_shapes=[pltpu.VMEM((tm, d), BF16)],
        compiler_params=_params("arbitrary", "arbitrary"),
        name="in_proj",
    )(h, g, shift, scale, w)


def _in_proj_dilated_kernel(h_ref, g_ref, sh_ref, sc_ref, w_ref, gain_ref, cu_ref, cv_ref,
                            o0_ref, o1_ref, o2_ref, ou_ref, ovt_ref,
                            xm0_ref, xm1_ref, xm2_ref, slab_ref, *, parts):
    j = pl.program_id(1)
    seg = j // parts
    tm, d = xm0_ref.shape
    xm_refs = (xm0_ref, xm1_ref, xm2_ref)

    @pl.when(j == 0)
    def _():
        xm = _norm_modulate(h_ref[...], g_ref[...], sh_ref[0], sc_ref[0])
        for sb in range(d // HEAD_DIM):
            slab_ref[sb] = xm[:, sb * HEAD_DIM:(sb + 1) * HEAD_DIM]
        for (_, dil), xm_ref in zip(DIL_GROUPS, xm_refs):
            if dil == 1:
                xm_ref[...] = xm.astype(BF16)
                continue
            rows = tm // dil
            for sb in range(d // HEAD_DIM):
                for r in range(dil):
                    xm_ref[r * rows:(r + 1) * rows, sb * HEAD_DIM:(sb + 1) * HEAD_DIM] = (
                        slab_ref[sb, pl.ds(r, rows, stride=dil), :].astype(BF16))

    def project(xm_ref, o_ref, dil):
        rows = tm // dil
        slab = min(2 * HEAD_DIM, w_ref.shape[1])
        n_slabs = w_ref.shape[1] // slab

        @pl.when(seg % 3 == 2)
        def _():
            _convert_expert_slice(cu_ref, cv_ref, ou_ref, ovt_ref)
            for sb in range(n_slabs):
                cols = slice(sb * slab, (sb + 1) * slab)
                acc = jnp.dot(xm_ref[...], w_ref[:, cols], preferred_element_type=F32).astype(BF16)
                for r in range(dil):
                    o_ref[0, r, :, cols] = acc[r * rows:(r + 1) * rows]

        @pl.when(seg % 3 != 2)
        def _():
            _convert_expert_slice(cu_ref, cv_ref, ou_ref, ovt_ref)
            for sb in range(n_slabs):
                acc = jnp.dot(xm_ref[...], w_ref[:, sb * slab:(sb + 1) * slab], preferred_element_type=F32)
                for hh in range(slab // HEAD_DIM):
                    sl = slice(sb * slab + hh * HEAD_DIM, sb * slab + (hh + 1) * HEAD_DIM)
                    a = acc[:, hh * HEAD_DIM:(hh + 1) * HEAD_DIM]
                    ms = jnp.mean(a * a, axis=-1, keepdims=True)
                    y = (a * lax.rsqrt(ms + EPS) * gain_ref[0, :, sl]).astype(BF16)
                    for r in range(dil):
                        o_ref[0, r, :, sl] = y[r * rows:(r + 1) * rows]

    for grp, ((_, dil), xm_ref, o_ref) in enumerate(zip(DIL_GROUPS, xm_refs, (o0_ref, o1_ref, o2_ref))):
        @pl.when(seg // 3 == grp)
        def _(dil=dil, xm_ref=xm_ref, o_ref=o_ref):
            project(xm_ref, o_ref, dil)


def in_proj_dilated(h, g, shift, scale, w, gain, peer_u, peer_v, *, layer, peer_layer, seq, tm=512, tn=2048):
    t, d = h.shape
    n = w.shape[2]
    b = t // seq
    per_batch = seq // tm
    n_groups = len(DIL_GROUPS)
    width = gain.shape[2]
    parts = width // tn
    assert n == n_groups * 3 * width

    def out_spec(grp, dil):
        def imap(i, j):
            return (i // per_batch, 0, i % per_batch, jnp.clip(j - 3 * parts * grp, 0, 3 * parts - 1))
        return pl.BlockSpec((1, dil, tm // dil, tn), imap)

    n_cols = n // tn
    assert (t // tm) * n_cols >= peer_u.shape[1] // CONVERT_ROWS
    cv_in, cv_out, cv_shape = _convert_specs(peer_u, peer_v, peer_layer, lambda i, j: i * n_cols + j)
    *qkvs, u_bf, vt_bf = pl.pallas_call(
        functools.partial(_in_proj_dilated_kernel, parts=parts),
        grid=(t // tm, n // tn),
        in_specs=[
            pl.BlockSpec((tm, d), lambda i, j: (i, 0)),
            pl.BlockSpec((1, d), lambda i, j: (0, 0)),
            pl.BlockSpec((1, 1, d), lambda i, j: (i // per_batch, 0, 0)),
            pl.BlockSpec((1, 1, d), lambda i, j: (i // per_batch, 0, 0)),
            pl.BlockSpec((None, d, tn), lambda i, j: (layer, 0, j)),
            pl.BlockSpec((1, 1, tn), lambda i, j: (j // parts, 0, j % parts)),
        ] + cv_in,
        out_specs=[out_spec(grp, dil) for grp, (_, dil) in enumerate(DIL_GROUPS)] + cv_out,
        out_shape=[jax.ShapeDtypeStruct((b, dil, seq // dil, 3 * width), BF16)
                   for _, dil in DIL_GROUPS] + cv_shape,
        scratch_shapes=[pltpu.VMEM((tm, d), BF16)] * 3 + [pltpu.VMEM((d // HEAD_DIM, tm, HEAD_DIM), F32)],
        compiler_params=_params("arbitrary", "arbitrary"),
        name="in_proj_dilated",
    )(h, g, shift, scale, w, gain, peer_u, peer_v)
    return qkvs, u_bf, vt_bf


def _dil_attn_kernel(q_ref, kp_ref, kc_ref, vp_ref, vc_ref, bias_ref, o_ref, lse_ref, *, n_heads):
    n = pl.program_id(2)
    n_sub = q_ref.shape[2] // BLOCK
    lane = lax.broadcasted_iota(jnp.int32, (1, 2 * BLOCK), 1)
    pen = jnp.where(jnp.logical_and(lane < BLOCK, n == 0), NEG_BIG, 0.0).astype(F32)
    ones = jnp.ones((2 * BLOCK, HEAD_DIM), BF16)
    lse_lanes = BLOCK // n_heads
    lane_head = lax.broadcasted_iota(jnp.int32, (1, BLOCK), 1) // lse_lanes
    for sb in range(n_sub):
        rows = slice(sb * BLOCK, (sb + 1) * BLOCK)
        prev = slice((sb - 1) * BLOCK, sb * BLOCK)
        lse_pack = jnp.zeros((BLOCK, BLOCK), F32)
        for hh in range(n_heads):
            sl = slice(hh * HEAD_DIM, (hh + 1) * HEAD_DIM)
            q = q_ref[0, 0, rows, sl]
            k_prev = kp_ref[0, 0, :, sl] if sb == 0 else kc_ref[0, 0, prev, sl]
            v_prev = vp_ref[0, 0, :, sl] if sb == 0 else vc_ref[0, 0, prev, sl]
            k = jnp.concatenate([k_prev, kc_ref[0, 0, rows, sl]], axis=0)
            v = jnp.concatenate([v_prev, vc_ref[0, 0, rows, sl]], axis=0)
            logits = lax.dot_general(q, k, NT_DIMS, preferred_element_type=F32)
            logits = logits + bias_ref[hh]
            if sb == 0:
                logits = logits + pen
            m = jnp.max(logits, axis=-1, keepdims=True)
            p = jnp.exp(logits - m).astype(BF16)
            v_ext = jnp.concatenate([v, ones], axis=1)
            pv = jnp.dot(p, v_ext, preferred_element_type=F32)
            denom = pv[:, HEAD_DIM:]
            o_ref[0, 0, rows, sl] = (pv[:, :HEAD_DIM] / denom).astype(o_ref.dtype)
            lse_pack = jnp.where(lane_head == hh, m + jnp.log(denom), lse_pack)
        lse_ref[0, 0, rows, :] = lse_pack


def dilated_group_attention(qkv, bias, *, n_heads, q_blocks=2):
    b, dil, sub_len, c = qkv.shape
    assert BLOCK % n_heads == 0
    width = n_heads * HEAD_DIM
    q_blocks = min(q_blocks, sub_len // BLOCK)
    rows = q_blocks * BLOCK
    steps = sub_len // rows

    def cur(which):
        return pl.BlockSpec((1, 1, rows, width), lambda bi, r, n: (bi, r, n, which))

    def prev(which):
        return pl.BlockSpec((1, 1, BLOCK, width),
                            lambda bi, r, n: (bi, r, jnp.maximum(q_blocks * n - 1, 0), which))

    return pl.pallas_call(
        functools.partial(_dil_attn_kernel, n_heads=n_heads),
        grid=(b, dil, steps),
        in_specs=[cur(0), prev(1), cur(1), prev(2), cur(2),
                  pl.BlockSpec((n_heads, BLOCK, 2 * BLOCK), lambda bi, r, n: (0, 0, 0))],
        out_specs=[pl.BlockSpec((1, 1, rows, width), lambda bi, r, n: (bi, r, n, 0)),
                   pl.BlockSpec((1, 1, rows, BLOCK), lambda bi, r, n: (bi, r, n, 0))],
        out_shape=[jax.ShapeDtypeStruct((b, dil, sub_len, width), BF16),
                   jax.ShapeDtypeStruct((b, dil, sub_len, BLOCK), F32)],
        compiler_params=_params("arbitrary", "arbitrary", "arbitrary"),
        name=f"dilated_attn_d{dil}",
    )(qkv, qkv, qkv, qkv, qkv, bias)


def _rel_bucket(dist):
    exact = REL_BUCKETS // 2
    d_f = jnp.maximum(dist, exact).astype(F32)
    log_b = exact + (jnp.log(d_f / exact) / math.log(REL_MAX_DIST / exact)
                     * (REL_BUCKETS - exact)).astype(jnp.int32)
    return jnp.where(dist < exact, dist, jnp.minimum(log_b, REL_BUCKETS - 1))


def _band_bias(table_g, window, dil):
    w_steps = window // dil
    period = 3 * BLOCK
    per_delta = table_g[_rel_bucket(jnp.arange(w_steps + 1) * dil)].astype(F32)
    row = jnp.full((period, table_g.shape[1]), NEG_BIG, F32)
    row = row.at[BLOCK - w_steps:BLOCK + 1].set(per_delta[::-1])
    flat = jnp.tile(row.T, (1, BLOCK))[:, :BLOCK * (period - 1)]
    return flat.reshape(-1, BLOCK, period - 1)[:, :, :2 * BLOCK]


def _out_proj_merge_kernel(o0_ref, o1_ref, o2_ref, l0_ref, l1_ref, l2_ref, w_ref, h_ref, g_ref, out_ref,
                           os1_ref, os2_ref, ls1_ref, ls2_ref, ya_ref, yb_ref):
    g = pl.program_id(0)
    tm = out_ref.shape[0]
    n_heads = ya_ref.shape[1] // HEAD_DIM

    @pl.when(g == 0)
    def _():
        yb_ref[...] = jnp.zeros_like(yb_ref)

    def to_token_order(src_ref, dst_ref, sl):
        dil = src_ref.shape[1]
        for r in range(dil):
            dst_ref[pl.ds(r, tm // dil, stride=dil), :] = src_ref[0, r, :, sl].astype(F32)

    def step(y_ref, prev_y_ref):
        to_token_order(l1_ref, ls1_ref, slice(None))
        to_token_order(l2_ref, ls2_ref, slice(None))
        l0, l1, l2 = l0_ref[0, 0], ls1_ref[...], ls2_ref[...]
        m = jnp.maximum(jnp.maximum(l0, l1), l2)
        e0, e1, e2 = jnp.exp(l0 - m), jnp.exp(l1 - m), jnp.exp(l2 - m)
        inv = 1.0 / (e0 + e1 + e2)
        a0, a1, a2 = e0 * inv, e1 * inv, e2 * inv
        for hh in range(n_heads):
            sl = slice(hh * HEAD_DIM, (hh + 1) * HEAD_DIM)
            col = slice(hh * (BLOCK // n_heads), hh * (BLOCK // n_heads) + 1)
            to_token_order(o1_ref, os1_ref, sl)
            to_token_order(o2_ref, os2_ref, sl)
            y = (a0[:, col] * o0_ref[0, 0, :, sl].astype(F32) + a1[:, col] * os1_ref[...]
                 + a2[:, col] * os2_ref[...])
            y_ref[:, sl] = y.astype(BF16)
        proj = jnp.dot(prev_y_ref[...], w_ref[...], preferred_element_type=F32)
        out_ref[...] = h_ref[...] + g_ref[0] * proj

    @pl.when(g % 2 == 0)
    def _():
        step(ya_ref, yb_ref)

    @pl.when(g % 2 == 1)
    def _():
        step(yb_ref, ya_ref)


def out_proj_merge(outs, lses, w, h, gate, *, layer, seq, tm=256):
    t, d = h.shape
    width = w.shape[1]
    per_batch = seq // tm
    n_steps = t // tm

    def merged(g):
        return jnp.minimum(g, n_steps - 1)

    def projected(g):
        return jnp.maximum(g - 1, 0)

    row = pl.BlockSpec((tm, d), lambda g: (projected(g), 0))

    def grp_spec(a):
        dil, cols = a.shape[1], a.shape[3]
        return pl.BlockSpec((1, dil, tm // dil, cols),
                            lambda g: (merged(g) // per_batch, 0, merged(g) % per_batch, 0))

    return pl.pallas_call(
        _out_proj_merge_kernel,
        grid=(n_steps + 1,),
        in_specs=[grp_spec(a) for a in (*outs, *lses)] + [
            pl.BlockSpec((None, width, d), lambda g: (layer, 0, 0)),
            row,
            pl.BlockSpec((1, 1, d), lambda g: (projected(g) // per_batch, 0, 0)),
        ],
        out_specs=row,
        out_shape=jax.ShapeDtypeStruct((t, d), F32),
        scratch_shapes=[pltpu.VMEM((tm, HEAD_DIM), F32)] * 4 + [pltpu.VMEM((tm, width), BF16)] * 2,
        compiler_params=_params("arbitrary"),
        name="out_proj_merge",
    )(*outs, *lses, w, h, gate)


def _out_proj_kernel(y_ref, w_ref, h_ref, g_ref, out_ref):
    proj = jnp.dot(y_ref[...], w_ref[...], preferred_element_type=F32)
    out_ref[...] = h_ref[...] + g_ref[0] * proj


def out_proj(y, w, h, gate, *, layer, seq, tm=512):
    t, d = h.shape
    width = w.shape[1]
    per_batch = seq // tm
    row = lambda cols: pl.BlockSpec((tm, cols), lambda i: (i, 0))
    return pl.pallas_call(
        _out_proj_kernel,
        grid=(t // tm,),
        in_specs=[row(width), pl.BlockSpec((None, width, d), lambda i: (layer, 0, 0)), row(d),
                  pl.BlockSpec((1, 1, d), lambda i: (i // per_batch, 0, 0))],
        out_specs=row(d),
        out_shape=jax.ShapeDtypeStruct((t, d), F32),
        compiler_params=_params("arbitrary"),
        name="out_proj",
    )(y, w, h, gate)


def _split2(x):
    hi = x.astype(BF16)
    lo = (x - hi.astype(F32)).astype(BF16)
    return hi, lo


def _sb_kernel(q_ref, k_ref, v_ref, o_ref, acc_ref, decay_ref, *, scale):
    nq = q_ref.shape[1] // BLOCK
    q_base = pl.program_id(2) * q_ref.shape[1]
    n_heads = acc_ref.shape[0]
    row = lax.broadcasted_iota(jnp.int32, (BLOCK, BLOCK), 0)
    col = lax.broadcasted_iota(jnp.int32, (BLOCK, BLOCK), 1)
    before = col < row

    def tri_ext(n_keys):
        r = lax.broadcasted_iota(jnp.int32, (n_keys, n_keys), 0)
        c = lax.broadcasted_iota(jnp.int32, (n_keys, n_keys), 1)
        tri = jnp.where(r > c, 1.0, 0.0).astype(BF16)
        return jnp.concatenate([tri, jnp.ones((n_keys, BLOCK), BF16)], axis=1)

    tri1, tri2 = tri_ext(BLOCK), tri_ext(2 * BLOCK)

    def tails_and_sums(sps, tri):
        n_keys = tri.shape[0]
        parts = [t for sp in sps for t in _split2(sp)]
        r = jnp.dot(jnp.concatenate(parts, axis=0), tri, preferred_element_type=F32)
        out = []
        for g in range(len(sps)):
            hi = r[(2 * g) * BLOCK:(2 * g + 1) * BLOCK]
            lo = r[(2 * g + 1) * BLOCK:(2 * g + 2) * BLOCK]
            both = hi + lo
            out.append((both[:, :n_keys], both[:, n_keys:]))
        return out

    def softplus(z):
        return jnp.maximum(z, 0.0) + jnp.log1p(jnp.exp(-jnp.abs(z)))

    def logits(q0, k0, g, n_keys=BLOCK):
        cs = slice(g * HEAD_DIM, (g + 1) * HEAD_DIM)
        return lax.dot_general(q_ref[0, pl.ds(q0, BLOCK), cs], k_ref[0, pl.ds(k0, n_keys), cs],
                               NT_DIMS, preferred_element_type=F32) * scale

    def values(k0, g, n_keys=BLOCK):
        return v_ref[0, pl.ds(k0, n_keys), g * HEAD_DIM:(g + 1) * HEAD_DIM]

    def qblock(i, carry):
        q0 = pl.multiple_of(i * BLOCK, BLOCK)
        d0 = pl.multiple_of(q_base + q0, BLOCK)
        zs = [logits(q0, d0, g) for g in range(n_heads)]
        sps = [softplus(z) for z in zs]
        ts = tails_and_sums([jnp.where(before, sp, 0.0) for sp in sps], tri1)
        live = None
        for g in range(n_heads):
            tail, rsum = ts[g]
            a = jnp.where(before, jnp.exp(zs[g] - sps[g] - tail), 0.0)
            acc_ref[g] = jnp.dot(a.astype(BF16), values(d0, g), preferred_element_type=F32)
            decay_ref[g] = rsum
            lo = jnp.min(rsum)
            live = lo if live is None else jnp.minimum(live, lo)

        def walk(k0, tri):
            n_keys = tri.shape[0]
            zs = [logits(q0, k0, g, n_keys) for g in range(n_heads)]
            sps = [softplus(z) for z in zs]
            ts = tails_and_sums(sps, tri)
            live = None
            for g in range(n_heads):
                tail, rsum = ts[g]
                decay = decay_ref[g]
                later = jnp.concatenate([decay] * (n_keys // BLOCK), axis=1)
                a = jnp.exp(zs[g] - sps[g] - (later + tail))
                acc_ref[g] += jnp.dot(a.astype(BF16), values(k0, g, n_keys), preferred_element_type=F32)
                decay = decay + rsum
                decay_ref[g] = decay
                lo = jnp.min(decay)
                live = lo if live is None else jnp.minimum(live, lo)
            return live

        def cond(c):
            j, live = c
            return jnp.logical_and(j >= 1, live < SB_DEAD_DECAY)

        def body(c):
            j, _ = c
            return j - 2, walk(pl.multiple_of((j - 1) * BLOCK, BLOCK), tri2)

        j, live = lax.while_loop(cond, body, (d0 // BLOCK - 1, live))

        @pl.when(jnp.logical_and(j == 0, live < SB_DEAD_DECAY))
        def _():
            walk(0, tri1)

        for g in range(n_heads):
            o_ref[0, pl.ds(q0, BLOCK), g * HEAD_DIM:(g + 1) * HEAD_DIM] = acc_ref[g].astype(o_ref.dtype)
        return carry

    lax.fori_loop(0, nq, qblock, 0)


def stick_breaking_attention(qkv, *, n_heads):
    b, s, _ = qkv.shape
    hps = min(SB_HEADS_PER_STEP, n_heads)
    steps = n_heads // hps
    width = hps * HEAD_DIM
    qc = min(SB_Q_CHUNK, s)

    def spec(which):
        return pl.BlockSpec((1, s, width), lambda bi, h, c: (bi, 0, which * steps + h))

    return pl.pallas_call(
        functools.partial(_sb_kernel, scale=1.0 / math.sqrt(HEAD_DIM)),
        grid=(b, steps, s // qc),
        in_specs=[pl.BlockSpec((1, qc, width), lambda bi, h, c: (bi, c, h)), spec(1), spec(2)],
        out_specs=pl.BlockSpec((1, qc, width), lambda bi, h, c: (bi, c, h)),
        out_shape=jax.ShapeDtypeStruct((b, s, n_heads * HEAD_DIM), BF16),
        scratch_shapes=[pltpu.VMEM((hps, BLOCK, HEAD_DIM), F32),
                        pltpu.VMEM((hps, BLOCK, BLOCK), F32)],
        compiler_params=_params("arbitrary", "arbitrary", "arbitrary"),
        name="stick_breaking",
    )(qkv, qkv, qkv)


UNRANKED = 127.0
RANK_MARK = -(2.0 ** 100)


def _as_f32(r):
    return jnp.asarray(r).astype(F32)


def _extract_ranked(s, dst_ref, count):
    n = s.shape[0]
    idx = lax.broadcasted_iota(jnp.int32, s.shape, 0).astype(F32)

    def body(r, carry):
        s, rank = carry
        m = jnp.max(s, axis=0, keepdims=True)
        first = jnp.min(jnp.where(s == m, idx, float(n)), axis=0, keepdims=True)
        pick = idx == first
        dst_ref[pl.ds(r, 1), :] = m
        return jnp.where(pick, -jnp.inf, s), jnp.where(pick, _as_f32(r), rank)

    _, rank = lax.fori_loop(0, count, body, (s, jnp.full(s.shape, UNRANKED, F32)))
    return rank


def _candidates(sv1, sv2):
    k = PEER_TOPK
    groups = [sv1[0:1] + sv2]
    groups += [sv1[a:a + 1] + sv2[0:k // 2] for a in range(1, k // 2)]
    groups += [sv1[k // 2:] + sv2[0:1]]
    spans = [(0, k)] + [(k + (a - 1) * (k // 2), k // 2) for a in range(1, k // 2)]
    tail0 = k + (k // 2 - 1) * (k // 2)
    spans += [(tail0 + a, 1) for a in range(k // 2)]
    return jnp.concatenate(groups, axis=0), spans


def _staircase_counts(taken, spans):
    return [jnp.sum(taken[lo:lo + n], axis=0, keepdims=True) for lo, n in spans]


def _peer_select_kernel(h_ref, g_ref, sh_ref, sc_ref, wq_ref, keys_ref,
                        xf_ref, r2_ref, a2_ref, lim_ref, a1_ref,
                        qt_ref, sv1_tiles, sv2_tiles, best_ref):
    xf = _norm_modulate(h_ref[...], g_ref[...], sh_ref[0], sc_ref[0]).astype(BF16)
    xf_ref[...] = xf
    qt_ref[...] = lax.dot_general(wq_ref[...], xf, NT_DIMS, preferred_element_type=F32).astype(BF16)
    k = PEER_TOPK
    half = N_SUBKEYS
    kf = float(k)
    n_tiles = sv2_tiles.shape[0]

    def head(h, carry):
        r0 = pl.multiple_of(h * 2 * half, 2 * half)
        s1 = jnp.dot(keys_ref[h, 0], qt_ref[pl.ds(r0, half), :], preferred_element_type=F32)
        s2 = jnp.dot(keys_ref[h, 1], qt_ref[pl.ds(r0 + half, half), :], preferred_element_type=F32)

        def sorted1():
            return jnp.concatenate([sv1_tiles[lt] for lt in range(n_tiles)], axis=1)

        def sorted2():
            return jnp.concatenate([sv2_tiles[lt] for lt in range(n_tiles)], axis=1)

        def emit(rank2, lim):
            sv1, sv2, best = sorted1(), sorted2(), best_ref[...]
            z = jnp.sum(jnp.exp(best - best[0:1]), axis=0, keepdims=True)
            r2_ref[h] = rank2.astype(BF16)
            a2_ref[h] = jnp.exp(s2 - sv2[0:1]).astype(BF16)
            a1 = jnp.exp(s1 - sv1[0:1]) / z
            rows = lim_ref.shape[2]
            for grp in range(half // rows):
                lim_ref[h, grp] = lim[grp * rows:(grp + 1) * rows]
                a1_ref[h, grp] = a1[grp * rows:(grp + 1) * rows]

        marked = []
        for lt in range(n_tiles):
            def round12(r, c, lt=lt):
                w1, w2 = c
                m1 = jnp.max(w1, axis=0, keepdims=True)
                m2 = jnp.max(w2, axis=0, keepdims=True)
                sv1_tiles[lt, pl.ds(r, 1), :] = m1
                sv2_tiles[lt, pl.ds(r, 1), :] = m2
                mark = RANK_MARK * (1.0 + _as_f32(r) / kf)
                return jnp.where(w1 == m1, -jnp.inf, w1), jnp.where(w2 == m2, mark, w2)

            ls = slice(lt * 128, (lt + 1) * 128)
            marked.append(lax.fori_loop(0, k, round12, (s1[:, ls], s2[:, ls]), unroll=True)[1])
        w2 = jnp.concatenate(marked, axis=1)
        rank2 = jnp.where(w2 <= RANK_MARK, w2 * (kf / RANK_MARK) - kf, UNRANKED)

        sv1 = sorted1()
        cand, spans = _candidates(sv1, sorted2())

        def round_c(r, w):
            m = jnp.max(w, axis=0, keepdims=True)
            best_ref[pl.ds(r, 1), :] = m
            return jnp.where(w == m, -jnp.inf, w)

        lax.fori_loop(0, k, round_c, cand, unroll=True)
        taken = jnp.where(cand >= best_ref[k - 1:k, :], 1.0, 0.0)
        counts = _staircase_counts(taken, spans)
        lim = jnp.zeros_like(s1)
        for a in range(k):
            lim = jnp.where(s1 == sv1[a:a + 1], counts[a], lim)
        emit(rank2, lim)

        n1 = jnp.sum(jnp.where(s1 >= sv1[k - 1:k], 1.0, 0.0), axis=0, keepdims=True)
        n2 = jnp.sum(jnp.where(rank2 < kf, 1.0, 0.0), axis=0, keepdims=True)
        nc = jnp.sum(taken, axis=0, keepdims=True)
        tied = jnp.logical_or(jnp.max(jnp.maximum(jnp.maximum(n1, n2), nc)) > kf,
                              jnp.min(s2) <= RANK_MARK)

        @pl.when(tied)
        def _():
            tiles = [slice(lt * 128, (lt + 1) * 128) for lt in range(n_tiles)]
            rank1 = jnp.concatenate([_extract_ranked(s1[:, ls], sv1_tiles.at[lt], k)
                                     for lt, ls in enumerate(tiles)], axis=1)
            rank2 = jnp.concatenate([_extract_ranked(s2[:, ls], sv2_tiles.at[lt], k)
                                     for lt, ls in enumerate(tiles)], axis=1)
            cand, spans = _candidates(sorted1(), sorted2())
            taken = jnp.where(_extract_ranked(cand, best_ref, k) < kf, 1.0, 0.0)
            counts = _staircase_counts(taken, spans)
            lim = jnp.zeros_like(s1)
            for a in range(k):
                lim = jnp.where(rank1 == float(a), counts[a], lim)
            emit(rank2, lim)

        return carry

    lax.fori_loop(0, PEER_HEADS, head, 0)


def peer_select(h, g, shift, scale, wq_t, keys, *, layer, seq, rows, tm=512):
    t, d = h.shape
    nq = wq_t.shape[1]
    per_batch = seq // tm
    tab_spec = pl.BlockSpec((PEER_HEADS, N_SUBKEYS, tm), lambda i: (0, 0, i))
    tab = lambda dt: jax.ShapeDtypeStruct((PEER_HEADS, N_SUBKEYS, t), dt)
    grp_spec = pl.BlockSpec((PEER_HEADS, N_SUBKEYS // rows, rows, tm), lambda i: (0, 0, 0, i))
    grp = jax.ShapeDtypeStruct((PEER_HEADS, N_SUBKEYS // rows, rows, t), F32)
    return pl.pallas_call(
        _peer_select_kernel,
        grid=(t // tm,),
        in_specs=[
            pl.BlockSpec((tm, d), lambda i: (i, 0)),
            pl.BlockSpec((1, d), lambda i: (0, 0)),
            pl.BlockSpec((1, 1, d), lambda i: (i // per_batch, 0, 0)),
            pl.BlockSpec((1, 1, d), lambda i: (i // per_batch, 0, 0)),
            pl.BlockSpec((None, nq, d), lambda i: (layer, 0, 0)),
            pl.BlockSpec((None,) + keys.shape[1:], lambda i: (layer, 0, 0, 0, 0)),
        ],
        out_specs=[pl.BlockSpec((tm, d), lambda i: (i, 0)), tab_spec, tab_spec, grp_spec, grp_spec],
        out_shape=[jax.ShapeDtypeStruct((t, d), BF16), tab(BF16), tab(BF16), grp, grp],
        scratch_shapes=[pltpu.VMEM((nq, tm), BF16),
                        pltpu.VMEM((tm // 128, PEER_TOPK, 128), F32),
                        pltpu.VMEM((tm // 128, PEER_TOPK, 128), F32),
                        pltpu.VMEM((PEER_TOPK, tm), F32)],
        compiler_params=_params("arbitrary"),
        name="peer_select",
    )(h, g, shift, scale, wq_t, keys)


def _peer_mix_kernel(xf_ref, u_ref, vt_ref, r2_ref, a2_ref, lim_ref, a1_ref, h_ref, g_ref, *rest,
                     n_blocks, convert):
    if convert:
        cu_ref, cv_ref, o_ref, ou_ref, ovt_ref, acc_ref, hid_a_ref, hid_b_ref, p_ref = rest
    else:
        o_ref, acc_ref, hid_a_ref, hid_b_ref, p_ref = rest
    g = pl.program_id(0)
    eb, tc = hid_a_ref.shape
    rows_per_step = eb // N_SUBKEYS
    mix_block = jnp.maximum(g - 1, 0) % n_blocks

    @pl.when(mix_block == 0)
    def _():
        acc_ref[...] = jnp.zeros_like(acc_ref)

    @pl.when(g == 0)
    def _():
        hid_b_ref[...] = jnp.zeros_like(hid_b_ref)

    def step(cur_ref, prev_ref):
        if convert:
            _convert_expert_slice(cu_ref, cv_ref, ou_ref, ovt_ref)
        cur_ref[...] = lax.dot_general(u_ref[...], xf_ref[...], NT_DIMS, preferred_element_type=F32)
        for kk in range(rows_per_step):
            gate = None
            for h in range(PEER_HEADS):
                lim = jnp.broadcast_to(lim_ref[h, 0, kk:kk + 1, :], (16, tc)).astype(BF16)
                a1 = jnp.broadcast_to(a1_ref[h, 0, kk:kk + 1, :], (16, tc)).astype(BF16)
                lim = jnp.concatenate([lim] * (N_SUBKEYS // 16), axis=0)
                a1 = jnp.concatenate([a1] * (N_SUBKEYS // 16), axis=0)
                w = jnp.where(r2_ref[h] < lim, a2_ref[h] * a1, jnp.zeros((), BF16))
                gate = w if gate is None else gate + w
            sl = slice(kk * N_SUBKEYS, (kk + 1) * N_SUBKEYS)
            hid = prev_ref[sl, :]
            act = 0.5 * hid * (1.0 + lax.erf(hid * INV_SQRT2))
            p_ref[sl, :] = act.astype(BF16) * gate
        acc_ref[...] += jnp.dot(vt_ref[...], p_ref[...], preferred_element_type=F32)

    @pl.when(g % 2 == 0)
    def _():
        step(hid_a_ref, hid_b_ref)

    @pl.when(g % 2 == 1)
    def _():
        step(hid_b_ref, hid_a_ref)

    @pl.when(jnp.logical_and(mix_block == n_blocks - 1, g > 0))
    def _():
        o_ref[...] = h_ref[...] + g_ref[0] * acc_ref[...].T


def peer_mix(xf, u, vt, r2, a2, lim, a1, h, gate, *, seq, next_tables=None, tc=512):
    t, d = h.shape
    rows = lim.shape[2]
    eb = rows * N_SUBKEYS
    n_blocks = u.shape[0] // eb
    n_items = (t // tc) * n_blocks
    per_batch = seq // tc

    def hid_item(g):
        return jnp.minimum(g, n_items - 1)

    def mix_item(g):
        return jnp.maximum(g - 1, 0)

    tab_spec = pl.BlockSpec((PEER_HEADS, N_SUBKEYS, tc), lambda g: (0, 0, mix_item(g) // n_blocks))
    row_spec = pl.BlockSpec((PEER_HEADS, 1, rows, tc),
                            lambda g: (0, mix_item(g) % n_blocks, 0, mix_item(g) // n_blocks))
    chunk_spec = pl.BlockSpec((tc, d), lambda g: (mix_item(g) // n_blocks, 0))
    cv_in, cv_out, cv_shape, cv_args = [], [], [], []
    if next_tables is not None:
        peer_u, peer_v, layer = next_tables
        assert n_items + 1 >= peer_u.shape[1] // CONVERT_ROWS
        cv_in, cv_out, cv_shape = _convert_specs(peer_u, peer_v, layer, lambda g: g)
        cv_args = [peer_u, peer_v]
    out = pl.pallas_call(
        functools.partial(_peer_mix_kernel, n_blocks=n_blocks, convert=bool(cv_args)),
        grid=(n_items + 1,),
        in_specs=[
            pl.BlockSpec((tc, d), lambda g: (hid_item(g) // n_blocks, 0)),
            pl.BlockSpec((eb, d), lambda g: (hid_item(g) % n_blocks, 0)),
            pl.BlockSpec((d, eb), lambda g: (0, mix_item(g) % n_blocks)),
            tab_spec, tab_spec, row_spec, row_spec,
            chunk_spec,
            pl.BlockSpec((1, 1, d), lambda g: (mix_item(g) // n_blocks // per_batch, 0, 0)),
        ] + cv_in,
        out_specs=[chunk_spec] + cv_out,
        out_shape=[jax.ShapeDtypeStruct((t, d), F32)] + cv_shape,
        scratch_shapes=[pltpu.VMEM((d, tc), F32),
                        pltpu.VMEM((eb, tc), F32),
                        pltpu.VMEM((eb, tc), F32),
                        pltpu.VMEM((eb, tc), BF16)],
        compiler_params=_params("arbitrary"),
        name="peer_mix",
    )(xf, u, vt, r2, a2, lim, a1, h, gate, *cv_args)
    return out if cv_args else out[0]


def kernel(x, c, rel_bias, norm1_g, norm2_g, w_ada, b_ada, a_w_in, a_q_gain, a_k_gain,
           b_w_in, w_out, peer_w_query, peer_sub_keys, peer_u, peer_v):
    b, s, d = x.shape
    depth = w_ada.shape[0]
    n_groups = len(DIL_GROUPS)
    heads = d // HEAD_DIM

    mod = adaln(c, w_ada, b_ada).reshape(depth, b, 6, 1, d)
    a_w, b_w, w_o = a_w_in.astype(BF16), b_w_in.astype(BF16), w_out.astype(BF16)
    wq_t = jnp.swapaxes(peer_w_query, 1, 2).astype(BF16)
    keys = peer_sub_keys.astype(BF16)
    u = vt = None
    h = x.reshape(b * s, d)
    for i in range(depth):
        sh1, sc1, g1, sh2, sc2, g2 = (mod[i, :, k] for k in range(6))
        j = i // 2
        if i % 2 == 0:
            ones = jnp.ones((HEAD_DIM,), F32)
            gain = jnp.stack([jnp.tile(row, heads) for g in range(n_groups)
                              for row in (a_q_gain[j, g] / math.sqrt(HEAD_DIM), a_k_gain[j, g], ones)])
            qkvs, u0, vt0 = in_proj_dilated(h, norm1_g[i][None], sh1, sc1, a_w, gain[:, None, :],
                                            peer_u, peer_v, layer=j, peer_layer=i, seq=s)
            if u is None:
                u, vt = u0, vt0
            outs, lses = [], []
            for g, (window, dil) in enumerate(DIL_GROUPS):
                bias = _band_bias(rel_bias[:, g * heads:(g + 1) * heads], window, dil)
                o, lse = dilated_group_attention(qkvs[g], bias, n_heads=heads)
                outs.append(o)
                lses.append(lse)
            h = out_proj_merge(outs, lses, w_o, h, g1, layer=i, seq=s)
        else:
            qkv = in_proj(h, norm1_g[i][None], sh1, sc1, b_w, layer=j, seq=s)
            y = stick_breaking_attention(qkv.reshape(b, s, -1), n_heads=heads)
            h = out_proj(y.reshape(b * s, d), w_o, h, g1, layer=i, seq=s)
        xf, r2, a2, lim, a1 = peer_select(h, norm2_g[i][None], sh2, sc2, wq_t, keys, layer=i, seq=s,
                                          rows=PEER_BLOCK_ROWS)
        if i + 1 < depth:
            h, u, vt = peer_mix(xf, u, vt, r2, a2, lim, a1, h, g2, seq=s,
                                next_tables=(peer_u, peer_v, i + 1))
        else:
            h = peer_mix(xf, u, vt, r2, a2, lim, a1, h, g2, seq=s)
    return h.reshape(b, s, d)
```

```python
import functools
import math

import jax
import jax.numpy as jnp
from jax import lax
from jax.experimental import pallas as pl
from jax.experimental.pallas import tpu as pltpu

F32 = jnp.float32
BF16 = jnp.bfloat16

LANES = 128
BF16_ROWS = 16
HEAD_DIM = 128
BLOCK = 128
DIL_GROUPS = ((128, 1), (512, 4), (2048, 16))
REL_BUCKETS = 32
REL_MAX_DIST = 2048
PEER_HEADS = 8
N_SUBKEYS = 128
PEER_TOPK = 16
PEER_BLOCK_ROWS = 8
EPS = 1e-6
NEG_BIG = -1e30
SB_DEAD_DECAY = 104.0
SB_HEADS_PER_STEP = 8
SB_Q_CHUNK = 1024
INV_SQRT2 = 0.7071067811865476

VMEM_LIMIT = 60 * 1024 * 1024

NT_DIMS = (((1,), (1,)), ((), ()))


def _params(*sem):
    return pltpu.CompilerParams(dimension_semantics=sem, vmem_limit_bytes=VMEM_LIMIT)


def _adaln_kernel(c_ref, w_lo_ref, w_hi_ref, b_ref, o_ref):
    c = c_ref[...]
    cs = c / (1.0 + jnp.exp(-c))
    half = w_lo_ref.shape[1]
    dot = functools.partial(jnp.dot, preferred_element_type=F32, precision=lax.Precision.HIGHEST)
    o_ref[0] = dot(cs[:, :half], w_lo_ref[0]) + dot(cs[:, half:], w_hi_ref[0]) + b_ref[0]


ADALN_ROWS = 8


def adaln(c_pad, w_ada, b_ada, *, layers, tn=1024):
    depth, d, n = w_ada.shape
    rows = c_pad.shape[0]
    return pl.pallas_call(
        _adaln_kernel,
        grid=(layers, n // tn),
        in_specs=[
            pl.BlockSpec((rows, d), lambda i, j: (0, 0)),
            pl.BlockSpec((1, d // 2, tn), lambda i, j: (i, 0, j)),
            pl.BlockSpec((1, d // 2, tn), lambda i, j: (i, 1, j)),
            pl.BlockSpec((1, 1, tn), lambda i, j: (i, 0, j)),
        ],
        out_specs=pl.BlockSpec((1, rows, tn), lambda i, j: (i, 0, j)),
        out_shape=jax.ShapeDtypeStruct((layers, rows, n), F32),
        compiler_params=_params("arbitrary", "arbitrary"),
        name="adaln",
    )(c_pad, w_ada, w_ada, b_ada.reshape(depth, 1, n))


CONVERT_ROWS = 128


def _convert_expert_slice(cu_ref, cv_ref, ou_ref, ovt_ref):
    ou_ref[...] = cu_ref[...].astype(BF16)
    ovt_ref[...] = cv_ref[...].T.astype(BF16)


def _convert_specs(peer_u, peer_v, layer, step_of):
    n_exp, d = peer_u.shape[1:]
    last = n_exp // CONVERT_ROWS - 1

    def slice_of(*idx):
        return jnp.minimum(step_of(*idx), last)

    src = pl.BlockSpec((None, CONVERT_ROWS, d), lambda *idx: (layer, slice_of(*idx), 0))
    in_specs = [src, src]
    out_specs = [pl.BlockSpec((CONVERT_ROWS, d), lambda *idx: (slice_of(*idx), 0)),
                 pl.BlockSpec((d, CONVERT_ROWS), lambda *idx: (0, slice_of(*idx)))]
    out_shape = [jax.ShapeDtypeStruct((n_exp, d), BF16), jax.ShapeDtypeStruct((d, n_exp), BF16)]
    return in_specs, out_specs, out_shape


def _norm_modulate(x, g, shift, scale):
    ms = jnp.mean(x * x, axis=-1, keepdims=True)
    y = x * lax.rsqrt(ms + EPS) * g
    return y * (1.0 + scale) + shift


def _in_proj_kernel(h_ref, g_ref, sh_ref, sc_ref, w_ref, o_ref, xm_ref):
    @pl.when(pl.program_id(1) == 0)
    def _():
        xm_ref[...] = _norm_modulate(h_ref[...], g_ref[...], sh_ref[0], sc_ref[0]).astype(BF16)

    o_ref[...] = jnp.dot(xm_ref[...], w_ref[...], preferred_element_type=F32).astype(o_ref.dtype)


def in_proj(h, g, shift, scale, w, *, layer, seq, tm=1024, tn=2048):
    t, d = h.shape
    n = w.shape[2]
    per_batch = seq // tm
    return pl.pallas_call(
        _in_proj_kernel,
        grid=(t // tm, n // tn),
        in_specs=[
            pl.BlockSpec((tm, d), lambda i, j: (i, 0)),
            pl.BlockSpec((1, d), lambda i, j: (0, 0)),
            pl.BlockSpec((1, 1, d), lambda i, j: (i // per_batch, 0, 0)),
            pl.BlockSpec((1, 1, d), lambda i, j: (i // per_batch, 0, 0)),
            pl.BlockSpec((None, d, tn), lambda i, j: (layer, 0, j)),
        ],
        out_specs=pl.BlockSpec((tm, tn), lambda i, j: (i, j)),
        out_shape=jax.ShapeDtypeStruct((t, n), BF16),
        scratch_shapes=[pltpu.VMEM((tm, d), BF16)],
        compiler_params=_params("arbitrary", "arbitrary"),
        name="in_proj",
    )(h, g, shift, scale, w)


def _in_proj_dilated_kernel(h_ref, g_ref, sh_ref, sc_ref, w_ref, gain_ref,
                            cu_ref, cv_ref, cb_ref, co_ref,
                            o0_ref, o1_ref, o2_ref, ou_ref, ovt_ref, ob_ref, oo_ref,
                            xm0_ref, xm1_ref, xm2_ref, slab_ref, *, parts):
    j = pl.program_id(1)
    seg = j // parts
    tm, d = xm0_ref.shape
    xm_refs = (xm0_ref, xm1_ref, xm2_ref)

    def side_jobs():
        _convert_expert_slice(cu_ref, cv_ref, ou_ref, ovt_ref)
        ob_ref[...] = cb_ref[...].astype(BF16)
        oo_ref[...] = co_ref[...].astype(BF16)

    @pl.when(j == 0)
    def _():
        xm = _norm_modulate(h_ref[...], g_ref[...], sh_ref[0], sc_ref[0])
        for sb in range(d // HEAD_DIM):
            slab_ref[sb] = xm[:, sb * HEAD_DIM:(sb + 1) * HEAD_DIM]
        for (_, dil), xm_ref in zip(DIL_GROUPS, xm_refs):
            if dil == 1:
                xm_ref[...] = xm.astype(BF16)
                continue
            rows = tm // dil
            for sb in range(d // HEAD_DIM):
                for r in range(dil):
                    xm_ref[r * rows:(r + 1) * rows, sb * HEAD_DIM:(sb + 1) * HEAD_DIM] = (
                        slab_ref[sb, pl.ds(r, rows, stride=dil), :].astype(BF16))

    def project(xm_ref, o_ref, dil):
        rows = tm // dil
        slab = min(2 * HEAD_DIM, w_ref.shape[1])
        n_slabs = w_ref.shape[1] // slab

        @pl.when(seg % 3 == 2)
        def _():
            side_jobs()
            for sb in range(n_slabs):
                cols = slice(sb * slab, (sb + 1) * slab)
                acc = jnp.dot(xm_ref[...], w_ref[:, cols], preferred_element_type=F32).astype(BF16)
                for r in range(dil):
                    o_ref[0, r, :, cols] = acc[r * rows:(r + 1) * rows]

        @pl.when(seg % 3 != 2)
        def _():
            side_jobs()
            for sb in range(n_slabs):
                acc = jnp.dot(xm_ref[...], w_ref[:, sb * slab:(sb + 1) * slab], preferred_element_type=F32)
                for hh in range(slab // HEAD_DIM):
                    sl = slice(sb * slab + hh * HEAD_DIM, sb * slab + (hh + 1) * HEAD_DIM)
                    a = acc[:, hh * HEAD_DIM:(hh + 1) * HEAD_DIM]
                    ms = jnp.mean(a * a, axis=-1, keepdims=True)
                    y = (a * lax.rsqrt(ms + EPS) * gain_ref[0, :, sl]).astype(BF16)
                    for r in range(dil):
                        o_ref[0, r, :, sl] = y[r * rows:(r + 1) * rows]

    for grp, ((_, dil), xm_ref, o_ref) in enumerate(zip(DIL_GROUPS, xm_refs, (o0_ref, o1_ref, o2_ref))):
        @pl.when(seg // 3 == grp)
        def _(dil=dil, xm_ref=xm_ref, o_ref=o_ref):
            project(xm_ref, o_ref, dil)


def in_proj_dilated(h, g, shift, scale, w, gain, peer_u, peer_v, casts, *,
                    layer, peer_layer, seq, tm=512, tn=2048):
    t, d = h.shape
    n = w.shape[2]
    b = t // seq
    per_batch = seq // tm
    n_groups = len(DIL_GROUPS)
    width = gain.shape[2]
    parts = width // tn
    assert n == n_groups * 3 * width

    def out_spec(grp, dil):
        def imap(i, j):
            return (i // per_batch, 0, i % per_batch, jnp.clip(j - 3 * parts * grp, 0, 3 * parts - 1))
        return pl.BlockSpec((1, dil, tm // dil, tn), imap)

    n_cols = n // tn
    n_steps = (t // tm) * n_cols
    step = lambda i, j: i * n_cols + j
    assert n_steps >= peer_u.shape[1] // CONVERT_ROWS
    cv_in, cv_out, cv_shape = _convert_specs(peer_u, peer_v, peer_layer, step)
    side_in, side_out, side_shape = [], [], []
    for a in casts:
        rows = a.shape[0] // (peer_u.shape[1] // CONVERT_ROWS)
        last = a.shape[0] // rows - 1
        spec = pl.BlockSpec((rows, a.shape[1]), lambda i, j, last=last: (jnp.minimum(step(i, j), last), 0))
        side_in.append(spec)
        side_out.append(spec)
        side_shape.append(jax.ShapeDtypeStruct(a.shape, BF16))
    *qkvs, u_bf, vt_bf, cast0, cast1 = pl.pallas_call(
        functools.partial(_in_proj_dilated_kernel, parts=parts),
        grid=(t // tm, n // tn),
        in_specs=[
            pl.BlockSpec((tm, d), lambda i, j: (i, 0)),
            pl.BlockSpec((1, d), lambda i, j: (0, 0)),
            pl.BlockSpec((1, 1, d), lambda i, j: (i // per_batch, 0, 0)),
            pl.BlockSpec((1, 1, d), lambda i, j: (i // per_batch, 0, 0)),
            pl.BlockSpec((None, d, tn), lambda i, j: (layer, 0, j)),
            pl.BlockSpec((1, 1, tn), lambda i, j: (j // parts, 0, j % parts)),
        ] + cv_in + side_in,
        out_specs=[out_spec(grp, dil) for grp, (_, dil) in enumerate(DIL_GROUPS)] + cv_out + side_out,
        out_shape=[jax.ShapeDtypeStruct((b, dil, seq // dil, 3 * width), BF16)
                   for _, dil in DIL_GROUPS] + cv_shape + side_shape,
        scratch_shapes=[pltpu.VMEM((tm, d), BF16)] * 3 + [pltpu.VMEM((d // HEAD_DIM, tm, HEAD_DIM), F32)],
        compiler_params=_params("arbitrary", "arbitrary"),
        name="in_proj_dilated",
    )(h, g, shift, scale, w, gain, peer_u, peer_v, *casts)
    return qkvs, u_bf, vt_bf, cast0, cast1


def _dil_attn_kernel(q_ref, kp_ref, kc_ref, vp_ref, vc_ref, bias_ref, o_ref, lse_ref, *, n_heads):
    n = pl.program_id(2)
    n_sub = q_ref.shape[2] // BLOCK
    lane = lax.broadcasted_iota(jnp.int32, (1, 2 * BLOCK), 1)
    pen = jnp.where(jnp.logical_and(lane < BLOCK, n == 0), NEG_BIG, 0.0).astype(F32)
    ones = jnp.ones((2 * BLOCK, HEAD_DIM), BF16)
    lse_lanes = BLOCK // n_heads
    lane_head = lax.broadcasted_iota(jnp.int32, (1, BLOCK), 1) // lse_lanes
    for sb in range(n_sub):
        rows = slice(sb * BLOCK, (sb + 1) * BLOCK)
        prev = slice((sb - 1) * BLOCK, sb * BLOCK)
        lse_pack = jnp.zeros((BLOCK, BLOCK), F32)
        for hh in range(n_heads):
            sl = slice(hh * HEAD_DIM, (hh + 1) * HEAD_DIM)
            q = q_ref[0, 0, rows, sl]
            k_prev = kp_ref[0, 0, :, sl] if sb == 0 else kc_ref[0, 0, prev, sl]
            v_prev = vp_ref[0, 0, :, sl] if sb == 0 else vc_ref[0, 0, prev, sl]
            k = jnp.concatenate([k_prev, kc_ref[0, 0, rows, sl]], axis=0)
            v = jnp.concatenate([v_prev, vc_ref[0, 0, rows, sl]], axis=0)
            logits = lax.dot_general(q, k, NT_DIMS, preferred_element_type=F32)
            logits = logits + bias_ref[hh]
            if sb == 0:
                logits = logits + pen
            m = jnp.max(logits, axis=-1, keepdims=True)
            p = jnp.exp(logits - m).astype(BF16)
            v_ext = jnp.concatenate([v, ones], axis=1)
            pv = jnp.dot(p, v_ext, preferred_element_type=F32)
            denom = pv[:, HEAD_DIM:]
            o_ref[0, 0, rows, sl] = (pv[:, :HEAD_DIM] / denom).astype(o_ref.dtype)
            lse_pack = jnp.where(lane_head == hh, m + jnp.log(denom), lse_pack)
        lse_ref[0, 0, rows, :] = lse_pack


def dilated_group_attention(qkv, bias, *, n_heads, q_blocks=2):
    b, dil, sub_len, c = qkv.shape
    assert BLOCK % n_heads == 0
    width = n_heads * HEAD_DIM
    q_blocks = min(q_blocks, sub_len // BLOCK)
    rows = q_blocks * BLOCK
    steps = sub_len // rows

    def cur(which):
        return pl.BlockSpec((1, 1, rows, width), lambda bi, r, n: (bi, r, n, which))

    def prev(which):
        return pl.BlockSpec((1, 1, BLOCK, width),
                            lambda bi, r, n: (bi, r, jnp.maximum(q_blocks * n - 1, 0), which))

    return pl.pallas_call(
        functools.partial(_dil_attn_kernel, n_heads=n_heads),
        grid=(b, dil, steps),
        in_specs=[cur(0), prev(1), cur(1), prev(2), cur(2),
                  pl.BlockSpec((n_heads, BLOCK, 2 * BLOCK), lambda bi, r, n: (0, 0, 0))],
        out_specs=[pl.BlockSpec((1, 1, rows, width), lambda bi, r, n: (bi, r, n, 0)),
                   pl.BlockSpec((1, 1, rows, BLOCK), lambda bi, r, n: (bi, r, n, 0))],
        out_shape=[jax.ShapeDtypeStruct((b, dil, sub_len, width), BF16),
                   jax.ShapeDtypeStruct((b, dil, sub_len, BLOCK), F32)],
        compiler_params=_params("arbitrary", "arbitrary", "arbitrary"),
        name=f"dilated_attn_d{dil}",
    )(qkv, qkv, qkv, qkv, qkv, bias)


def _rel_bucket(dist):
    exact = REL_BUCKETS // 2
    d_f = jnp.maximum(dist, exact).astype(F32)
    log_b = exact + (jnp.log(d_f / exact) / math.log(REL_MAX_DIST / exact)
                     * (REL_BUCKETS - exact)).astype(jnp.int32)
    return jnp.where(dist < exact, dist, jnp.minimum(log_b, REL_BUCKETS - 1))


def _band_bias(table_g, window, dil):
    w_steps = window // dil
    period = 3 * BLOCK
    per_delta = table_g[_rel_bucket(jnp.arange(w_steps + 1) * dil)].astype(F32)
    row = jnp.full((period, table_g.shape[1]), NEG_BIG, F32)
    row = row.at[BLOCK - w_steps:BLOCK + 1].set(per_delta[::-1])
    flat = jnp.tile(row.T, (1, BLOCK))[:, :BLOCK * (period - 1)]
    return flat.reshape(-1, BLOCK, period - 1)[:, :, :2 * BLOCK]


def _out_proj_merge_kernel(o0_ref, o1_ref, o2_ref, l0_ref, l1_ref, l2_ref, w_ref, h_ref, g_ref, out_ref,
                           os1_ref, os2_ref, ls1_ref, ls2_ref, ya_ref, yb_ref):
    g = pl.program_id(0)
    tm = out_ref.shape[0]
    n_heads = ya_ref.shape[1] // HEAD_DIM

    @pl.when(g == 0)
    def _():
        yb_ref[...] = jnp.zeros_like(yb_ref)

    def to_token_order(src_ref, dst_ref, sl):
        dil = src_ref.shape[1]
        for r in range(dil):
            dst_ref[pl.ds(r, tm // dil, stride=dil), :] = src_ref[0, r, :, sl].astype(F32)

    def step(y_ref, prev_y_ref):
        to_token_order(l1_ref, ls1_ref, slice(None))
        to_token_order(l2_ref, ls2_ref, slice(None))
        l0, l1, l2 = l0_ref[0, 0], ls1_ref[...], ls2_ref[...]
        m = jnp.maximum(jnp.maximum(l0, l1), l2)
        e0, e1, e2 = jnp.exp(l0 - m), jnp.exp(l1 - m), jnp.exp(l2 - m)
        inv = 1.0 / (e0 + e1 + e2)
        a0, a1, a2 = e0 * inv, e1 * inv, e2 * inv
        for hh in range(n_heads):
            sl = slice(hh * HEAD_DIM, (hh + 1) * HEAD_DIM)
            col = slice(hh * (BLOCK // n_heads), hh * (BLOCK // n_heads) + 1)
            to_token_order(o1_ref, os1_ref, sl)
            to_token_order(o2_ref, os2_ref, sl)
            y = (a0[:, col] * o0_ref[0, 0, :, sl].astype(F32) + a1[:, col] * os1_ref[...]
                 + a2[:, col] * os2_ref[...])
            y_ref[:, sl] = y.astype(BF16)
        proj = jnp.dot(prev_y_ref[...], w_ref[...], preferred_element_type=F32)
        out_ref[...] = h_ref[...] + g_ref[0] * proj

    @pl.when(g % 2 == 0)
    def _():
        step(ya_ref, yb_ref)

    @pl.when(g % 2 == 1)
    def _():
        step(yb_ref, ya_ref)


def out_proj_merge(outs, lses, w, h, gate, *, layer, seq, tm=256):
    t, d = h.shape
    width = w.shape[1]
    per_batch = seq // tm
    n_steps = t // tm

    def merged(g):
        return jnp.minimum(g, n_steps - 1)

    def projected(g):
        return jnp.maximum(g - 1, 0)

    row = pl.BlockSpec((tm, d), lambda g: (projected(g), 0))

    def grp_spec(a):
        dil, cols = a.shape[1], a.shape[3]
        return pl.BlockSpec((1, dil, tm // dil, cols),
                            lambda g: (merged(g) // per_batch, 0, merged(g) % per_batch, 0))

    return pl.pallas_call(
        _out_proj_merge_kernel,
        grid=(n_steps + 1,),
        in_specs=[grp_spec(a) for a in (*outs, *lses)] + [
            pl.BlockSpec((None, width, d), lambda g: (layer, 0, 0)),
            row,
            pl.BlockSpec((1, 1, d), lambda g: (projected(g) // per_batch, 0, 0)),
        ],
        out_specs=row,
        out_shape=jax.ShapeDtypeStruct((t, d), F32),
        scratch_shapes=[pltpu.VMEM((tm, HEAD_DIM), F32)] * 4 + [pltpu.VMEM((tm, width), BF16)] * 2,
        compiler_params=_params("arbitrary"),
        name="out_proj_merge",
    )(*outs, *lses, w, h, gate)


def _out_proj_kernel(y_ref, w_ref, h_ref, g_ref, out_ref):
    proj = jnp.dot(y_ref[...], w_ref[...], preferred_element_type=F32)
    out_ref[...] = h_ref[...] + g_ref[0] * proj


def out_proj(y, w, h, gate, *, layer, seq, tm=512):
    t, d = h.shape
    width = w.shape[1]
    per_batch = seq // tm
    row = lambda cols: pl.BlockSpec((tm, cols), lambda i: (i, 0))
    return pl.pallas_call(
        _out_proj_kernel,
        grid=(t // tm,),
        in_specs=[row(width), pl.BlockSpec((None, width, d), lambda i: (layer, 0, 0)), row(d),
                  pl.BlockSpec((1, 1, d), lambda i: (i // per_batch, 0, 0))],
        out_specs=row(d),
        out_shape=jax.ShapeDtypeStruct((t, d), F32),
        compiler_params=_params("arbitrary"),
        name="out_proj",
    )(y, w, h, gate)


def _split2(x):
    hi = x.astype(BF16)
    lo = (x - hi.astype(F32)).astype(BF16)
    return hi, lo


def _sb_kernel(q_ref, k_ref, v_ref, o_ref, acc_ref, decay_ref, *, scale):
    nq = q_ref.shape[1] // BLOCK
    q_base = pl.program_id(2) * q_ref.shape[1]
    n_heads = acc_ref.shape[0]
    row = lax.broadcasted_iota(jnp.int32, (BLOCK, BLOCK), 0)
    col = lax.broadcasted_iota(jnp.int32, (BLOCK, BLOCK), 1)
    before = col < row

    def tri_ext(n_keys):
        r = lax.broadcasted_iota(jnp.int32, (n_keys, n_keys), 0)
        c = lax.broadcasted_iota(jnp.int32, (n_keys, n_keys), 1)
        tri = jnp.where(r > c, 1.0, 0.0).astype(BF16)
        return jnp.concatenate([tri, jnp.ones((n_keys, BLOCK), BF16)], axis=1)

    tri1, tri2 = tri_ext(BLOCK), tri_ext(2 * BLOCK)

    def tails_and_sums(sps, tri):
        n_keys = tri.shape[0]
        parts = [t for sp in sps for t in _split2(sp)]
        r = jnp.dot(jnp.concatenate(parts, axis=0), tri, preferred_element_type=F32)
        out = []
        for g in range(len(sps)):
            hi = r[(2 * g) * BLOCK:(2 * g + 1) * BLOCK]
            lo = r[(2 * g + 1) * BLOCK:(2 * g + 2) * BLOCK]
            both = hi + lo
            out.append((both[:, :n_keys], both[:, n_keys:]))
        return out

    def softplus(z):
        return jnp.maximum(z, 0.0) + jnp.log1p(jnp.exp(-jnp.abs(z)))

    def logits(q0, k0, g, n_keys=BLOCK):
        cs = slice(g * HEAD_DIM, (g + 1) * HEAD_DIM)
        return lax.dot_general(q_ref[0, pl.ds(q0, BLOCK), cs], k_ref[0, pl.ds(k0, n_keys), cs],
                               NT_DIMS, preferred_element_type=F32) * scale

    def values(k0, g, n_keys=BLOCK):
        return v_ref[0, pl.ds(k0, n_keys), g * HEAD_DIM:(g + 1) * HEAD_DIM]

    def qblock(i, carry):
        q0 = pl.multiple_of(i * BLOCK, BLOCK)
        d0 = pl.multiple_of(q_base + q0, BLOCK)
        zs = [logits(q0, d0, g) for g in range(n_heads)]
        sps = [softplus(z) for z in zs]
        ts = tails_and_sums([jnp.where(before, sp, 0.0) for sp in sps], tri1)
        live = None
        for g in range(n_heads):
            tail, rsum = ts[g]
            a = jnp.where(before, jnp.exp(zs[g] - sps[g] - tail), 0.0)
            acc_ref[g] = jnp.dot(a.astype(BF16), values(d0, g), preferred_element_type=F32)
            decay_ref[g] = rsum
            lo = jnp.min(rsum)
            live = lo if live is None else jnp.minimum(live, lo)

        def walk(k0, tri):
            n_keys = tri.shape[0]
            zs = [logits(q0, k0, g, n_keys) for g in range(n_heads)]
            sps = [softplus(z) for z in zs]
            ts = tails_and_sums(sps, tri)
            live = None
            for g in range(n_heads):
                tail, rsum = ts[g]
                decay = decay_ref[g]
                later = jnp.concatenate([decay] * (n_keys // BLOCK), axis=1)
                a = jnp.exp(zs[g] - sps[g] - (later + tail))
                acc_ref[g] += jnp.dot(a.astype(BF16), values(k0, g, n_keys), preferred_element_type=F32)
                decay = decay + rsum
                decay_ref[g] = decay
                lo = jnp.min(decay)
                live = lo if live is None else jnp.minimum(live, lo)
            return live

        def cond(c):
            j, live = c
            return jnp.logical_and(j >= 1, live < SB_DEAD_DECAY)

        def body(c):
            j, _ = c
            return j - 2, walk(pl.multiple_of((j - 1) * BLOCK, BLOCK), tri2)

        j, live = lax.while_loop(cond, body, (d0 // BLOCK - 1, live))

        @pl.when(jnp.logical_and(j == 0, live < SB_DEAD_DECAY))
        def _():
            walk(0, tri1)

        for g in range(n_heads):
            o_ref[0, pl.ds(q0, BLOCK), g * HEAD_DIM:(g + 1) * HEAD_DIM] = acc_ref[g].astype(o_ref.dtype)
        return carry

    lax.fori_loop(0, nq, qblock, 0)


def stick_breaking_attention(qkv, *, n_heads):
    b, s, _ = qkv.shape
    hps = min(SB_HEADS_PER_STEP, n_heads)
    steps = n_heads // hps
    width = hps * HEAD_DIM
    qc = min(SB_Q_CHUNK, s)

    def spec(which):
        return pl.BlockSpec((1, s, width), lambda bi, h, c: (bi, 0, which * steps + h))

    return pl.pallas_call(
        functools.partial(_sb_kernel, scale=1.0 / math.sqrt(HEAD_DIM)),
        grid=(b, steps, s // qc),
        in_specs=[pl.BlockSpec((1, qc, width), lambda bi, h, c: (bi, c, h)), spec(1), spec(2)],
        out_specs=pl.BlockSpec((1, qc, width), lambda bi, h, c: (bi, c, h)),
        out_shape=jax.ShapeDtypeStruct((b, s, n_heads * HEAD_DIM), BF16),
        scratch_shapes=[pltpu.VMEM((hps, BLOCK, HEAD_DIM), F32),
                        pltpu.VMEM((hps, BLOCK, BLOCK), F32)],
        compiler_params=_params("arbitrary", "arbitrary", "arbitrary"),
        name="stick_breaking",
    )(qkv, qkv, qkv)


UNRANKED = 127.0
RANK_MARK = -(2.0 ** 100)


def _as_f32(r):
    return jnp.asarray(r).astype(F32)


def _extract_ranked(s, dst_ref, count):
    n = s.shape[0]
    idx = lax.broadcasted_iota(jnp.int32, s.shape, 0).astype(F32)

    def body(r, carry):
        s, rank = carry
        m = jnp.max(s, axis=0, keepdims=True)
        first = jnp.min(jnp.where(s == m, idx, float(n)), axis=0, keepdims=True)
        pick = idx == first
        dst_ref[pl.ds(r, 1), :] = m
        return jnp.where(pick, -jnp.inf, s), jnp.where(pick, _as_f32(r), rank)

    _, rank = lax.fori_loop(0, count, body, (s, jnp.full(s.shape, UNRANKED, F32)))
    return rank


def _candidates(sv1, sv2):
    k = PEER_TOPK
    groups = [sv1[0:1] + sv2]
    groups += [sv1[a:a + 1] + sv2[0:k // 2] for a in range(1, k // 2)]
    groups += [sv1[k // 2:] + sv2[0:1]]
    spans = [(0, k)] + [(k + (a - 1) * (k // 2), k // 2) for a in range(1, k // 2)]
    tail0 = k + (k // 2 - 1) * (k // 2)
    spans += [(tail0 + a, 1) for a in range(k // 2)]
    return jnp.concatenate(groups, axis=0), spans


def _staircase_counts(taken, spans):
    return [jnp.sum(taken[lo:lo + n], axis=0, keepdims=True) for lo, n in spans]


def _peer_select_kernel(h_ref, g_ref, sh_ref, sc_ref, wq_ref, keys_ref,
                        xf_ref, r2_ref, a2_ref, lim_ref, a1_ref,
                        qt_ref, sv1_tiles, sv2_tiles, best_ref):
    xf = _norm_modulate(h_ref[...], g_ref[...], sh_ref[0], sc_ref[0]).astype(BF16)
    xf_ref[...] = xf
    qt_ref[...] = lax.dot_general(wq_ref[...], xf, NT_DIMS, preferred_element_type=F32).astype(BF16)
    k = PEER_TOPK
    half = N_SUBKEYS
    kf = float(k)
    n_tiles = sv2_tiles.shape[0]

    def head(h, carry):
        r0 = pl.multiple_of(h * 2 * half, 2 * half)
        s1 = jnp.dot(keys_ref[h, 0], qt_ref[pl.ds(r0, half), :], preferred_element_type=F32)
        s2 = jnp.dot(keys_ref[h, 1], qt_ref[pl.ds(r0 + half, half), :], preferred_element_type=F32)

        def sorted1():
            return jnp.concatenate([sv1_tiles[lt] for lt in range(n_tiles)], axis=1)

        def sorted2():
            return jnp.concatenate([sv2_tiles[lt] for lt in range(n_tiles)], axis=1)

        def emit(rank2, lim):
            sv1, sv2, best = sorted1(), sorted2(), best_ref[...]
            z = jnp.sum(jnp.exp(best - best[0:1]), axis=0, keepdims=True)
            r2_ref[h] = rank2.astype(BF16)
            a2_ref[h] = jnp.exp(s2 - sv2[0:1]).astype(BF16)
            a1 = jnp.exp(s1 - sv1[0:1]) / z
            rows = lim_ref.shape[2]
            for grp in range(half // rows):
                lim_ref[h, grp] = lim[grp * rows:(grp + 1) * rows]
                a1_ref[h, grp] = a1[grp * rows:(grp + 1) * rows]

        marked = []
        for lt in range(n_tiles):
            def round12(r, c, lt=lt):
                w1, w2 = c
                m1 = jnp.max(w1, axis=0, keepdims=True)
                m2 = jnp.max(w2, axis=0, keepdims=True)
                sv1_tiles[lt, pl.ds(r, 1), :] = m1
                sv2_tiles[lt, pl.ds(r, 1), :] = m2
                mark = RANK_MARK * (1.0 + _as_f32(r) / kf)
                return jnp.where(w1 == m1, -jnp.inf, w1), jnp.where(w2 == m2, mark, w2)

            ls = slice(lt * LANES, (lt + 1) * LANES)
            marked.append(lax.fori_loop(0, k, round12, (s1[:, ls], s2[:, ls]), unroll=True)[1])
        w2 = jnp.concatenate(marked, axis=1)
        rank2 = jnp.where(w2 <= RANK_MARK, w2 * (kf / RANK_MARK) - kf, UNRANKED)

        sv1 = sorted1()
        cand, spans = _candidates(sv1, sorted2())

        def round_c(r, w):
            m = jnp.max(w, axis=0, keepdims=True)
            best_ref[pl.ds(r, 1), :] = m
            return jnp.where(w == m, -jnp.inf, w)

        lax.fori_loop(0, k, round_c, cand, unroll=True)
        taken = jnp.where(cand >= best_ref[k - 1:k, :], 1.0, 0.0)
        counts = _staircase_counts(taken, spans)
        lim = jnp.zeros_like(s1)
        for a in range(k):
            lim = jnp.where(s1 == sv1[a:a + 1], counts[a], lim)
        emit(rank2, lim)

        n1 = jnp.sum(jnp.where(s1 >= sv1[k - 1:k], 1.0, 0.0), axis=0, keepdims=True)
        n2 = jnp.sum(jnp.where(rank2 < kf, 1.0, 0.0), axis=0, keepdims=True)
        nc = jnp.sum(taken, axis=0, keepdims=True)
        tied = jnp.logical_or(jnp.max(jnp.maximum(jnp.maximum(n1, n2), nc)) > kf,
                              jnp.min(s2) <= RANK_MARK)

        @pl.when(tied)
        def _():
            tiles = [slice(lt * LANES, (lt + 1) * LANES) for lt in range(n_tiles)]
            rank1 = jnp.concatenate([_extract_ranked(s1[:, ls], sv1_tiles.at[lt], k)
                                     for lt, ls in enumerate(tiles)], axis=1)
            rank2 = jnp.concatenate([_extract_ranked(s2[:, ls], sv2_tiles.at[lt], k)
                                     for lt, ls in enumerate(tiles)], axis=1)
            cand, spans = _candidates(sorted1(), sorted2())
            taken = jnp.where(_extract_ranked(cand, best_ref, k) < kf, 1.0, 0.0)
            counts = _staircase_counts(taken, spans)
            lim = jnp.zeros_like(s1)
            for a in range(k):
                lim = jnp.where(rank1 == float(a), counts[a], lim)
            emit(rank2, lim)

        return carry

    lax.fori_loop(0, PEER_HEADS, head, 0)


def peer_select(h, g, shift, scale, wq_t, keys, *, layer, seq, rows, tm=512):
    t, d = h.shape
    nq = wq_t.shape[1]
    per_batch = seq // tm
    tab_spec = pl.BlockSpec((PEER_HEADS, N_SUBKEYS, tm), lambda i: (0, 0, i))
    tab = lambda dt: jax.ShapeDtypeStruct((PEER_HEADS, N_SUBKEYS, t), dt)
    grp_spec = pl.BlockSpec((PEER_HEADS, N_SUBKEYS // rows, rows, tm), lambda i: (0, 0, 0, i))
    grp = jax.ShapeDtypeStruct((PEER_HEADS, N_SUBKEYS // rows, rows, t), F32)
    return pl.pallas_call(
        _peer_select_kernel,
        grid=(t // tm,),
        in_specs=[
            pl.BlockSpec((tm, d), lambda i: (i, 0)),
            pl.BlockSpec((1, d), lambda i: (0, 0)),
            pl.BlockSpec((1, 1, d), lambda i: (i // per_batch, 0, 0)),
            pl.BlockSpec((1, 1, d), lambda i: (i // per_batch, 0, 0)),
            pl.BlockSpec((None, nq, d), lambda i: (layer, 0, 0)),
            pl.BlockSpec((None,) + keys.shape[1:], lambda i: (layer, 0, 0, 0, 0)),
        ],
        out_specs=[pl.BlockSpec((tm, d), lambda i: (i, 0)), tab_spec, tab_spec, grp_spec, grp_spec],
        out_shape=[jax.ShapeDtypeStruct((t, d), BF16), tab(BF16), tab(BF16), grp, grp],
        scratch_shapes=[pltpu.VMEM((nq, tm), BF16),
                        pltpu.VMEM((tm // LANES, PEER_TOPK, LANES), F32),
                        pltpu.VMEM((tm // LANES, PEER_TOPK, LANES), F32),
                        pltpu.VMEM((PEER_TOPK, tm), F32)],
        compiler_params=_params("arbitrary"),
        name="peer_select",
    )(h, g, shift, scale, wq_t, keys)


def _peer_mix_kernel(xf_ref, u_ref, vt_ref, r2_ref, a2_ref, lim_ref, a1_ref, h_ref, g_ref, *rest,
                     n_blocks, convert):
    if convert:
        cu_ref, cv_ref, o_ref, ou_ref, ovt_ref, acc_ref, hid_a_ref, hid_b_ref, p_ref = rest
    else:
        o_ref, acc_ref, hid_a_ref, hid_b_ref, p_ref = rest
    g = pl.program_id(0)
    eb, tc = hid_a_ref.shape
    rows_per_step = eb // N_SUBKEYS
    mix_block = jnp.maximum(g - 1, 0) % n_blocks

    @pl.when(mix_block == 0)
    def _():
        acc_ref[...] = jnp.zeros_like(acc_ref)

    @pl.when(g == 0)
    def _():
        hid_b_ref[...] = jnp.zeros_like(hid_b_ref)

    def step(cur_ref, prev_ref):
        if convert:
            _convert_expert_slice(cu_ref, cv_ref, ou_ref, ovt_ref)
        cur_ref[...] = lax.dot_general(u_ref[...], xf_ref[...], NT_DIMS, preferred_element_type=F32)
        for kk in range(rows_per_step):
            gate = None
            for h in range(PEER_HEADS):
                lim = jnp.broadcast_to(lim_ref[h, 0, kk:kk + 1, :], (BF16_ROWS, tc)).astype(BF16)
                a1 = jnp.broadcast_to(a1_ref[h, 0, kk:kk + 1, :], (BF16_ROWS, tc)).astype(BF16)
                lim = jnp.concatenate([lim] * (N_SUBKEYS // BF16_ROWS), axis=0)
                a1 = jnp.concatenate([a1] * (N_SUBKEYS // BF16_ROWS), axis=0)
                w = jnp.where(r2_ref[h] < lim, a2_ref[h] * a1, jnp.zeros((), BF16))
                gate = w if gate is None else gate + w
            sl = slice(kk * N_SUBKEYS, (kk + 1) * N_SUBKEYS)
            hid = prev_ref[sl, :]
            act = 0.5 * hid * (1.0 + lax.erf(hid * INV_SQRT2))
            p_ref[sl, :] = act.astype(BF16) * gate
        acc_ref[...] += jnp.dot(vt_ref[...], p_ref[...], preferred_element_type=F32)

    @pl.when(g % 2 == 0)
    def _():
        step(hid_a_ref, hid_b_ref)

    @pl.when(g % 2 == 1)
    def _():
        step(hid_b_ref, hid_a_ref)

    @pl.when(jnp.logical_and(mix_block == n_blocks - 1, g > 0))
    def _():
        o_ref[...] = h_ref[...] + g_ref[0] * acc_ref[...].T


def peer_mix(xf, u, vt, r2, a2, lim, a1, h, gate, *, seq, next_tables=None, tc=512):
    t, d = h.shape
    rows = lim.shape[2]
    eb = rows * N_SUBKEYS
    n_blocks = u.shape[0] // eb
    n_items = (t // tc) * n_blocks
    per_batch = seq // tc

    def hid_item(g):
        return jnp.minimum(g, n_items - 1)

    def mix_item(g):
        return jnp.maximum(g - 1, 0)

    tab_spec = pl.BlockSpec((PEER_HEADS, N_SUBKEYS, tc), lambda g: (0, 0, mix_item(g) // n_blocks))
    row_spec = pl.BlockSpec((PEER_HEADS, 1, rows, tc),
                            lambda g: (0, mix_item(g) % n_blocks, 0, mix_item(g) // n_blocks))
    chunk_spec = pl.BlockSpec((tc, d), lambda g: (mix_item(g) // n_blocks, 0))
    cv_in, cv_out, cv_shape, cv_args = [], [], [], []
    if next_tables is not None:
        peer_u, peer_v, layer = next_tables
        assert n_items + 1 >= peer_u.shape[1] // CONVERT_ROWS
        cv_in, cv_out, cv_shape = _convert_specs(peer_u, peer_v, layer, lambda g: g)
        cv_args = [peer_u, peer_v]
    out = pl.pallas_call(
        functools.partial(_peer_mix_kernel, n_blocks=n_blocks, convert=bool(cv_args)),
        grid=(n_items + 1,),
        in_specs=[
            pl.BlockSpec((tc, d), lambda g: (hid_item(g) // n_blocks, 0)),
            pl.BlockSpec((eb, d), lambda g: (hid_item(g) % n_blocks, 0)),
            pl.BlockSpec((d, eb), lambda g: (0, mix_item(g) % n_blocks)),
            tab_spec, tab_spec, row_spec, row_spec,
            chunk_spec,
            pl.BlockSpec((1, 1, d), lambda g: (mix_item(g) // n_blocks // per_batch, 0, 0)),
        ] + cv_in,
        out_specs=[chunk_spec] + cv_out,
        out_shape=[jax.ShapeDtypeStruct((t, d), F32)] + cv_shape,
        scratch_shapes=[pltpu.VMEM((d, tc), F32),
                        pltpu.VMEM((eb, tc), F32),
                        pltpu.VMEM((eb, tc), F32),
                        pltpu.VMEM((eb, tc), BF16)],
        compiler_params=_params("arbitrary"),
        name="peer_mix",
    )(xf, u, vt, r2, a2, lim, a1, h, gate, *cv_args)
    return out if cv_args else out[0]


def kernel(x, c, rel_bias, norm1_g, norm2_g, w_ada, b_ada, a_w_in, a_q_gain, a_k_gain,
           b_w_in, w_out, peer_w_query, peer_sub_keys, peer_u, peer_v):
    b, s, d = x.shape
    depth = w_ada.shape[0]
    n_groups = len(DIL_GROUPS)
    heads = d // HEAD_DIM

    c_pad = jnp.pad(c, ((0, ADALN_ROWS - b), (0, 0)))
    mod = adaln(c_pad, w_ada, b_ada, layers=depth)
    a_w = a_w_in.astype(BF16)
    b_w = w_o = None
    wq_t = jnp.swapaxes(peer_w_query, 1, 2).astype(BF16)
    keys = peer_sub_keys.astype(BF16)
    u = vt = None
    h = x.reshape(b * s, d)
    for i in range(depth):
        sh1, sc1, g1, sh2, sc2, g2 = (mod[i, :b, k * d:(k + 1) * d][:, None, :] for k in range(6))
        j = i // 2
        if i % 2 == 0:
            ones = jnp.ones((HEAD_DIM,), F32)
            gain = jnp.stack([jnp.tile(row, heads) for g in range(n_groups)
                              for row in (a_q_gain[j, g] / math.sqrt(HEAD_DIM), a_k_gain[j, g], ones)])
            assert i == 0
            qkvs, u, vt, b_w, w_o = in_proj_dilated(
                h, norm1_g[i][None], sh1, sc1, a_w, gain[:, None, :], peer_u, peer_v,
                (b_w_in.reshape(-1, b_w_in.shape[2]), w_out.reshape(-1, d)), layer=j, peer_layer=i, seq=s)
            b_w, w_o = b_w.reshape(b_w_in.shape), w_o.reshape(w_out.shape)
            outs, lses = [], []
            for g, (window, dil) in enumerate(DIL_GROUPS):
                bias = _band_bias(rel_bias[:, g * heads:(g + 1) * heads], window, dil)
                o, lse = dilated_group_attention(qkvs[g], bias, n_heads=heads)
                outs.append(o)
                lses.append(lse)
            h = out_proj_merge(outs, lses, w_o, h, g1, layer=i, seq=s)
        else:
            qkv = in_proj(h, norm1_g[i][None], sh1, sc1, b_w, layer=j, seq=s)
            y = stick_breaking_attention(qkv.reshape(b, s, -1), n_heads=heads)
            h = out_proj(y.reshape(b * s, d), w_o, h, g1, layer=i, seq=s)
        xf, r2, a2, lim, a1 = peer_select(h, norm2_g[i][None], sh2, sc2, wq_t, keys, layer=i, seq=s,
                                          rows=PEER_BLOCK_ROWS)
        if i + 1 < depth:
            h, u, vt = peer_mix(xf, u, vt, r2, a2, lim, a1, h, g2, seq=s,
                                next_tables=(peer_u, peer_v, i + 1))
        else:
            h = peer_mix(xf, u, vt, r2, a2, lim, a1, h, g2, seq=s)
    return h.reshape(b, s, d)
```

```python
import functools
import math

import jax
import jax.numpy as jnp
from jax import lax
from jax.experimental import pallas as pl
from jax.experimental.pallas import tpu as pltpu

F32 = jnp.float32
BF16 = jnp.bfloat16

LANES = 128
BF16_ROWS = 16
HEAD_DIM = 128
BLOCK = 128
DIL_GROUPS = ((128, 1), (512, 4), (2048, 16))
REL_BUCKETS = 32
REL_MAX_DIST = 2048
PEER_HEADS = 8
N_SUBKEYS = 128
PEER_TOPK = 16
PEER_BLOCK_ROWS = 8
EPS = 1e-6
NEG_BIG = -1e30
SB_DEAD_DECAY = 104.0
SB_HEADS_PER_STEP = 8
SB_Q_CHUNK = 1024
INV_SQRT2 = 0.7071067811865476

VMEM_LIMIT = 60 * 1024 * 1024

NT_DIMS = (((1,), (1,)), ((), ()))


def _params(*sem):
    return pltpu.CompilerParams(dimension_semantics=sem, vmem_limit_bytes=VMEM_LIMIT)


def _adaln_kernel(c_ref, w_lo_ref, w_hi_ref, b_ref, o_ref):
    c = c_ref[...]
    cs = c / (1.0 + jnp.exp(-c))
    half = w_lo_ref.shape[1]
    dot = functools.partial(jnp.dot, preferred_element_type=F32, precision=lax.Precision.HIGHEST)
    o_ref[0] = dot(cs[:, :half], w_lo_ref[0]) + dot(cs[:, half:], w_hi_ref[0]) + b_ref[0]


ADALN_ROWS = 8


def adaln(c_pad, w_ada, b_ada, *, layers, tn=1024):
    depth, d, n = w_ada.shape
    rows = c_pad.shape[0]
    return pl.pallas_call(
        _adaln_kernel,
        grid=(layers, n // tn),
        in_specs=[
            pl.BlockSpec((rows, d), lambda i, j: (0, 0)),
            pl.BlockSpec((1, d // 2, tn), lambda i, j: (i, 0, j)),
            pl.BlockSpec((1, d // 2, tn), lambda i, j: (i, 1, j)),
            pl.BlockSpec((1, 1, tn), lambda i, j: (i, 0, j)),
        ],
        out_specs=pl.BlockSpec((1, rows, tn), lambda i, j: (i, 0, j)),
        out_shape=jax.ShapeDtypeStruct((layers, rows, n), F32),
        compiler_params=_params("arbitrary", "arbitrary"),
        name="adaln",
    )(c_pad, w_ada, w_ada, b_ada.reshape(depth, 1, n))


CONVERT_ROWS = 128


def _convert_expert_slice(cu_ref, cv_ref, ou_ref, ovt_ref):
    ou_ref[...] = cu_ref[...].astype(BF16)
    ovt_ref[...] = cv_ref[...].T.astype(BF16)


def _convert_specs(peer_u, peer_v, layer, step_of):
    n_exp, d = peer_u.shape[1:]
    last = n_exp // CONVERT_ROWS - 1

    def slice_of(*idx):
        return jnp.minimum(step_of(*idx), last)

    src = pl.BlockSpec((None, CONVERT_ROWS, d), lambda *idx: (layer, slice_of(*idx), 0))
    in_specs = [src, src]
    out_specs = [pl.BlockSpec((CONVERT_ROWS, d), lambda *idx: (slice_of(*idx), 0)),
                 pl.BlockSpec((d, CONVERT_ROWS), lambda *idx: (0, slice_of(*idx)))]
    out_shape = [jax.ShapeDtypeStruct((n_exp, d), BF16), jax.ShapeDtypeStruct((d, n_exp), BF16)]
    return in_specs, out_specs, out_shape


def _norm_modulate(x, g, shift, scale):
    ms = jnp.mean(x * x, axis=-1, keepdims=True)
    y = x * lax.rsqrt(ms + EPS) * g
    return y * (1.0 + scale) + shift


def _in_proj_kernel(h_ref, g_ref, sh_ref, sc_ref, w_ref, o_ref, xm_ref):
    @pl.when(pl.program_id(1) == 0)
    def _():
        xm_ref[...] = _norm_modulate(h_ref[...], g_ref[...], sh_ref[0], sc_ref[0]).astype(BF16)

    o_ref[...] = jnp.dot(xm_ref[...], w_ref[...], preferred_element_type=F32).astype(o_ref.dtype)


def in_proj(h, g, shift, scale, w, *, layer, seq, tm=1024, tn=2048):
    t, d = h.shape
    n = w.shape[2]
    per_batch = seq // tm
    return pl.pallas_call(
        _in_proj_kernel,
        grid=(t // tm, n // tn),
        in_specs=[
            pl.BlockSpec((tm, d), lambda i, j: (i, 0)),
            pl.BlockSpec((1, d), lambda i, j: (0, 0)),
            pl.BlockSpec((1, 1, d), lambda i, j: (i // per_batch, 0, 0)),
            pl.BlockSpec((1, 1, d), lambda i, j: (i // per_batch, 0, 0)),
            pl.BlockSpec((None, d, tn), lambda i, j: (layer, 0, j)),
        ],
        out_specs=pl.BlockSpec((tm, tn), lambda i, j: (i, j)),
        out_shape=jax.ShapeDtypeStruct((t, n), BF16),
        scratch_shapes=[pltpu.VMEM((tm, d), BF16)],
        compiler_params=_params("arbitrary", "arbitrary"),
        name="in_proj",
    )(h, g, shift, scale, w)


def _in_proj_dilated_kernel(h_ref, g_ref, sh_ref, sc_ref, w_ref, gain_ref,
                            cu_ref, cv_ref, cb_ref, co_ref,
                            o0_ref, o1_ref, o2_ref, ou_ref, ovt_ref, ob_ref, oo_ref,
                            xm0_ref, xm1_ref, xm2_ref, slab_ref, *, parts):
    j = pl.program_id(1)
    seg = j // parts
    tm, d = xm0_ref.shape
    xm_refs = (xm0_ref, xm1_ref, xm2_ref)

    def side_jobs():
        _convert_expert_slice(cu_ref, cv_ref, ou_ref, ovt_ref)
        ob_ref[...] = cb_ref[...].astype(BF16)
        oo_ref[...] = co_ref[...].astype(BF16)

    @pl.when(j == 0)
    def _():
        xm = _norm_modulate(h_ref[...], g_ref[...], sh_ref[0], sc_ref[0])
        for sb in range(d // HEAD_DIM):
            slab_ref[sb] = xm[:, sb * HEAD_DIM:(sb + 1) * HEAD_DIM]
        for (_, dil), xm_ref in zip(DIL_GROUPS, xm_refs):
            if dil == 1:
                xm_ref[...] = xm.astype(BF16)
                continue
            rows = tm // dil
            for sb in range(d // HEAD_DIM):
                for r in range(dil):
                    xm_ref[r * rows:(r + 1) * rows, sb * HEAD_DIM:(sb + 1) * HEAD_DIM] = (
                        slab_ref[sb, pl.ds(r, rows, stride=dil), :].astype(BF16))

    def project(xm_ref, o_ref, dil):
        rows = tm // dil
        slab = min(2 * HEAD_DIM, w_ref.shape[1])
        n_slabs = w_ref.shape[1] // slab

        @pl.when(seg % 3 == 2)
        def _():
            side_jobs()
            for sb in range(n_slabs):
                cols = slice(sb * slab, (sb + 1) * slab)
                acc = jnp.dot(xm_ref[...], w_ref[:, cols], preferred_element_type=F32).astype(BF16)
                for r in range(dil):
                    o_ref[0, r, :, cols] = acc[r * rows:(r + 1) * rows]

        @pl.when(seg % 3 != 2)
        def _():
            side_jobs()
            for sb in range(n_slabs):
                acc = jnp.dot(xm_ref[...], w_ref[:, sb * slab:(sb + 1) * slab], preferred_element_type=F32)
                for hh in range(slab // HEAD_DIM):
                    sl = slice(sb * slab + hh * HEAD_DIM, sb * slab + (hh + 1) * HEAD_DIM)
                    a = acc[:, hh * HEAD_DIM:(hh + 1) * HEAD_DIM]
                    ms = jnp.mean(a * a, axis=-1, keepdims=True)
                    y = (a * lax.rsqrt(ms + EPS) * gain_ref[0, :, sl]).astype(BF16)
                    for r in range(dil):
                        o_ref[0, r, :, sl] = y[r * rows:(r + 1) * rows]

    for grp, ((_, dil), xm_ref, o_ref) in enumerate(zip(DIL_GROUPS, xm_refs, (o0_ref, o1_ref, o2_ref))):
        @pl.when(seg // 3 == grp)
        def _(dil=dil, xm_ref=xm_ref, o_ref=o_ref):
            project(xm_ref, o_ref, dil)


def in_proj_dilated(h, g, shift, scale, w, gain, peer_u, peer_v, casts, *,
                    layer, peer_layer, seq, tm=512, tn=2048):
    t, d = h.shape
    n = w.shape[2]
    b = t // seq
    per_batch = seq // tm
    n_groups = len(DIL_GROUPS)
    width = gain.shape[2]
    parts = width // tn
    assert n == n_groups * 3 * width

    def out_spec(grp, dil):
        def imap(i, j):
            return (i // per_batch, 0, i % per_batch, jnp.clip(j - 3 * parts * grp, 0, 3 * parts - 1))
        return pl.BlockSpec((1, dil, tm // dil, tn), imap)

    n_cols = n // tn
    n_steps = (t // tm) * n_cols
    step = lambda i, j: i * n_cols + j
    assert n_steps >= peer_u.shape[1] // CONVERT_ROWS
    cv_in, cv_out, cv_shape = _convert_specs(peer_u, peer_v, peer_layer, step)
    side_in, side_out, side_shape = [], [], []
    for a in casts:
        rows = a.shape[0] // (peer_u.shape[1] // CONVERT_ROWS)
        last = a.shape[0] // rows - 1
        spec = pl.BlockSpec((rows, a.shape[1]), lambda i, j, last=last: (jnp.minimum(step(i, j), last), 0))
        side_in.append(spec)
        side_out.append(spec)
        side_shape.append(jax.ShapeDtypeStruct(a.shape, BF16))
    *qkvs, u_bf, vt_bf, cast0, cast1 = pl.pallas_call(
        functools.partial(_in_proj_dilated_kernel, parts=parts),
        grid=(t // tm, n // tn),
        in_specs=[
            pl.BlockSpec((tm, d), lambda i, j: (i, 0)),
            pl.BlockSpec((1, d), lambda i, j: (0, 0)),
            pl.BlockSpec((1, 1, d), lambda i, j: (i // per_batch, 0, 0)),
            pl.BlockSpec((1, 1, d), lambda i, j: (i // per_batch, 0, 0)),
            pl.BlockSpec((None, d, tn), lambda i, j: (layer, 0, j)),
            pl.BlockSpec((1, 1, tn), lambda i, j: (j // parts, 0, j % parts)),
        ] + cv_in + side_in,
        out_specs=[out_spec(grp, dil) for grp, (_, dil) in enumerate(DIL_GROUPS)] + cv_out + side_out,
        out_shape=[jax.ShapeDtypeStruct((b, dil, seq // dil, 3 * width), BF16)
                   for _, dil in DIL_GROUPS] + cv_shape + side_shape,
        scratch_shapes=[pltpu.VMEM((tm, d), BF16)] * 3 + [pltpu.VMEM((d // HEAD_DIM, tm, HEAD_DIM), F32)],
        compiler_params=_params("arbitrary", "arbitrary"),
        name="in_proj_dilated",
    )(h, g, shift, scale, w, gain, peer_u, peer_v, *casts)
    return qkvs, u_bf, vt_bf, cast0, cast1


def _dil_attn_kernel(q_ref, kp_ref, kc_ref, vp_ref, vc_ref, bias_ref, o_ref, lse_ref, *, n_heads):
    n = pl.program_id(2)
    n_sub = q_ref.shape[2] // BLOCK
    lane = lax.broadcasted_iota(jnp.int32, (1, 2 * BLOCK), 1)
    pen = jnp.where(jnp.logical_and(lane < BLOCK, n == 0), NEG_BIG, 0.0).astype(F32)
    ones = jnp.ones((2 * BLOCK, HEAD_DIM), BF16)
    lse_lanes = BLOCK // n_heads
    lane_head = lax.broadcasted_iota(jnp.int32, (1, BLOCK), 1) // lse_lanes
    for sb in range(n_sub):
        rows = slice(sb * BLOCK, (sb + 1) * BLOCK)
        prev = slice((sb - 1) * BLOCK, sb * BLOCK)
        lse_pack = jnp.zeros((BLOCK, BLOCK), F32)
        for hh in range(n_heads):
            sl = slice(hh * HEAD_DIM, (hh + 1) * HEAD_DIM)
            q = q_ref[0, 0, rows, sl]
            k_prev = kp_ref[0, 0, :, sl] if sb == 0 else kc_ref[0, 0, prev, sl]
            v_prev = vp_ref[0, 0, :, sl] if sb == 0 else vc_ref[0, 0, prev, sl]
            k = jnp.concatenate([k_prev, kc_ref[0, 0, rows, sl]], axis=0)
            v = jnp.concatenate([v_prev, vc_ref[0, 0, rows, sl]], axis=0)
            logits = lax.dot_general(q, k, NT_DIMS, preferred_element_type=F32)
            logits = logits + bias_ref[hh]
            if sb == 0:
                logits = logits + pen
            m = jnp.max(logits, axis=-1, keepdims=True)
            p = jnp.exp(logits - m).astype(BF16)
            v_ext = jnp.concatenate([v, ones], axis=1)
            pv = jnp.dot(p, v_ext, preferred_element_type=F32)
            denom = pv[:, HEAD_DIM:]
            o_ref[0, 0, rows, sl] = (pv[:, :HEAD_DIM] / denom).astype(o_ref.dtype)
            lse_pack = jnp.where(lane_head == hh, m + jnp.log(denom), lse_pack)
        lse_ref[0, 0, rows, :] = lse_pack


def dilated_group_attention(qkv, bias, *, n_heads, q_blocks=4):
    b, dil, sub_len, c = qkv.shape
    assert BLOCK % n_heads == 0
    width = n_heads * HEAD_DIM
    q_blocks = min(q_blocks, sub_len // BLOCK)
    rows = q_blocks * BLOCK
    steps = sub_len // rows

    def cur(which):
        return pl.BlockSpec((1, 1, rows, width), lambda bi, r, n: (bi, r, n, which))

    def prev(which):
        return pl.BlockSpec((1, 1, BLOCK, width),
                            lambda bi, r, n: (bi, r, jnp.maximum(q_blocks * n - 1, 0), which))

    return pl.pallas_call(
        functools.partial(_dil_attn_kernel, n_heads=n_heads),
        grid=(b, dil, steps),
        in_specs=[cur(0), prev(1), cur(1), prev(2), cur(2),
                  pl.BlockSpec((n_heads, BLOCK, 2 * BLOCK), lambda bi, r, n: (0, 0, 0))],
        out_specs=[pl.BlockSpec((1, 1, rows, width), lambda bi, r, n: (bi, r, n, 0)),
                   pl.BlockSpec((1, 1, rows, BLOCK), lambda bi, r, n: (bi, r, n, 0))],
        out_shape=[jax.ShapeDtypeStruct((b, dil, sub_len, width), BF16),
                   jax.ShapeDtypeStruct((b, dil, sub_len, BLOCK), F32)],
        compiler_params=_params("arbitrary", "arbitrary", "arbitrary"),
        name=f"dilated_attn_d{dil}",
    )(qkv, qkv, qkv, qkv, qkv, bias)


def _rel_bucket(dist):
    exact = REL_BUCKETS // 2
    d_f = jnp.maximum(dist, exact).astype(F32)
    log_b = exact + (jnp.log(d_f / exact) / math.log(REL_MAX_DIST / exact)
                     * (REL_BUCKETS - exact)).astype(jnp.int32)
    return jnp.where(dist < exact, dist, jnp.minimum(log_b, REL_BUCKETS - 1))


def _band_bias(table_g, window, dil):
    w_steps = window // dil
    period = 3 * BLOCK
    per_delta = table_g[_rel_bucket(jnp.arange(w_steps + 1) * dil)].astype(F32)
    row = jnp.full((period, table_g.shape[1]), NEG_BIG, F32)
    row = row.at[BLOCK - w_steps:BLOCK + 1].set(per_delta[::-1])
    flat = jnp.tile(row.T, (1, BLOCK))[:, :BLOCK * (period - 1)]
    return flat.reshape(-1, BLOCK, period - 1)[:, :, :2 * BLOCK]


def _out_proj_merge_kernel(o0_ref, o1_ref, o2_ref, l0_ref, l1_ref, l2_ref, w_ref, h_ref, g_ref, out_ref,
                           os1_ref, os2_ref, ls1_ref, ls2_ref, ya_ref, yb_ref):
    g = pl.program_id(0)
    tm = out_ref.shape[0]
    n_heads = ya_ref.shape[1] // HEAD_DIM

    @pl.when(g == 0)
    def _():
        yb_ref[...] = jnp.zeros_like(yb_ref)

    def to_token_order(src_ref, dst_ref, sl):
        dil = src_ref.shape[1]
        for r in range(dil):
            dst_ref[pl.ds(r, tm // dil, stride=dil), :] = src_ref[0, r, :, sl].astype(F32)

    def step(y_ref, prev_y_ref):
        to_token_order(l1_ref, ls1_ref, slice(None))
        to_token_order(l2_ref, ls2_ref, slice(None))
        l0, l1, l2 = l0_ref[0, 0], ls1_ref[...], ls2_ref[...]
        m = jnp.maximum(jnp.maximum(l0, l1), l2)
        e0, e1, e2 = jnp.exp(l0 - m), jnp.exp(l1 - m), jnp.exp(l2 - m)
        inv = 1.0 / (e0 + e1 + e2)
        a0, a1, a2 = e0 * inv, e1 * inv, e2 * inv
        for hh in range(n_heads):
            sl = slice(hh * HEAD_DIM, (hh + 1) * HEAD_DIM)
            col = slice(hh * (BLOCK // n_heads), hh * (BLOCK // n_heads) + 1)
            to_token_order(o1_ref, os1_ref, sl)
            to_token_order(o2_ref, os2_ref, sl)
            y = (a0[:, col] * o0_ref[0, 0, :, sl].astype(F32) + a1[:, col] * os1_ref[...]
                 + a2[:, col] * os2_ref[...])
            y_ref[:, sl] = y.astype(BF16)
        proj = jnp.dot(prev_y_ref[...], w_ref[...], preferred_element_type=F32)
        out_ref[...] = h_ref[...] + g_ref[0] * proj

    @pl.when(g % 2 == 0)
    def _():
        step(ya_ref, yb_ref)

    @pl.when(g % 2 == 1)
    def _():
        step(yb_ref, ya_ref)


def out_proj_merge(outs, lses, w, h, gate, *, layer, seq, tm=256):
    t, d = h.shape
    width = w.shape[1]
    per_batch = seq // tm
    n_steps = t // tm

    def merged(g):
        return jnp.minimum(g, n_steps - 1)

    def projected(g):
        return jnp.maximum(g - 1, 0)

    row = pl.BlockSpec((tm, d), lambda g: (projected(g), 0))

    def grp_spec(a):
        dil, cols = a.shape[1], a.shape[3]
        return pl.BlockSpec((1, dil, tm // dil, cols),
                            lambda g: (merged(g) // per_batch, 0, merged(g) % per_batch, 0))

    return pl.pallas_call(
        _out_proj_merge_kernel,
        grid=(n_steps + 1,),
        in_specs=[grp_spec(a) for a in (*outs, *lses)] + [
            pl.BlockSpec((None, width, d), lambda g: (layer, 0, 0)),
            row,
            pl.BlockSpec((1, 1, d), lambda g: (projected(g) // per_batch, 0, 0)),
        ],
        out_specs=row,
        out_shape=jax.ShapeDtypeStruct((t, d), F32),
        scratch_shapes=[pltpu.VMEM((tm, HEAD_DIM), F32)] * 4 + [pltpu.VMEM((tm, width), BF16)] * 2,
        compiler_params=_params("arbitrary"),
        name="out_proj_merge",
    )(*outs, *lses, w, h, gate)


def _out_proj_kernel(y_ref, w_ref, h_ref, g_ref, out_ref):
    proj = jnp.dot(y_ref[...], w_ref[...], preferred_element_type=F32)
    out_ref[...] = h_ref[...] + g_ref[0] * proj


def out_proj(y, w, h, gate, *, layer, seq, tm=512):
    t, d = h.shape
    width = w.shape[1]
    per_batch = seq // tm
    row = lambda cols: pl.BlockSpec((tm, cols), lambda i: (i, 0))
    return pl.pallas_call(
        _out_proj_kernel,
        grid=(t // tm,),
        in_specs=[row(width), pl.BlockSpec((None, width, d), lambda i: (layer, 0, 0)), row(d),
                  pl.BlockSpec((1, 1, d), lambda i: (i // per_batch, 0, 0))],
        out_specs=row(d),
        out_shape=jax.ShapeDtypeStruct((t, d), F32),
        compiler_params=_params("arbitrary"),
        name="out_proj",
    )(y, w, h, gate)


def _split2(x):
    hi = x.astype(BF16)
    lo = (x - hi.astype(F32)).astype(BF16)
    return hi, lo


def _sb_kernel(q_ref, k_ref, v_ref, o_ref, acc_ref, decay_ref, *, scale):
    nq = q_ref.shape[1] // BLOCK
    q_base = pl.program_id(2) * q_ref.shape[1]
    n_heads = acc_ref.shape[0]
    row = lax.broadcasted_iota(jnp.int32, (BLOCK, BLOCK), 0)
    col = lax.broadcasted_iota(jnp.int32, (BLOCK, BLOCK), 1)
    before = col < row

    def tri_ext(n_keys):
        r = lax.broadcasted_iota(jnp.int32, (n_keys, n_keys), 0)
        c = lax.broadcasted_iota(jnp.int32, (n_keys, n_keys), 1)
        tri = jnp.where(r > c, 1.0, 0.0).astype(BF16)
        return jnp.concatenate([tri, jnp.ones((n_keys, BLOCK), BF16)], axis=1)

    tri1, tri2 = tri_ext(BLOCK), tri_ext(2 * BLOCK)

    def tails_and_sums(sps, tri):
        n_keys = tri.shape[0]
        parts = [t for sp in sps for t in _split2(sp)]
        r = jnp.dot(jnp.concatenate(parts, axis=0), tri, preferred_element_type=F32)
        out = []
        for g in range(len(sps)):
            hi = r[(2 * g) * BLOCK:(2 * g + 1) * BLOCK]
            lo = r[(2 * g + 1) * BLOCK:(2 * g + 2) * BLOCK]
            both = hi + lo
            out.append((both[:, :n_keys], both[:, n_keys:]))
        return out

    def softplus(z):
        return jnp.maximum(z, 0.0) + jnp.log1p(jnp.exp(-jnp.abs(z)))

    def logits(q0, k0, g, n_keys=BLOCK):
        cs = slice(g * HEAD_DIM, (g + 1) * HEAD_DIM)
        return lax.dot_general(q_ref[0, pl.ds(q0, BLOCK), cs], k_ref[0, pl.ds(k0, n_keys), cs],
                               NT_DIMS, preferred_element_type=F32) * scale

    def values(k0, g, n_keys=BLOCK):
        return v_ref[0, pl.ds(k0, n_keys), g * HEAD_DIM:(g + 1) * HEAD_DIM]

    def qblock(i, carry):
        q0 = pl.multiple_of(i * BLOCK, BLOCK)
        d0 = pl.multiple_of(q_base + q0, BLOCK)
        zs = [logits(q0, d0, g) for g in range(n_heads)]
        sps = [softplus(z) for z in zs]
        ts = tails_and_sums([jnp.where(before, sp, 0.0) for sp in sps], tri1)
        live = None
        for g in range(n_heads):
            tail, rsum = ts[g]
            a = jnp.where(before, jnp.exp(zs[g] - sps[g] - tail), 0.0)
            acc_ref[g] = jnp.dot(a.astype(BF16), values(d0, g), preferred_element_type=F32)
            decay_ref[g] = rsum
            lo = jnp.min(rsum)
            live = lo if live is None else jnp.minimum(live, lo)

        def walk(k0, tri):
            n_keys = tri.shape[0]
            zs = [logits(q0, k0, g, n_keys) for g in range(n_heads)]
            sps = [softplus(z) for z in zs]
            ts = tails_and_sums(sps, tri)
            live = None
            for g in range(n_heads):
                tail, rsum = ts[g]
                decay = decay_ref[g]
                later = jnp.concatenate([decay] * (n_keys // BLOCK), axis=1)
                a = jnp.exp(zs[g] - sps[g] - (later + tail))
                acc_ref[g] += jnp.dot(a.astype(BF16), values(k0, g, n_keys), preferred_element_type=F32)
                decay = decay + rsum
                decay_ref[g] = decay
                lo = jnp.min(decay)
                live = lo if live is None else jnp.minimum(live, lo)
            return live

        def cond(c):
            j, live = c
            return jnp.logical_and(j >= 1, live < SB_DEAD_DECAY)

        def body(c):
            j, _ = c
            return j - 2, walk(pl.multiple_of((j - 1) * BLOCK, BLOCK), tri2)

        j, live = lax.while_loop(cond, body, (d0 // BLOCK - 1, live))

        @pl.when(jnp.logical_and(j == 0, live < SB_DEAD_DECAY))
        def _():
            walk(0, tri1)

        for g in range(n_heads):
            o_ref[0, pl.ds(q0, BLOCK), g * HEAD_DIM:(g + 1) * HEAD_DIM] = acc_ref[g].astype(o_ref.dtype)
        return carry

    lax.fori_loop(0, nq, qblock, 0)


def stick_breaking_attention(qkv, *, n_heads):
    b, s, _ = qkv.shape
    hps = min(SB_HEADS_PER_STEP, n_heads)
    steps = n_heads // hps
    width = hps * HEAD_DIM
    qc = min(SB_Q_CHUNK, s)

    def spec(which):
        return pl.BlockSpec((1, s, width), lambda bi, h, c: (bi, 0, which * steps + h))

    return pl.pallas_call(
        functools.partial(_sb_kernel, scale=1.0 / math.sqrt(HEAD_DIM)),
        grid=(b, steps, s // qc),
        in_specs=[pl.BlockSpec((1, qc, width), lambda bi, h, c: (bi, c, h)), spec(1), spec(2)],
        out_specs=pl.BlockSpec((1, qc, width), lambda bi, h, c: (bi, c, h)),
        out_shape=jax.ShapeDtypeStruct((b, s, n_heads * HEAD_DIM), BF16),
        scratch_shapes=[pltpu.VMEM((hps, BLOCK, HEAD_DIM), F32),
                        pltpu.VMEM((hps, BLOCK, BLOCK), F32)],
        compiler_params=_params("arbitrary", "arbitrary", "arbitrary"),
        name="stick_breaking",
    )(qkv, qkv, qkv)


UNRANKED = 127.0
RANK_MARK = -(2.0 ** 100)


def _as_f32(r):
    return jnp.asarray(r).astype(F32)


def _extract_ranked(s, dst_ref, count):
    n = s.shape[0]
    idx = lax.broadcasted_iota(jnp.int32, s.shape, 0).astype(F32)

    def body(r, carry):
        s, rank = carry
        m = jnp.max(s, axis=0, keepdims=True)
        first = jnp.min(jnp.where(s == m, idx, float(n)), axis=0, keepdims=True)
        pick = idx == first
        dst_ref[pl.ds(r, 1), :] = m
        return jnp.where(pick, -jnp.inf, s), jnp.where(pick, _as_f32(r), rank)

    _, rank = lax.fori_loop(0, count, body, (s, jnp.full(s.shape, UNRANKED, F32)))
    return rank


def _candidates(sv1, sv2):
    k = PEER_TOPK
    groups = [sv1[0:1] + sv2]
    groups += [sv1[a:a + 1] + sv2[0:k // 2] for a in range(1, k // 2)]
    groups += [sv1[k // 2:] + sv2[0:1]]
    spans = [(0, k)] + [(k + (a - 1) * (k // 2), k // 2) for a in range(1, k // 2)]
    tail0 = k + (k // 2 - 1) * (k // 2)
    spans += [(tail0 + a, 1) for a in range(k // 2)]
    return jnp.concatenate(groups, axis=0), spans


def _staircase_counts(taken, spans):
    return [jnp.sum(taken[lo:lo + n], axis=0, keepdims=True) for lo, n in spans]


def _peer_select_kernel(h_ref, g_ref, sh_ref, sc_ref, wq_ref, keys_ref,
                        xf_ref, r2_ref, a2_ref, lim_ref, a1_ref,
                        qt_ref, sv1_tiles, sv2_tiles, best_ref):
    xf = _norm_modulate(h_ref[...], g_ref[...], sh_ref[0], sc_ref[0]).astype(BF16)
    xf_ref[...] = xf
    qt_ref[...] = lax.dot_general(wq_ref[...], xf, NT_DIMS, preferred_element_type=F32).astype(BF16)
    k = PEER_TOPK
    half = N_SUBKEYS
    kf = float(k)
    n_tiles = sv2_tiles.shape[0]

    def head(h, carry):
        r0 = pl.multiple_of(h * 2 * half, 2 * half)
        s1 = jnp.dot(keys_ref[h, 0], qt_ref[pl.ds(r0, half), :], preferred_element_type=F32)
        s2 = jnp.dot(keys_ref[h, 1], qt_ref[pl.ds(r0 + half, half), :], preferred_element_type=F32)

        def sorted1():
            return jnp.concatenate([sv1_tiles[lt] for lt in range(n_tiles)], axis=1)

        def sorted2():
            return jnp.concatenate([sv2_tiles[lt] for lt in range(n_tiles)], axis=1)

        def emit(rank2, lim):
            sv1, sv2, best = sorted1(), sorted2(), best_ref[...]
            z = jnp.sum(jnp.exp(best - best[0:1]), axis=0, keepdims=True)
            r2_ref[h] = rank2.astype(BF16)
            a2_ref[h] = jnp.exp(s2 - sv2[0:1]).astype(BF16)
            a1 = jnp.exp(s1 - sv1[0:1]) / z
            rows = lim_ref.shape[2]
            for grp in range(half // rows):
                lim_ref[h, grp] = lim[grp * rows:(grp + 1) * rows]
                a1_ref[h, grp] = a1[grp * rows:(grp + 1) * rows]

        marked = []
        for lt in range(n_tiles):
            def round12(r, c, lt=lt):
                w1, w2 = c
                m1 = jnp.max(w1, axis=0, keepdims=True)
                m2 = jnp.max(w2, axis=0, keepdims=True)
                sv1_tiles[lt, pl.ds(r, 1), :] = m1
                sv2_tiles[lt, pl.ds(r, 1), :] = m2
                mark = RANK_MARK * (1.0 + _as_f32(r) / kf)
                return jnp.where(w1 == m1, -jnp.inf, w1), jnp.where(w2 == m2, mark, w2)

            ls = slice(lt * LANES, (lt + 1) * LANES)
            marked.append(lax.fori_loop(0, k, round12, (s1[:, ls], s2[:, ls]), unroll=True)[1])
        w2 = jnp.concatenate(marked, axis=1)
        rank2 = jnp.where(w2 <= RANK_MARK, w2 * (kf / RANK_MARK) - kf, UNRANKED)

        sv1 = sorted1()
        cand, spans = _candidates(sv1, sorted2())

        def round_c(r, w):
            m = jnp.max(w, axis=0, keepdims=True)
            best_ref[pl.ds(r, 1), :] = m
            return jnp.where(w == m, -jnp.inf, w)

        lax.fori_loop(0, k, round_c, cand, unroll=True)
        taken = jnp.where(cand >= best_ref[k - 1:k, :], 1.0, 0.0)
        counts = _staircase_counts(taken, spans)
        lim = jnp.zeros_like(s1)
        for a in range(k):
            lim = jnp.where(s1 == sv1[a:a + 1], counts[a], lim)
        emit(rank2, lim)

        n1 = jnp.sum(jnp.where(s1 >= sv1[k - 1:k], 1.0, 0.0), axis=0, keepdims=True)
        n2 = jnp.sum(jnp.where(rank2 < kf, 1.0, 0.0), axis=0, keepdims=True)
        nc = jnp.sum(taken, axis=0, keepdims=True)
        tied = jnp.logical_or(jnp.max(jnp.maximum(jnp.maximum(n1, n2), nc)) > kf,
                              jnp.min(s2) <= RANK_MARK)

        @pl.when(tied)
        def _():
            tiles = [slice(lt * LANES, (lt + 1) * LANES) for lt in range(n_tiles)]
            rank1 = jnp.concatenate([_extract_ranked(s1[:, ls], sv1_tiles.at[lt], k)
                                     for lt, ls in enumerate(tiles)], axis=1)
            rank2 = jnp.concatenate([_extract_ranked(s2[:, ls], sv2_tiles.at[lt], k)
                                     for lt, ls in enumerate(tiles)], axis=1)
            cand, spans = _candidates(sorted1(), sorted2())
            taken = jnp.where(_extract_ranked(cand, best_ref, k) < kf, 1.0, 0.0)
            counts = _staircase_counts(taken, spans)
            lim = jnp.zeros_like(s1)
            for a in range(k):
                lim = jnp.where(rank1 == float(a), counts[a], lim)
            emit(rank2, lim)

        return carry

    lax.fori_loop(0, PEER_HEADS, head, 0)


def peer_select(h, g, shift, scale, wq_t, keys, *, layer, seq, rows, tm=512):
    t, d = h.shape
    nq = wq_t.shape[1]
    per_batch = seq // tm
    tab_spec = pl.BlockSpec((PEER_HEADS, N_SUBKEYS, tm), lambda i: (0, 0, i))
    tab = lambda dt: jax.ShapeDtypeStruct((PEER_HEADS, N_SUBKEYS, t), dt)
    grp_spec = pl.BlockSpec((PEER_HEADS, N_SUBKEYS // rows, rows, tm), lambda i: (0, 0, 0, i))
    grp = jax.ShapeDtypeStruct((PEER_HEADS, N_SUBKEYS // rows, rows, t), F32)
    return pl.pallas_call(
        _peer_select_kernel,
        grid=(t // tm,),
        in_specs=[
            pl.BlockSpec((tm, d), lambda i: (i, 0)),
            pl.BlockSpec((1, d), lambda i: (0, 0)),
            pl.BlockSpec((1, 1, d), lambda i: (i // per_batch, 0, 0)),
            pl.BlockSpec((1, 1, d), lambda i: (i // per_batch, 0, 0)),
            pl.BlockSpec((None, nq, d), lambda i: (layer, 0, 0)),
            pl.BlockSpec((None,) + keys.shape[1:], lambda i: (layer, 0, 0, 0, 0)),
        ],
        out_specs=[pl.BlockSpec((tm, d), lambda i: (i, 0)), tab_spec, tab_spec, grp_spec, grp_spec],
        out_shape=[jax.ShapeDtypeStruct((t, d), BF16), tab(BF16), tab(BF16), grp, grp],
        scratch_shapes=[pltpu.VMEM((nq, tm), BF16),
                        pltpu.VMEM((tm // LANES, PEER_TOPK, LANES), F32),
                        pltpu.VMEM((tm // LANES, PEER_TOPK, LANES), F32),
                        pltpu.VMEM((PEER_TOPK, tm), F32)],
        compiler_params=_params("arbitrary"),
        name="peer_select",
    )(h, g, shift, scale, wq_t, keys)


def _peer_mix_kernel(xf_ref, u_ref, vt_ref, r2_ref, a2_ref, lim_ref, a1_ref, h_ref, g_ref, *rest,
                     n_blocks, convert):
    if convert:
        cu_ref, cv_ref, o_ref, ou_ref, ovt_ref, acc_ref, hid_a_ref, hid_b_ref, p_ref = rest
    else:
        o_ref, acc_ref, hid_a_ref, hid_b_ref, p_ref = rest
    g = pl.program_id(0)
    eb, tc = hid_a_ref.shape
    rows_per_step = eb // N_SUBKEYS
    mix_block = jnp.maximum(g - 1, 0) % n_blocks

    @pl.when(mix_block == 0)
    def _():
        acc_ref[...] = jnp.zeros_like(acc_ref)

    @pl.when(g == 0)
    def _():
        hid_b_ref[...] = jnp.zeros_like(hid_b_ref)

    def step(cur_ref, prev_ref):
        if convert:
            _convert_expert_slice(cu_ref, cv_ref, ou_ref, ovt_ref)
        cur_ref[...] = lax.dot_general(u_ref[...], xf_ref[...], NT_DIMS, preferred_element_type=F32)
        for kk in range(rows_per_step):
            gate = None
            for h in range(PEER_HEADS):
                lim = jnp.broadcast_to(lim_ref[h, 0, kk:kk + 1, :], (BF16_ROWS, tc)).astype(BF16)
                a1 = jnp.broadcast_to(a1_ref[h, 0, kk:kk + 1, :], (BF16_ROWS, tc)).astype(BF16)
                lim = jnp.concatenate([lim] * (N_SUBKEYS // BF16_ROWS), axis=0)
                a1 = jnp.concatenate([a1] * (N_SUBKEYS // BF16_ROWS), axis=0)
                w = jnp.where(r2_ref[h] < lim, a2_ref[h] * a1, jnp.zeros((), BF16))
                gate = w if gate is None else gate + w
            sl = slice(kk * N_SUBKEYS, (kk + 1) * N_SUBKEYS)
            hid = prev_ref[sl, :]
            act = 0.5 * hid * (1.0 + lax.erf(hid * INV_SQRT2))
            p_ref[sl, :] = act.astype(BF16) * gate
        acc_ref[...] += jnp.dot(vt_ref[...], p_ref[...], preferred_element_type=F32)

    @pl.when(g % 2 == 0)
    def _():
        step(hid_a_ref, hid_b_ref)

    @pl.when(g % 2 == 1)
    def _():
        step(hid_b_ref, hid_a_ref)

    @pl.when(jnp.logical_and(mix_block == n_blocks - 1, g > 0))
    def _():
        o_ref[...] = h_ref[...] + g_ref[0] * acc_ref[...].T


def peer_mix(xf, u, vt, r2, a2, lim, a1, h, gate, *, seq, next_tables=None, tc=512):
    t, d = h.shape
    rows = lim.shape[2]
    eb = rows * N_SUBKEYS
    n_blocks = u.shape[0] // eb
    n_items = (t // tc) * n_blocks
    per_batch = seq // tc

    def hid_item(g):
        return jnp.minimum(g, n_items - 1)

    def mix_item(g):
        return jnp.maximum(g - 1, 0)

    tab_spec = pl.BlockSpec((PEER_HEADS, N_SUBKEYS, tc), lambda g: (0, 0, mix_item(g) // n_blocks))
    row_spec = pl.BlockSpec((PEER_HEADS, 1, rows, tc),
                            lambda g: (0, mix_item(g) % n_blocks, 0, mix_item(g) // n_blocks))
    chunk_spec = pl.BlockSpec((tc, d), lambda g: (mix_item(g) // n_blocks, 0))
    cv_in, cv_out, cv_shape, cv_args = [], [], [], []
    if next_tables is not None:
        peer_u, peer_v, layer = next_tables
        assert n_items + 1 >= peer_u.shape[1] // CONVERT_ROWS
        cv_in, cv_out, cv_shape = _convert_specs(peer_u, peer_v, layer, lambda g: g)
        cv_args = [peer_u, peer_v]
    out = pl.pallas_call(
        functools.partial(_peer_mix_kernel, n_blocks=n_blocks, convert=bool(cv_args)),
        grid=(n_items + 1,),
        in_specs=[
            pl.BlockSpec((tc, d), lambda g: (hid_item(g) // n_blocks, 0)),
            pl.BlockSpec((eb, d), lambda g: (hid_item(g) % n_blocks, 0)),
            pl.BlockSpec((d, eb), lambda g: (0, mix_item(g) % n_blocks)),
            tab_spec, tab_spec, row_spec, row_spec,
            chunk_spec,
            pl.BlockSpec((1, 1, d), lambda g: (mix_item(g) // n_blocks // per_batch, 0, 0)),
        ] + cv_in,
        out_specs=[chunk_spec] + cv_out,
        out_shape=[jax.ShapeDtypeStruct((t, d), F32)] + cv_shape,
        scratch_shapes=[pltpu.VMEM((d, tc), F32),
                        pltpu.VMEM((eb, tc), F32),
                        pltpu.VMEM((eb, tc), F32),
                        pltpu.VMEM((eb, tc), BF16)],
        compiler_params=_params("arbitrary"),
        name="peer_mix",
    )(xf, u, vt, r2, a2, lim, a1, h, gate, *cv_args)
    return out if cv_args else out[0]


def kernel(x, c, rel_bias, norm1_g, norm2_g, w_ada, b_ada, a_w_in, a_q_gain, a_k_gain,
           b_w_in, w_out, peer_w_query, peer_sub_keys, peer_u, peer_v):
    b, s, d = x.shape
    depth = w_ada.shape[0]
    n_groups = len(DIL_GROUPS)
    heads = d // HEAD_DIM

    c_pad = jnp.pad(c, ((0, ADALN_ROWS - b), (0, 0)))
    mod = adaln(c_pad, w_ada, b_ada, layers=depth)
    a_w = a_w_in.astype(BF16)
    b_w = w_o = None
    wq_t = jnp.swapaxes(peer_w_query, 1, 2).astype(BF16)
    keys = peer_sub_keys.astype(BF16)
    u = vt = None
    h = x.reshape(b * s, d)
    for i in range(depth):
        sh1, sc1, g1, sh2, sc2, g2 = (mod[i, :b, k * d:(k + 1) * d][:, None, :] for k in range(6))
        j = i // 2
        if i % 2 == 0:
            ones = jnp.ones((HEAD_DIM,), F32)
            gain = jnp.stack([jnp.tile(row, heads) for g in range(n_groups)
                              for row in (a_q_gain[j, g] / math.sqrt(HEAD_DIM), a_k_gain[j, g], ones)])
            assert i == 0
            qkvs, u, vt, b_w, w_o = in_proj_dilated(
                h, norm1_g[i][None], sh1, sc1, a_w, gain[:, None, :], peer_u, peer_v,
                (b_w_in.reshape(-1, b_w_in.shape[2]), w_out.reshape(-1, d)), layer=j, peer_layer=i, seq=s)
            b_w, w_o = b_w.reshape(b_w_in.shape), w_o.reshape(w_out.shape)
            outs, lses = [], []
            for g, (window, dil) in enumerate(DIL_GROUPS):
                bias = _band_bias(rel_bias[:, g * heads:(g + 1) * heads], window, dil)
                o, lse = dilated_group_attention(qkvs[g], bias, n_heads=heads)
                outs.append(o)
                lses.append(lse)
            h = out_proj_merge(outs, lses, w_o, h, g1, layer=i, seq=s)
        else:
            qkv = in_proj(h, norm1_g[i][None], sh1, sc1, b_w, layer=j, seq=s)
            y = stick_breaking_attention(qkv.reshape(b, s, -1), n_heads=heads)
            h = out_proj(y.reshape(b * s, d), w_o, h, g1, layer=i, seq=s)
        xf, r2, a2, lim, a1 = peer_select(h, norm2_g[i][None], sh2, sc2, wq_t, keys, layer=i, seq=s,
                                          rows=PEER_BLOCK_ROWS)
        if i + 1 < depth:
            h, u, vt = peer_mix(xf, u, vt, r2, a2, lim, a1, h, g2, seq=s,
                                next_tables=(peer_u, peer_v, i + 1))
        else:
            h = peer_mix(xf, u, vt, r2, a2, lim, a1, h, g2, seq=s)
    return h.reshape(b, s, d)
```

```python
import functools
import math

import jax
import jax.numpy as jnp
from jax import lax
from jax.experimental import pallas as pl
from jax.experimental.pallas import tpu as pltpu

F32 = jnp.float32
BF16 = jnp.bfloat16

LANES = 128
BF16_ROWS = 16
HEAD_DIM = 128
BLOCK = 128
DIL_GROUPS = ((128, 1), (512, 4), (2048, 16))
REL_BUCKETS = 32
REL_MAX_DIST = 2048
PEER_HEADS = 8
N_SUBKEYS = 128
PEER_TOPK = 16
PEER_BLOCK_ROWS = 8
EPS = 1e-6
NEG_BIG = -1e30
SB_DEAD_DECAY = 104.0
SB_HEADS_PER_STEP = 8
SB_Q_CHUNK = 2048
INV_SQRT2 = 0.7071067811865476

VMEM_LIMIT = 60 * 1024 * 1024

NT_DIMS = (((1,), (1,)), ((), ()))


def _params(*sem):
    return pltpu.CompilerParams(dimension_semantics=sem, vmem_limit_bytes=VMEM_LIMIT)


def _adaln_kernel(c_ref, w_lo_ref, w_hi_ref, b_ref, o_ref):
    c = c_ref[...]
    cs = c / (1.0 + jnp.exp(-c))
    half = w_lo_ref.shape[1]
    dot = functools.partial(jnp.dot, preferred_element_type=F32, precision=lax.Precision.HIGHEST)
    o_ref[0] = dot(cs[:, :half], w_lo_ref[0]) + dot(cs[:, half:], w_hi_ref[0]) + b_ref[0]


ADALN_ROWS = 8


def adaln(c_pad, w_ada, b_ada, *, layers, tn=1024):
    depth, d, n = w_ada.shape
    rows = c_pad.shape[0]
    return pl.pallas_call(
        _adaln_kernel,
        grid=(layers, n // tn),
        in_specs=[
            pl.BlockSpec((rows, d), lambda i, j: (0, 0)),
            pl.BlockSpec((1, d // 2, tn), lambda i, j: (i, 0, j)),
            pl.BlockSpec((1, d // 2, tn), lambda i, j: (i, 1, j)),
            pl.BlockSpec((1, 1, tn), lambda i, j: (i, 0, j)),
        ],
        out_specs=pl.BlockSpec((1, rows, tn), lambda i, j: (i, 0, j)),
        out_shape=jax.ShapeDtypeStruct((layers, rows, n), F32),
        compiler_params=_params("arbitrary", "arbitrary"),
        name="adaln",
    )(c_pad, w_ada, w_ada, b_ada.reshape(depth, 1, n))


CONVERT_ROWS = 128


def _convert_expert_slice(cu_ref, cv_ref, ou_ref, ovt_ref):
    ou_ref[...] = cu_ref[...].astype(BF16)
    ovt_ref[...] = cv_ref[...].T.astype(BF16)


def _convert_specs(peer_u, peer_v, layer, step_of, rows=CONVERT_ROWS):
    n_exp, d = peer_u.shape[1:]
    last = n_exp // rows - 1

    def slice_of(*idx):
        return jnp.minimum(step_of(*idx), last)

    src = pl.BlockSpec((None, rows, d), lambda *idx: (layer, slice_of(*idx), 0))
    in_specs = [src, src]
    out_specs = [pl.BlockSpec((rows, d), lambda *idx: (slice_of(*idx), 0)),
                 pl.BlockSpec((d, rows), lambda *idx: (0, slice_of(*idx)))]
    out_shape = [jax.ShapeDtypeStruct((n_exp, d), BF16), jax.ShapeDtypeStruct((d, n_exp), BF16)]
    return in_specs, out_specs, out_shape


def _norm_modulate(x, g, shift, scale):
    ms = jnp.mean(x * x, axis=-1, keepdims=True)
    y = x * lax.rsqrt(ms + EPS) * g
    return y * (1.0 + scale) + shift


def _in_proj_kernel(h_ref, g_ref, sh_ref, sc_ref, w_ref, o_ref, xm_ref):
    @pl.when(pl.program_id(1) == 0)
    def _():
        xm_ref[...] = _norm_modulate(h_ref[...], g_ref[...], sh_ref[0], sc_ref[0]).astype(BF16)

    o_ref[...] = jnp.dot(xm_ref[...], w_ref[...], preferred_element_type=F32).astype(o_ref.dtype)


def in_proj(h, g, shift, scale, w, *, layer, seq, tm=1024, tn=2048):
    t, d = h.shape
    n = w.shape[2]
    per_batch = seq // tm
    return pl.pallas_call(
        _in_proj_kernel,
        grid=(t // tm, n // tn),
        in_specs=[
            pl.BlockSpec((tm, d), lambda i, j: (i, 0)),
            pl.BlockSpec((1, d), lambda i, j: (0, 0)),
            pl.BlockSpec((1, 1, d), lambda i, j: (i // per_batch, 0, 0)),
            pl.BlockSpec((1, 1, d), lambda i, j: (i // per_batch, 0, 0)),
            pl.BlockSpec((None, d, tn), lambda i, j: (layer, 0, j)),
        ],
        out_specs=pl.BlockSpec((tm, tn), lambda i, j: (i, j)),
        out_shape=jax.ShapeDtypeStruct((t, n), BF16),
        scratch_shapes=[pltpu.VMEM((tm, d), BF16)],
        compiler_params=_params("arbitrary", "arbitrary"),
        name="in_proj",
    )(h, g, shift, scale, w)


def _in_proj_dilated_kernel(h_ref, g_ref, sh_ref, sc_ref, w_ref, gain_ref,
                            cu_ref, cv_ref, cb_ref, co_ref,
                            o0_ref, o1_ref, o2_ref, ou_ref, ovt_ref, ob_ref, oo_ref,
                            xm0_ref, xm1_ref, xm2_ref, slab_ref, *, parts):
    j = pl.program_id(1)
    seg = j // parts
    tm, d = xm0_ref.shape
    xm_refs = (xm0_ref, xm1_ref, xm2_ref)

    def side_jobs():
        _convert_expert_slice(cu_ref, cv_ref, ou_ref, ovt_ref)
        ob_ref[...] = cb_ref[...].astype(BF16)
        oo_ref[...] = co_ref[...].astype(BF16)

    @pl.when(j == 0)
    def _():
        xm = _norm_modulate(h_ref[...], g_ref[...], sh_ref[0], sc_ref[0])
        for sb in range(d // HEAD_DIM):
            slab_ref[sb] = xm[:, sb * HEAD_DIM:(sb + 1) * HEAD_DIM]
        for (_, dil), xm_ref in zip(DIL_GROUPS, xm_refs):
            if dil == 1:
                xm_ref[...] = xm.astype(BF16)
                continue
            rows = tm // dil
            for sb in range(d // HEAD_DIM):
                for r in range(dil):
                    xm_ref[r * rows:(r + 1) * rows, sb * HEAD_DIM:(sb + 1) * HEAD_DIM] = (
                        slab_ref[sb, pl.ds(r, rows, stride=dil), :].astype(BF16))

    def project(xm_ref, o_ref, dil):
        rows = tm // dil
        slab = min(2 * HEAD_DIM, w_ref.shape[1])
        n_slabs = w_ref.shape[1] // slab

        @pl.when(seg % 3 == 2)
        def _():
            side_jobs()
            for sb in range(n_slabs):
                cols = slice(sb * slab, (sb + 1) * slab)
                acc = jnp.dot(xm_ref[...], w_ref[:, cols], preferred_element_type=F32).astype(BF16)
                for r in range(dil):
                    o_ref[0, r, :, cols] = acc[r * rows:(r + 1) * rows]

        @pl.when(seg % 3 != 2)
        def _():
            side_jobs()
            for sb in range(n_slabs):
                acc = jnp.dot(xm_ref[...], w_ref[:, sb * slab:(sb + 1) * slab], preferred_element_type=F32)
                for hh in range(slab // HEAD_DIM):
                    sl = slice(sb * slab + hh * HEAD_DIM, sb * slab + (hh + 1) * HEAD_DIM)
                    a = acc[:, hh * HEAD_DIM:(hh + 1) * HEAD_DIM]
                    ms = jnp.mean(a * a, axis=-1, keepdims=True)
                    y = (a * lax.rsqrt(ms + EPS) * gain_ref[0, :, sl]).astype(BF16)
                    for r in range(dil):
                        o_ref[0, r, :, sl] = y[r * rows:(r + 1) * rows]

    for grp, ((_, dil), xm_ref, o_ref) in enumerate(zip(DIL_GROUPS, xm_refs, (o0_ref, o1_ref, o2_ref))):
        @pl.when(seg // 3 == grp)
        def _(dil=dil, xm_ref=xm_ref, o_ref=o_ref):
            project(xm_ref, o_ref, dil)


def in_proj_dilated(h, g, shift, scale, w, gain, peer_u, peer_v, casts, *,
                    layer, peer_layer, seq, tm=512, tn=2048):
    t, d = h.shape
    n = w.shape[2]
    b = t // seq
    per_batch = seq // tm
    n_groups = len(DIL_GROUPS)
    width = gain.shape[2]
    parts = width // tn
    assert n == n_groups * 3 * width

    def out_spec(grp, dil):
        def imap(i, j):
            return (i // per_batch, 0, i % per_batch, jnp.clip(j - 3 * parts * grp, 0, 3 * parts - 1))
        return pl.BlockSpec((1, dil, tm // dil, tn), imap)

    n_cols = n // tn
    n_steps = (t // tm) * n_cols
    step = lambda i, j: i * n_cols + j
    conv_rows = CONVERT_ROWS
    n_slices = peer_u.shape[1] // conv_rows
    assert n_steps >= n_slices
    cv_in, cv_out, cv_shape = _convert_specs(peer_u, peer_v, peer_layer, step, conv_rows)
    side_in, side_out, side_shape = [], [], []
    for a in casts:
        rows = a.shape[0] // n_slices
        last = a.shape[0] // rows - 1
        spec = pl.BlockSpec((rows, a.shape[1]), lambda i, j, last=last: (jnp.minimum(step(i, j), last), 0))
        side_in.append(spec)
        side_out.append(spec)
        side_shape.append(jax.ShapeDtypeStruct(a.shape, BF16))
    *qkvs, u_bf, vt_bf, cast0, cast1 = pl.pallas_call(
        functools.partial(_in_proj_dilated_kernel, parts=parts),
        grid=(t // tm, n // tn),
        in_specs=[
            pl.BlockSpec((tm, d), lambda i, j: (i, 0)),
            pl.BlockSpec((1, d), lambda i, j: (0, 0)),
            pl.BlockSpec((1, 1, d), lambda i, j: (i // per_batch, 0, 0)),
            pl.BlockSpec((1, 1, d), lambda i, j: (i // per_batch, 0, 0)),
            pl.BlockSpec((None, d, tn), lambda i, j: (layer, 0, j)),
            pl.BlockSpec((1, 1, tn), lambda i, j: (j // parts, 0, j % parts)),
        ] + cv_in + side_in,
        out_specs=[out_spec(grp, dil) for grp, (_, dil) in enumerate(DIL_GROUPS)] + cv_out + side_out,
        out_shape=[jax.ShapeDtypeStruct((b, dil, seq // dil, 3 * width), BF16)
                   for _, dil in DIL_GROUPS] + cv_shape + side_shape,
        scratch_shapes=[pltpu.VMEM((tm, d), BF16)] * 3 + [pltpu.VMEM((d // HEAD_DIM, tm, HEAD_DIM), F32)],
        compiler_params=_params("arbitrary", "arbitrary"),
        name="in_proj_dilated",
    )(h, g, shift, scale, w, gain, peer_u, peer_v, *casts)
    return qkvs, u_bf, vt_bf, cast0, cast1


def _dil_attn_kernel(q_ref, kp_ref, kc_ref, vp_ref, vc_ref, bias_ref, o_ref, lse_ref, *, n_heads):
    n = pl.program_id(2)
    n_sub = q_ref.shape[2] // BLOCK
    lane = lax.broadcasted_iota(jnp.int32, (1, 2 * BLOCK), 1)
    pen = jnp.where(jnp.logical_and(lane < BLOCK, n == 0), NEG_BIG, 0.0).astype(F32)
    ones = jnp.ones((2 * BLOCK, HEAD_DIM), BF16)
    lse_lanes = BLOCK // n_heads
    lane_head = lax.broadcasted_iota(jnp.int32, (1, BLOCK), 1) // lse_lanes
    for sb in range(n_sub):
        rows = slice(sb * BLOCK, (sb + 1) * BLOCK)
        prev = slice((sb - 1) * BLOCK, sb * BLOCK)
        lse_pack = jnp.zeros((BLOCK, BLOCK), F32)
        for hh in range(n_heads):
            sl = slice(hh * HEAD_DIM, (hh + 1) * HEAD_DIM)
            q = q_ref[0, 0, rows, sl]
            k_prev = kp_ref[0, 0, :, sl] if sb == 0 else kc_ref[0, 0, prev, sl]
            v_prev = vp_ref[0, 0, :, sl] if sb == 0 else vc_ref[0, 0, prev, sl]
            k = jnp.concatenate([k_prev, kc_ref[0, 0, rows, sl]], axis=0)
            v = jnp.concatenate([v_prev, vc_ref[0, 0, rows, sl]], axis=0)
            logits = lax.dot_general(q, k, NT_DIMS, preferred_element_type=F32)
            logits = logits + bias_ref[hh]
            if sb == 0:
                logits = logits + pen
            m = jnp.max(logits, axis=-1, keepdims=True)
            p = jnp.exp(logits - m).astype(BF16)
            v_ext = jnp.concatenate([v, ones], axis=1)
            pv = jnp.dot(p, v_ext, preferred_element_type=F32)
            denom = pv[:, HEAD_DIM:]
            o_ref[0, 0, rows, sl] = (pv[:, :HEAD_DIM] / denom).astype(o_ref.dtype)
            lse_pack = jnp.where(lane_head == hh, m + jnp.log(denom), lse_pack)
        lse_ref[0, 0, rows, :] = lse_pack


def dilated_group_attention(qkv, bias, *, n_heads, q_blocks=4):
    b, dil, sub_len, c = qkv.shape
    assert BLOCK % n_heads == 0
    width = n_heads * HEAD_DIM
    q_blocks = min(q_blocks, sub_len // BLOCK)
    rows = q_blocks * BLOCK
    steps = sub_len // rows

    def cur(which):
        return pl.BlockSpec((1, 1, rows, width), lambda bi, r, n: (bi, r, n, which))

    def prev(which):
        return pl.BlockSpec((1, 1, BLOCK, width),
                            lambda bi, r, n: (bi, r, jnp.maximum(q_blocks * n - 1, 0), which))

    return pl.pallas_call(
        functools.partial(_dil_attn_kernel, n_heads=n_heads),
        grid=(b, dil, steps),
        in_specs=[cur(0), prev(1), cur(1), prev(2), cur(2),
                  pl.BlockSpec((n_heads, BLOCK, 2 * BLOCK), lambda bi, r, n: (0, 0, 0))],
        out_specs=[pl.BlockSpec((1, 1, rows, width), lambda bi, r, n: (bi, r, n, 0)),
                   pl.BlockSpec((1, 1, rows, BLOCK), lambda bi, r, n: (bi, r, n, 0))],
        out_shape=[jax.ShapeDtypeStruct((b, dil, sub_len, width), BF16),
                   jax.ShapeDtypeStruct((b, dil, sub_len, BLOCK), F32)],
        compiler_params=_params("arbitrary", "arbitrary", "arbitrary"),
        name=f"dilated_attn_d{dil}",
    )(qkv, qkv, qkv, qkv, qkv, bias)


def _rel_bucket(dist):
    exact = REL_BUCKETS // 2
    d_f = jnp.maximum(dist, exact).astype(F32)
    log_b = exact + (jnp.log(d_f / exact) / math.log(REL_MAX_DIST / exact)
                     * (REL_BUCKETS - exact)).astype(jnp.int32)
    return jnp.where(dist < exact, dist, jnp.minimum(log_b, REL_BUCKETS - 1))


def _band_bias(table_g, window, dil):
    w_steps = window // dil
    period = 3 * BLOCK
    per_delta = table_g[_rel_bucket(jnp.arange(w_steps + 1) * dil)].astype(F32)
    row = jnp.full((period, table_g.shape[1]), NEG_BIG, F32)
    row = row.at[BLOCK - w_steps:BLOCK + 1].set(per_delta[::-1])
    flat = jnp.tile(row.T, (1, BLOCK))[:, :BLOCK * (period - 1)]
    return flat.reshape(-1, BLOCK, period - 1)[:, :, :2 * BLOCK]


def _out_proj_merge_kernel(o0_ref, o1_ref, o2_ref, l0_ref, l1_ref, l2_ref, w_ref, h_ref, g_ref, out_ref,
                           os1_ref, os2_ref, ls1_ref, ls2_ref, ya_ref, yb_ref):
    g = pl.program_id(0)
    tm = out_ref.shape[0]
    n_heads = ya_ref.shape[1] // HEAD_DIM

    @pl.when(g == 0)
    def _():
        yb_ref[...] = jnp.zeros_like(yb_ref)

    def to_token_order(src_ref, dst_ref, sl):
        dil = src_ref.shape[1]
        for r in range(dil):
            dst_ref[pl.ds(r, tm // dil, stride=dil), :] = src_ref[0, r, :, sl].astype(F32)

    def step(y_ref, prev_y_ref):
        to_token_order(l1_ref, ls1_ref, slice(None))
        to_token_order(l2_ref, ls2_ref, slice(None))
        l0, l1, l2 = l0_ref[0, 0], ls1_ref[...], ls2_ref[...]
        m = jnp.maximum(jnp.maximum(l0, l1), l2)
        e0, e1, e2 = jnp.exp(l0 - m), jnp.exp(l1 - m), jnp.exp(l2 - m)
        inv = 1.0 / (e0 + e1 + e2)
        a0, a1, a2 = e0 * inv, e1 * inv, e2 * inv
        for hh in range(n_heads):
            sl = slice(hh * HEAD_DIM, (hh + 1) * HEAD_DIM)
            col = slice(hh * (BLOCK // n_heads), hh * (BLOCK // n_heads) + 1)
            to_token_order(o1_ref, os1_ref, sl)
            to_token_order(o2_ref, os2_ref, sl)
            y = (a0[:, col] * o0_ref[0, 0, :, sl].astype(F32) + a1[:, col] * os1_ref[...]
                 + a2[:, col] * os2_ref[...])
            y_ref[:, sl] = y.astype(BF16)
        proj = jnp.dot(prev_y_ref[...], w_ref[...], preferred_element_type=F32)
        out_ref[...] = h_ref[...] + g_ref[0] * proj

    @pl.when(g % 2 == 0)
    def _():
        step(ya_ref, yb_ref)

    @pl.when(g % 2 == 1)
    def _():
        step(yb_ref, ya_ref)


def out_proj_merge(outs, lses, w, h, gate, *, layer, seq, tm=512):
    t, d = h.shape
    width = w.shape[1]
    per_batch = seq // tm
    n_steps = t // tm

    def merged(g):
        return jnp.minimum(g, n_steps - 1)

    def projected(g):
        return jnp.maximum(g - 1, 0)

    row = pl.BlockSpec((tm, d), lambda g: (projected(g), 0))

    def grp_spec(a):
        dil, cols = a.shape[1], a.shape[3]
        return pl.BlockSpec((1, dil, tm // dil, cols),
                            lambda g: (merged(g) // per_batch, 0, merged(g) % per_batch, 0))

    return pl.pallas_call(
        _out_proj_merge_kernel,
        grid=(n_steps + 1,),
        in_specs=[grp_spec(a) for a in (*outs, *lses)] + [
            pl.BlockSpec((None, width, d), lambda g: (layer, 0, 0)),
            row,
            pl.BlockSpec((1, 1, d), lambda g: (projected(g) // per_batch, 0, 0)),
        ],
        out_specs=row,
        out_shape=jax.ShapeDtypeStruct((t, d), F32),
        scratch_shapes=[pltpu.VMEM((tm, HEAD_DIM), F32)] * 4 + [pltpu.VMEM((tm, width), BF16)] * 2,
        compiler_params=_params("arbitrary"),
        name="out_proj_merge",
    )(*outs, *lses, w, h, gate)


def _out_proj_kernel(y_ref, w_ref, h_ref, g_ref, out_ref):
    proj = jnp.dot(y_ref[...], w_ref[...], preferred_element_type=F32)
    out_ref[...] = h_ref[...] + g_ref[0] * proj


def out_proj(y, w, h, gate, *, layer, seq, tm=512):
    t, d = h.shape
    width = w.shape[1]
    per_batch = seq // tm
    row = lambda cols: pl.BlockSpec((tm, cols), lambda i: (i, 0))
    return pl.pallas_call(
        _out_proj_kernel,
        grid=(t // tm,),
        in_specs=[row(width), pl.BlockSpec((None, width, d), lambda i: (layer, 0, 0)), row(d),
                  pl.BlockSpec((1, 1, d), lambda i: (i // per_batch, 0, 0))],
        out_specs=row(d),
        out_shape=jax.ShapeDtypeStruct((t, d), F32),
        compiler_params=_params("arbitrary"),
        name="out_proj",
    )(y, w, h, gate)


def _split2(x):
    hi = x.astype(BF16)
    lo = (x - hi.astype(F32)).astype(BF16)
    return hi, lo


def _sb_kernel(q_ref, k_ref, v_ref, o_ref, acc_ref, decay_ref, *, scale):
    nq = q_ref.shape[1] // BLOCK
    q_base = pl.program_id(2) * q_ref.shape[1]
    n_heads = acc_ref.shape[0]
    row = lax.broadcasted_iota(jnp.int32, (BLOCK, BLOCK), 0)
    col = lax.broadcasted_iota(jnp.int32, (BLOCK, BLOCK), 1)
    before = col < row

    def tri_ext(n_keys):
        r = lax.broadcasted_iota(jnp.int32, (n_keys, n_keys), 0)
        c = lax.broadcasted_iota(jnp.int32, (n_keys, n_keys), 1)
        tri = jnp.where(r > c, 1.0, 0.0).astype(BF16)
        return jnp.concatenate([tri, jnp.ones((n_keys, BLOCK), BF16)], axis=1)

    tri1, tri2 = tri_ext(BLOCK), tri_ext(2 * BLOCK)

    def tails_and_sums(sps, tri):
        n_keys = tri.shape[0]
        parts = [t for sp in sps for t in _split2(sp)]
        r = jnp.dot(jnp.concatenate(parts, axis=0), tri, preferred_element_type=F32)
        out = []
        for g in range(len(sps)):
            hi = r[(2 * g) * BLOCK:(2 * g + 1) * BLOCK]
            lo = r[(2 * g + 1) * BLOCK:(2 * g + 2) * BLOCK]
            both = hi + lo
            out.append((both[:, :n_keys], both[:, n_keys:]))
        return out

    def softplus(z):
        return jnp.maximum(z, 0.0) + jnp.log1p(jnp.exp(-jnp.abs(z)))

    def logits(q0, k0, g, n_keys=BLOCK):
        cs = slice(g * HEAD_DIM, (g + 1) * HEAD_DIM)
        return lax.dot_general(q_ref[0, pl.ds(q0, BLOCK), cs], k_ref[0, pl.ds(k0, n_keys), cs],
                               NT_DIMS, preferred_element_type=F32) * scale

    def values(k0, g, n_keys=BLOCK):
        return v_ref[0, pl.ds(k0, n_keys), g * HEAD_DIM:(g + 1) * HEAD_DIM]

    def qblock(i, carry):
        q0 = pl.multiple_of(i * BLOCK, BLOCK)
        d0 = pl.multiple_of(q_base + q0, BLOCK)
        zs = [logits(q0, d0, g) for g in range(n_heads)]
        sps = [softplus(z) for z in zs]
        ts = tails_and_sums([jnp.where(before, sp, 0.0) for sp in sps], tri1)
        live = None
        for g in range(n_heads):
            tail, rsum = ts[g]
            a = jnp.where(before, jnp.exp(zs[g] - sps[g] - tail), 0.0)
            acc_ref[g] = jnp.dot(a.astype(BF16), values(d0, g), preferred_element_type=F32)
            decay_ref[g] = rsum
            lo = jnp.min(rsum)
            live = lo if live is None else jnp.minimum(live, lo)

        def walk(k0, tri):
            n_keys = tri.shape[0]
            zs = [logits(q0, k0, g, n_keys) for g in range(n_heads)]
            sps = [softplus(z) for z in zs]
            ts = tails_and_sums(sps, tri)
            live = None
            for g in range(n_heads):
                tail, rsum = ts[g]
                decay = decay_ref[g]
                later = jnp.concatenate([decay] * (n_keys // BLOCK), axis=1)
                a = jnp.exp(zs[g] - sps[g] - (later + tail))
                acc_ref[g] += jnp.dot(a.astype(BF16), values(k0, g, n_keys), preferred_element_type=F32)
                decay = decay + rsum
                decay_ref[g] = decay
                lo = jnp.min(decay)
                live = lo if live is None else jnp.minimum(live, lo)
            return live

        def cond(c):
            j, live = c
            return jnp.logical_and(j >= 1, live < SB_DEAD_DECAY)

        def body(c):
            j, _ = c
            return j - 2, walk(pl.multiple_of((j - 1) * BLOCK, BLOCK), tri2)

        j, live = lax.while_loop(cond, body, (d0 // BLOCK - 1, live))

        @pl.when(jnp.logical_and(j == 0, live < SB_DEAD_DECAY))
        def _():
            walk(0, tri1)

        for g in range(n_heads):
            o_ref[0, pl.ds(q0, BLOCK), g * HEAD_DIM:(g + 1) * HEAD_DIM] = acc_ref[g].astype(o_ref.dtype)
        return carry

    lax.fori_loop(0, nq, qblock, 0)


def stick_breaking_attention(qkv, *, n_heads):
    b, s, _ = qkv.shape
    hps = min(SB_HEADS_PER_STEP, n_heads)
    steps = n_heads // hps
    width = hps * HEAD_DIM
    qc = min(SB_Q_CHUNK, s)

    def spec(which):
        return pl.BlockSpec((1, s, width), lambda bi, h, c: (bi, 0, which * steps + h))

    return pl.pallas_call(
        functools.partial(_sb_kernel, scale=1.0 / math.sqrt(HEAD_DIM)),
        grid=(b, steps, s // qc),
        in_specs=[pl.BlockSpec((1, qc, width), lambda bi, h, c: (bi, c, h)), spec(1), spec(2)],
        out_specs=pl.BlockSpec((1, qc, width), lambda bi, h, c: (bi, c, h)),
        out_shape=jax.ShapeDtypeStruct((b, s, n_heads * HEAD_DIM), BF16),
        scratch_shapes=[pltpu.VMEM((hps, BLOCK, HEAD_DIM), F32),
                        pltpu.VMEM((hps, BLOCK, BLOCK), F32)],
        compiler_params=_params("arbitrary", "arbitrary", "arbitrary"),
        name="stick_breaking",
    )(qkv, qkv, qkv)


UNRANKED = 127.0
RANK_MARK = -(2.0 ** 100)


def _as_f32(r):
    return jnp.asarray(r).astype(F32)


def _extract_ranked(s, dst_ref, count):
    n = s.shape[0]
    idx = lax.broadcasted_iota(jnp.int32, s.shape, 0).astype(F32)

    def body(r, carry):
        s, rank = carry
        m = jnp.max(s, axis=0, keepdims=True)
        first = jnp.min(jnp.where(s == m, idx, float(n)), axis=0, keepdims=True)
        pick = idx == first
        dst_ref[pl.ds(r, 1), :] = m
        return jnp.where(pick, -jnp.inf, s), jnp.where(pick, _as_f32(r), rank)

    _, rank = lax.fori_loop(0, count, body, (s, jnp.full(s.shape, UNRANKED, F32)))
    return rank


def _candidates(sv1, sv2):
    k = PEER_TOPK
    groups = [sv1[0:1] + sv2]
    groups += [sv1[a:a + 1] + sv2[0:k // 2] for a in range(1, k // 2)]
    groups += [sv1[k // 2:] + sv2[0:1]]
    spans = [(0, k)] + [(k + (a - 1) * (k // 2), k // 2) for a in range(1, k // 2)]
    tail0 = k + (k // 2 - 1) * (k // 2)
    spans += [(tail0 + a, 1) for a in range(k // 2)]
    return jnp.concatenate(groups, axis=0), spans


def _staircase_counts(taken, spans):
    return [jnp.sum(taken[lo:lo + n], axis=0, keepdims=True) for lo, n in spans]


def _peer_select_kernel(h_ref, g_ref, sh_ref, sc_ref, wq_ref, keys_ref,
                        xf_ref, r2_ref, a2_ref, lim_ref, a1_ref,
                        qt_ref, sv1_tiles, sv2_tiles, best_ref):
    xf = _norm_modulate(h_ref[...], g_ref[...], sh_ref[0], sc_ref[0]).astype(BF16)
    xf_ref[...] = xf
    qt_ref[...] = lax.dot_general(wq_ref[...], xf, NT_DIMS, preferred_element_type=F32).astype(BF16)
    k = PEER_TOPK
    half = N_SUBKEYS
    kf = float(k)
    n_tiles = sv2_tiles.shape[0]

    def head(h, carry):
        r0 = pl.multiple_of(h * 2 * half, 2 * half)
        s1 = jnp.dot(keys_ref[h, 0], qt_ref[pl.ds(r0, half), :], preferred_element_type=F32)
        s2 = jnp.dot(keys_ref[h, 1], qt_ref[pl.ds(r0 + half, half), :], preferred_element_type=F32)

        def sorted1():
            return jnp.concatenate([sv1_tiles[lt] for lt in range(n_tiles)], axis=1)

        def sorted2():
            return jnp.concatenate([sv2_tiles[lt] for lt in range(n_tiles)], axis=1)

        def emit(rank2, lim):
            sv1, sv2, best = sorted1(), sorted2(), best_ref[...]
            z = jnp.sum(jnp.exp(best - best[0:1]), axis=0, keepdims=True)
            r2_ref[h] = rank2.astype(BF16)
            a2_ref[h] = jnp.exp(s2 - sv2[0:1]).astype(BF16)
            a1 = jnp.exp(s1 - sv1[0:1]) / z
            rows = lim_ref.shape[2]
            for grp in range(half // rows):
                lim_ref[h, grp] = lim[grp * rows:(grp + 1) * rows]
                a1_ref[h, grp] = a1[grp * rows:(grp + 1) * rows]

        marked = []
        for lt in range(n_tiles):
            def round12(r, c, lt=lt):
                w1, w2 = c
                m1 = jnp.max(w1, axis=0, keepdims=True)
                m2 = jnp.max(w2, axis=0, keepdims=True)
                sv1_tiles[lt, pl.ds(r, 1), :] = m1
                sv2_tiles[lt, pl.ds(r, 1), :] = m2
                mark = RANK_MARK * (1.0 + _as_f32(r) / kf)
                return jnp.where(w1 == m1, -jnp.inf, w1), jnp.where(w2 == m2, mark, w2)

            ls = slice(lt * LANES, (lt + 1) * LANES)
            marked.append(lax.fori_loop(0, k, round12, (s1[:, ls], s2[:, ls]), unroll=True)[1])
        w2 = jnp.concatenate(marked, axis=1)
        rank2 = jnp.where(w2 <= RANK_MARK, w2 * (kf / RANK_MARK) - kf, UNRANKED)

        sv1 = sorted1()
        cand, spans = _candidates(sv1, sorted2())

        def round_c(r, w):
            m = jnp.max(w, axis=0, keepdims=True)
            best_ref[pl.ds(r, 1), :] = m
            return jnp.where(w == m, -jnp.inf, w)

        lax.fori_loop(0, k, round_c, cand, unroll=True)
        taken = jnp.where(cand >= best_ref[k - 1:k, :], 1.0, 0.0)
        counts = _staircase_counts(taken, spans)
        lim = jnp.zeros_like(s1)
        for a in range(k):
            lim = jnp.where(s1 == sv1[a:a + 1], counts[a], lim)
        emit(rank2, lim)

        n1 = jnp.sum(jnp.where(s1 >= sv1[k - 1:k], 1.0, 0.0), axis=0, keepdims=True)
        n2 = jnp.sum(jnp.where(rank2 < kf, 1.0, 0.0), axis=0, keepdims=True)
        nc = jnp.sum(taken, axis=0, keepdims=True)
        tied = jnp.logical_or(jnp.max(jnp.maximum(jnp.maximum(n1, n2), nc)) > kf,
                              jnp.min(s2) <= RANK_MARK)

        @pl.when(tied)
        def _():
            tiles = [slice(lt * LANES, (lt + 1) * LANES) for lt in range(n_tiles)]
            rank1 = jnp.concatenate([_extract_ranked(s1[:, ls], sv1_tiles.at[lt], k)
                                     for lt, ls in enumerate(tiles)], axis=1)
            rank2 = jnp.concatenate([_extract_ranked(s2[:, ls], sv2_tiles.at[lt], k)
                                     for lt, ls in enumerate(tiles)], axis=1)
            cand, spans = _candidates(sorted1(), sorted2())
            taken = jnp.where(_extract_ranked(cand, best_ref, k) < kf, 1.0, 0.0)
            counts = _staircase_counts(taken, spans)
            lim = jnp.zeros_like(s1)
            for a in range(k):
                lim = jnp.where(rank1 == float(a), counts[a], lim)
            emit(rank2, lim)

        return carry

    lax.fori_loop(0, PEER_HEADS, head, 0)


def peer_select(h, g, shift, scale, wq_t, keys, *, layer, seq, rows, tm=512):
    t, d = h.shape
    nq = wq_t.shape[1]
    per_batch = seq // tm
    tab_spec = pl.BlockSpec((PEER_HEADS, N_SUBKEYS, tm), lambda i: (0, 0, i))
    tab = lambda dt: jax.ShapeDtypeStruct((PEER_HEADS, N_SUBKEYS, t), dt)
    grp_spec = pl.BlockSpec((PEER_HEADS, N_SUBKEYS // rows, rows, tm), lambda i: (0, 0, 0, i))
    grp = jax.ShapeDtypeStruct((PEER_HEADS, N_SUBKEYS // rows, rows, t), F32)
    return pl.pallas_call(
        _peer_select_kernel,
        grid=(t // tm,),
        in_specs=[
            pl.BlockSpec((tm, d), lambda i: (i, 0)),
            pl.BlockSpec((1, d), lambda i: (0, 0)),
            pl.BlockSpec((1, 1, d), lambda i: (i // per_batch, 0, 0)),
            pl.BlockSpec((1, 1, d), lambda i: (i // per_batch, 0, 0)),
            pl.BlockSpec((None, nq, d), lambda i: (layer, 0, 0)),
            pl.BlockSpec((None,) + keys.shape[1:], lambda i: (layer, 0, 0, 0, 0)),
        ],
        out_specs=[pl.BlockSpec((tm, d), lambda i: (i, 0)), tab_spec, tab_spec, grp_spec, grp_spec],
        out_shape=[jax.ShapeDtypeStruct((t, d), BF16), tab(BF16), tab(BF16), grp, grp],
        scratch_shapes=[pltpu.VMEM((nq, tm), BF16),
                        pltpu.VMEM((tm // LANES, PEER_TOPK, LANES), F32),
                        pltpu.VMEM((tm // LANES, PEER_TOPK, LANES), F32),
                        pltpu.VMEM((PEER_TOPK, tm), F32)],
        compiler_params=_params("arbitrary"),
        name="peer_select",
    )(h, g, shift, scale, wq_t, keys)


def _peer_mix_kernel(xf_ref, u_ref, vt_ref, r2_ref, a2_ref, lim_ref, a1_ref, h_ref, g_ref, *rest,
                     n_blocks, convert):
    if convert:
        cu_ref, cv_ref, o_ref, ou_ref, ovt_ref, acc_ref, hid_a_ref, hid_b_ref, p_ref = rest
    else:
        o_ref, acc_ref, hid_a_ref, hid_b_ref, p_ref = rest
    g = pl.program_id(0)
    eb, tc = hid_a_ref.shape
    rows_per_step = eb // N_SUBKEYS
    mix_block = jnp.maximum(g - 1, 0) % n_blocks

    @pl.when(mix_block == 0)
    def _():
        acc_ref[...] = jnp.zeros_like(acc_ref)

    @pl.when(g == 0)
    def _():
        hid_b_ref[...] = jnp.zeros_like(hid_b_ref)

    def step(cur_ref, prev_ref):
        if convert:
            _convert_expert_slice(cu_ref, cv_ref, ou_ref, ovt_ref)
        cur_ref[...] = lax.dot_general(u_ref[...], xf_ref[...], NT_DIMS, preferred_element_type=F32)
        for kk in range(rows_per_step):
            gate = None
            for h in range(PEER_HEADS):
                lim = jnp.broadcast_to(lim_ref[h, 0, kk:kk + 1, :], (BF16_ROWS, tc)).astype(BF16)
                a1 = jnp.broadcast_to(a1_ref[h, 0, kk:kk + 1, :], (BF16_ROWS, tc)).astype(BF16)
                lim = jnp.concatenate([lim] * (N_SUBKEYS // BF16_ROWS), axis=0)
                a1 = jnp.concatenate([a1] * (N_SUBKEYS // BF16_ROWS), axis=0)
                w = jnp.where(r2_ref[h] < lim, a2_ref[h] * a1, jnp.zeros((), BF16))
                gate = w if gate is None else gate + w
            sl = slice(kk * N_SUBKEYS, (kk + 1) * N_SUBKEYS)
            hid = prev_ref[sl, :]
            act = 0.5 * hid * (1.0 + lax.erf(hid * INV_SQRT2))
            p_ref[sl, :] = act.astype(BF16) * gate
        acc_ref[...] += jnp.dot(vt_ref[...], p_ref[...], preferred_element_type=F32)

    @pl.when(g % 2 == 0)
    def _():
        step(hid_a_ref, hid_b_ref)

    @pl.when(g % 2 == 1)
    def _():
        step(hid_b_ref, hid_a_ref)

    @pl.when(jnp.logical_and(mix_block == n_blocks - 1, g > 0))
    def _():
        o_ref[...] = h_ref[...] + g_ref[0] * acc_ref[...].T


def peer_mix(xf, u, vt, r2, a2, lim, a1, h, gate, *, seq, next_tables=None, tc=512):
    t, d = h.shape
    rows = lim.shape[2]
    eb = rows * N_SUBKEYS
    n_blocks = u.shape[0] // eb
    n_items = (t // tc) * n_blocks
    per_batch = seq // tc

    def hid_item(g):
        return jnp.minimum(g, n_items - 1)

    def mix_item(g):
        return jnp.maximum(g - 1, 0)

    tab_spec = pl.BlockSpec((PEER_HEADS, N_SUBKEYS, tc), lambda g: (0, 0, mix_item(g) // n_blocks))
    row_spec = pl.BlockSpec((PEER_HEADS, 1, rows, tc),
                            lambda g: (0, mix_item(g) % n_blocks, 0, mix_item(g) // n_blocks))
    chunk_spec = pl.BlockSpec((tc, d), lambda g: (mix_item(g) // n_blocks, 0))
    cv_in, cv_out, cv_shape, cv_args = [], [], [], []
    if next_tables is not None:
        peer_u, peer_v, layer = next_tables
        assert n_items + 1 >= peer_u.shape[1] // CONVERT_ROWS
        cv_in, cv_out, cv_shape = _convert_specs(peer_u, peer_v, layer, lambda g: g)
        cv_args = [peer_u, peer_v]
    out = pl.pallas_call(
        functools.partial(_peer_mix_kernel, n_blocks=n_blocks, convert=bool(cv_args)),
        grid=(n_items + 1,),
        in_specs=[
            pl.BlockSpec((tc, d), lambda g: (hid_item(g) // n_blocks, 0)),
            pl.BlockSpec((eb, d), lambda g: (hid_item(g) % n_blocks, 0)),
            pl.BlockSpec((d, eb), lambda g: (0, mix_item(g) % n_blocks)),
            tab_spec, tab_spec, row_spec, row_spec,
            chunk_spec,
            pl.BlockSpec((1, 1, d), lambda g: (mix_item(g) // n_blocks // per_batch, 0, 0)),
        ] + cv_in,
        out_specs=[chunk_spec] + cv_out,
        out_shape=[jax.ShapeDtypeStruct((t, d), F32)] + cv_shape,
        scratch_shapes=[pltpu.VMEM((d, tc), F32),
                        pltpu.VMEM((eb, tc), F32),
                        pltpu.VMEM((eb, tc), F32),
                        pltpu.VMEM((eb, tc), BF16)],
        compiler_params=_params("arbitrary"),
        name="peer_mix",
    )(xf, u, vt, r2, a2, lim, a1, h, gate, *cv_args)
    return out if cv_args else out[0]


def kernel(x, c, rel_bias, norm1_g, norm2_g, w_ada, b_ada, a_w_in, a_q_gain, a_k_gain,
           b_w_in, w_out, peer_w_query, peer_sub_keys, peer_u, peer_v):
    b, s, d = x.shape
    depth = w_ada.shape[0]
    n_groups = len(DIL_GROUPS)
    heads = d // HEAD_DIM

    c_pad = jnp.pad(c, ((0, ADALN_ROWS - b), (0, 0)))
    mod = adaln(c_pad, w_ada, b_ada, layers=depth)
    a_w = a_w_in.astype(BF16)
    b_w = w_o = None
    wq_t = jnp.swapaxes(peer_w_query, 1, 2).astype(BF16)
    keys = peer_sub_keys.astype(BF16)
    u = vt = None
    h = x.reshape(b * s, d)
    for i in range(depth):
        sh1, sc1, g1, sh2, sc2, g2 = (mod[i, :b, k * d:(k + 1) * d][:, None, :] for k in range(6))
        j = i // 2
        if i % 2 == 0:
            ones = jnp.ones((HEAD_DIM,), F32)
            gain = jnp.stack([jnp.tile(row, heads) for g in range(n_groups)
                              for row in (a_q_gain[j, g] / math.sqrt(HEAD_DIM), a_k_gain[j, g], ones)])
            assert i == 0
            qkvs, u, vt, b_w, w_o = in_proj_dilated(
                h, norm1_g[i][None], sh1, sc1, a_w, gain[:, None, :], peer_u, peer_v,
                (b_w_in.reshape(-1, b_w_in.shape[2]), w_out.reshape(-1, d)), layer=j, peer_layer=i, seq=s)
            b_w, w_o = b_w.reshape(b_w_in.shape), w_o.reshape(w_out.shape)
            outs, lses = [], []
            for g, (window, dil) in enumerate(DIL_GROUPS):
                bias = _band_bias(rel_bias[:, g * heads:(g + 1) * heads], window, dil)
                o, lse = dilated_group_attention(qkvs[g], bias, n_heads=heads)
                outs.append(o)
                lses.append(lse)
            h = out_proj_merge(outs, lses, w_o, h, g1, layer=i, seq=s)
        else:
            qkv = in_proj(h, norm1_g[i][None], sh1, sc1, b_w, layer=j, seq=s)
            y = stick_breaking_attention(qkv.reshape(b, s, -1), n_heads=heads)
            h = out_proj(y.reshape(b * s, d), w_o, h, g1, layer=i, seq=s)
        xf, r2, a2, lim, a1 = peer_select(h, norm2_g[i][None], sh2, sc2, wq_t, keys, layer=i, seq=s,
                                          rows=PEER_BLOCK_ROWS)
        if i + 1 < depth:
            h, u, vt = peer_mix(xf, u, vt, r2, a2, lim, a1, h, g2, seq=s,
                                next_tables=(peer_u, peer_v, i + 1))
        else:
            h = peer_mix(xf, u, vt, r2, a2, lim, a1, h, g2, seq=s)
    return h.reshape(b, s, d)
```
